```python
import jax, jax.numpy as jnp
from jax import lax
import numpy as np

D_MODEL = 1024
BATCH = 2
SEQ = 8192
DEPTH = 2
DEC_BATCH = 4
DEC_SEQ = 8192
PAST_LEN = 128

N_HEADS = 8
N_KV_HEADS = 2
HEAD_DIM = 128
Q_GROUPS = N_HEADS // N_KV_HEADS
D_ATTN_Q = N_HEADS * HEAD_DIM
D_ATTN_KV = N_KV_HEADS * HEAD_DIM
Q_BLOCK = 128
GRID_W = 64
ROPE_AXIS_HALF = HEAD_DIM // 4
ROPE_THETA = 10000.0
QK_EPS = 1e-6
D_CONV = D_MODEL
CONV_WIDTH = 3
CONV_PAD = (1, 1)
D_RNN = D_MODEL
LRU_CONV_WIDTH = 4
LRU_CONV_PAD = (1, 2)
LRU_BLOCKS = 8
LRU_BLOCK_W = D_RNN // LRU_BLOCKS
LRU_C = 8.0
N_BRANCHES = 3
IN_SPLITS = (D_ATTN_Q, D_ATTN_KV, D_ATTN_KV, D_CONV, D_CONV, D_CONV, D_RNN, D_RNN, N_BRANCHES * D_MODEL)
D_IN = D_ATTN_Q + 2 * D_ATTN_KV + 3 * D_CONV + 2 * D_RNN + N_BRANCHES * D_MODEL
MEM_TOKENS = 256
MEM_HEADS = 4
MEM_HEAD_DIM = D_MODEL // MEM_HEADS
N_EXPERTS = 32
TOP_K = 4
D_FF = D_MODEL // 2
SWIGLU_LIMIT = 7.0
SWIGLU_ALPHA = 1.702
MOE_BLOCK = 256
LN_EPS = 1e-5
DEEPNORM_ALPHA = (2 * DEPTH) ** 0.25
DEEPNORM_BETA = (8 * DEPTH) ** -0.25

kernel_name = "hybrid_bidir_encoder_conv_gqa_rglru_moe"


def layer_norm(x, g, b):
    xf = x.astype(jnp.float32)
    mu = jnp.mean(xf, axis=-1, keepdims=True)
    xc = xf - mu
    var = jnp.mean(xc * xc, axis=-1, keepdims=True)
    y = xc * lax.rsqrt(var + LN_EPS) * g.astype(jnp.float32) + b.astype(jnp.float32)
    return y.astype(x.dtype)


def rms_norm(x, g):
    xf = x.astype(jnp.float32)
    y = xf * lax.rsqrt(jnp.mean(xf * xf, axis=-1, keepdims=True) + QK_EPS) * g.astype(jnp.float32)
    return y.astype(x.dtype)


def split_cols(z, sizes):
    outs, start = [], 0
    for s in sizes:
        outs.append(z[..., start:start + s])
        start += s
    return outs


def depthwise_conv(x, w, pad):
    c = x.shape[-1]
    return lax.conv_general_dilated(x, w[:, None, :].astype(x.dtype), window_strides=(1,),
                                    padding=[pad], dimension_numbers=('NWC', 'WIO', 'NWC'),
                                    feature_group_count=c)


def axial_rope_tables(n_tokens):
    rows = n_tokens // GRID_W
    row = jnp.repeat(jnp.arange(rows, dtype=jnp.float32), GRID_W)
    col = jnp.tile(jnp.arange(GRID_W, dtype=jnp.float32), rows)
    inv_freq = ROPE_THETA ** (-jnp.arange(ROPE_AXIS_HALF, dtype=jnp.float32) / ROPE_AXIS_HALF)
    ang = jnp.stack([row[:, None] * inv_freq, col[:, None] * inv_freq], axis=1)
    return jnp.cos(ang), jnp.sin(ang)


def axial_rope(x, cos, sin):
    xr = x.astype(jnp.float32).reshape(*x.shape[:-1], 2, 2, ROPE_AXIS_HALF)
    x1, x2 = xr[..., 0, :], xr[..., 1, :]
    c = cos[None, :, None]
    s = sin[None, :, None]
    out = jnp.stack([x1 * c - x2 * s, x2 * c + x1 * s], axis=-2)
    return out.reshape(x.shape).astype(x.dtype)


def block_attention(q, k, v, cos, sin, q_norm_g, k_norm_g):
    b, s, _ = q.shape
    q = axial_rope(rms_norm(q.reshape(b, s, N_HEADS, HEAD_DIM), q_norm_g), cos, sin) * (HEAD_DIM ** -0.5)
    k = axial_rope(rms_norm(k.reshape(b, s, N_KV_HEADS, HEAD_DIM), k_norm_g), cos, sin)
    v = v.reshape(b, s, N_KV_HEADS, HEAD_DIM)
    nblk = s // Q_BLOCK
    qb = q.reshape(b, nblk, Q_BLOCK, N_KV_HEADS, Q_GROUPS, HEAD_DIM).transpose(1, 0, 2, 3, 4, 5)

    def one_block(qi):
        sc = jnp.einsum('bqkgd,bskd->bkgqs', qi, k).astype(jnp.float32)
        p = jnp.exp(sc - jnp.max(sc, axis=-1, keepdims=True))
        denom = jnp.sum(p, axis=-1, keepdims=True).transpose(0, 3, 1, 2, 4)
        o = jnp.einsum('bkgqs,bskd->bqkgd', p.astype(v.dtype), v)
        return (o.astype(jnp.float32) / denom).astype(v.dtype)

    o = lax.map(one_block, qb)
    return o.transpose(1, 0, 2, 3, 4, 5).reshape(b, s, D_ATTN_Q)


def rg_lru(xc, wa, ba, wx, bx, lam, reverse):
    b, s, _ = xc.shape
    xb = xc.reshape(b, s, LRU_BLOCKS, LRU_BLOCK_W)
    r = jax.nn.sigmoid(jnp.einsum('bshi,hij->bshj', xb, wa).reshape(b, s, D_RNN) + ba)
    i = jax.nn.sigmoid(jnp.einsum('bshi,hij->bshj', xb, wx).reshape(b, s, D_RNN) + bx)
    log_a = -LRU_C * r.astype(jnp.float32) * jax.nn.softplus(-lam.astype(jnp.float32))
    a = jnp.exp(log_a)
    u = jnp.sqrt(-jnp.expm1(2.0 * log_a)) * (i * xc).astype(jnp.float32)

    def combine(left, right):
        a1, b1 = left
        a2, b2 = right
        return a1 * a2, a2 * b1 + b2

    _, h = lax.associative_scan(combine, (a, u), reverse=reverse, axis=1)
    return h.astype(xc.dtype)


def mixing_sublayer(x, cos, sin, w_in, b_gate, q_norm_g, k_norm_g, conv_w, lru_conv_w, lru_conv_b,
                    lru_wa, lru_ba, lru_wx, lru_bx, lru_lam, w_mix_out):
    b, s, _ = x.shape
    z = jnp.einsum('bsd,de->bse', x, w_in)
    q, k, v, c_b, c_c, c_x, r_y, r_x, g = split_cols(z, IN_SPLITS)
    y_conv = c_b * depthwise_conv(c_c * c_x, conv_w, CONV_PAD)
    y_attn = block_attention(q, k, v, cos, sin, q_norm_g, k_norm_g)
    xc = depthwise_conv(r_x, lru_conv_w, LRU_CONV_PAD) + lru_conv_b
    h = (rg_lru(xc, lru_wa[0], lru_ba[0], lru_wx[0], lru_bx[0], lru_lam[0], False)
         + rg_lru(xc, lru_wa[1], lru_ba[1], lru_wx[1], lru_bx[1], lru_lam[1], True))
    y_rnn = h * jax.nn.gelu(r_y)
    gates = jax.nn.sigmoid(g.reshape(b, s, N_BRANCHES, D_MODEL) + b_gate)
    merged = gates[:, :, 0] * y_conv + gates[:, :, 1] * y_attn + gates[:, :, 2] * y_rnn
    return jnp.einsum('bsd,de->bse', merged, w_mix_out)


def memory_cross_attention(x, mem, xq_w, xkv_w, xo_w):
    b, s, _ = x.shape
    m = mem.shape[1]
    q = jnp.einsum('bsd,de->bse', x, xq_w).reshape(b, s, MEM_HEADS, MEM_HEAD_DIM) * (MEM_HEAD_DIM ** -0.5)
    kv = jnp.einsum('bmd,de->bme', mem, xkv_w)
    k = kv[..., :D_MODEL].reshape(b, m, MEM_HEADS, MEM_HEAD_DIM)
    v = kv[..., D_MODEL:].reshape(b, m, MEM_HEADS, MEM_HEAD_DIM)
    sc = jnp.einsum('bshd,bmhd->bhsm', q, k).astype(jnp.float32)
    p = jax.nn.softmax(sc, axis=-1).astype(v.dtype)
    o = jnp.einsum('bhsm,bmhd->bshd', p, v).reshape(b, s, D_MODEL)
    return jnp.einsum('bsd,de->bse', o, xo_w)


def moe(x, router_w, router_b, w1, b1, w2, b2):
    b, s, d = x.shape
    xt = x.reshape(b * s, d)
    n_assign = b * s * TOP_K
    logits = (xt @ router_w).astype(jnp.float32) + router_b.astype(jnp.float32)
    top_v, top_i = lax.top_k(logits, TOP_K)
    gate = jax.nn.softmax(top_v, axis=-1)
    flat_e = top_i.reshape(-1)
    order = jnp.argsort(flat_e)
    sorted_e = flat_e[order]
    tok = order // TOP_K
    counts = jnp.bincount(flat_e, length=N_EXPERTS).astype(jnp.int32)
    starts = jnp.cumsum(counts) - counts
    padded = (counts + MOE_BLOCK - 1) // MOE_BLOCK * MOE_BLOCK
    pad_ends = jnp.cumsum(padded)
    pad_starts = pad_ends - padded
    dest = pad_starts[sorted_e] + (jnp.arange(n_assign, dtype=jnp.int32) - starts[sorted_e])
    n_blocks = n_assign // MOE_BLOCK + N_EXPERTS
    xs = jnp.zeros((n_blocks * MOE_BLOCK, d), xt.dtype).at[dest].set(xt[tok])
    block_e = jnp.minimum(jnp.searchsorted(pad_ends, jnp.arange(n_blocks, dtype=jnp.int32) * MOE_BLOCK,
                                           side='right'), N_EXPERTS - 1).astype(jnp.int32)

    def expert_block(args):
        xb, e = args
        hcat = xb @ w1[e] + b1[e]
        glu = jnp.minimum(hcat[:, :D_FF], SWIGLU_LIMIT)
        lin = jnp.clip(hcat[:, D_FF:], -SWIGLU_LIMIT, SWIGLU_LIMIT)
        act = glu * jax.nn.sigmoid(SWIGLU_ALPHA * glu) * (lin + 1.0)
        return act @ w2[e] + b2[e]

    out_pad = lax.map(expert_block, (xs.reshape(n_blocks, MOE_BLOCK, d), block_e)).reshape(-1, d)
    out = out_pad[dest] * gate.reshape(-1)[order][:, None].astype(out_pad.dtype)
    y = jnp.zeros_like(xt).at[tok].add(out)
    return y.reshape(b, s, d)


def encode(x, mem, ln_in_g, ln_in_b, w_in, b_gate, q_norm_g, k_norm_g, conv_w, lru_conv_w, lru_conv_b,
           lru_wa, lru_ba, lru_wx, lru_bx, lru_lam, w_mix_out, ln1_g, ln1_b, xq_w, xkv_w, xo_w,
           ln2_g, ln2_b, router_w, router_b, w1, b1, w2, b2, ln3_g, ln3_b):
    cos, sin = axial_rope_tables(x.shape[1])
    x = layer_norm(x, ln_in_g, ln_in_b)
    for l in range(DEPTH):
        mix = mixing_sublayer(x, cos, sin, w_in[l], b_gate[l], q_norm_g[l], k_norm_g[l], conv_w[l],
                              lru_conv_w[l], lru_conv_b[l], lru_wa[l], lru_ba[l], lru_wx[l], lru_bx[l],
                              lru_lam[l], w_mix_out[l])
        x = layer_norm(DEEPNORM_ALPHA * x + mix, ln1_g[l], ln1_b[l])
        cross = memory_cross_attention(x, mem, xq_w[l], xkv_w[l], xo_w[l])
        x = layer_norm(DEEPNORM_ALPHA * x + cross, ln2_g[l], ln2_b[l])
        ff = moe(x, router_w[l], router_b[l], w1[l], b1[l], w2[l], b2[l])
        x = layer_norm(DEEPNORM_ALPHA * x + ff, ln3_g[l], ln3_b[l])
    return x


def setup_inputs(seed: int = 0) -> dict:
    key = jax.random.key(seed)
    ks = jax.random.split(key, 40)
    f = jnp.float32
    nrm = lambda k, shape, scale: jax.random.normal(k, shape, f) * scale
    gain = lambda k, shape: 1.0 + 0.02 * jax.random.normal(k, shape, f)
    u = jax.random.uniform(ks[17], (DEPTH, 2, D_RNN), f, minval=0.9, maxval=0.999)
    sa = u ** (1.0 / LRU_C)
    return {
        "x_prompt": nrm(ks[0], (BATCH, SEQ, D_MODEL), 1.0),
        "x_sample": nrm(ks[1], (DEC_BATCH, DEC_SEQ, D_MODEL), 1.0),
        "mem_prompt": nrm(ks[2], (BATCH, MEM_TOKENS, D_MODEL), 1.0),
        "mem_sample": nrm(ks[3], (DEC_BATCH, MEM_TOKENS, D_MODEL), 1.0),
        "ln_in_g": gain(ks[4], (D_MODEL,)),
        "ln_in_b": nrm(ks[5], (D_MODEL,), 0.02),
        "w_in": nrm(ks[6], (DEPTH, D_MODEL, D_IN), D_MODEL ** -0.5),
        "b_gate": nrm(ks[7], (DEPTH, N_BRANCHES, D_MODEL), 0.02),
        "q_norm_g": gain(ks[8], (DEPTH, HEAD_DIM)),
        "k_norm_g": gain(ks[9], (DEPTH, HEAD_DIM)),
        "conv_w": nrm(ks[10], (DEPTH, CONV_WIDTH, D_CONV), CONV_WIDTH ** -0.5),
        "lru_conv_w": nrm(ks[11], (DEPTH, LRU_CONV_WIDTH, D_RNN), LRU_CONV_WIDTH ** -0.5),
        "lru_conv_b": nrm(ks[12], (DEPTH, D_RNN), 0.02),
        "lru_wa": nrm(ks[13], (DEPTH, 2, LRU_BLOCKS, LRU_BLOCK_W, LRU_BLOCK_W), LRU_BLOCK_W ** -0.5),
        "lru_ba": nrm(ks[14], (DEPTH, 2, D_RNN), 0.02),
        "lru_wx": nrm(ks[15], (DEPTH, 2, LRU_BLOCKS, LRU_BLOCK_W, LRU_BLOCK_W), LRU_BLOCK_W ** -0.5),
        "lru_bx": nrm(ks[16], (DEPTH, 2, D_RNN), 0.02),
        "lru_lam": jnp.log(sa) - jnp.log1p(-sa),
        "w_mix_out": nrm(ks[18], (DEPTH, D_MODEL, D_MODEL), DEEPNORM_BETA * D_MODEL ** -0.5),
        "ln1_g": gain(ks[19], (DEPTH, D_MODEL)),
        "ln1_b": nrm(ks[20], (DEPTH, D_MODEL), 0.02),
        "xq_w": nrm(ks[21], (DEPTH, D_MODEL, D_MODEL), D_MODEL ** -0.5),
        "xkv_w": nrm(ks[22], (DEPTH, D_MODEL, 2 * D_MODEL), D_MODEL ** -0.5),
        "xo_w": nrm(ks[23], (DEPTH, D_MODEL, D_MODEL), DEEPNORM_BETA * D_MODEL ** -0.5),
        "ln2_g": gain(ks[24], (DEPTH, D_MODEL)),
        "ln2_b": nrm(ks[25], (DEPTH, D_MODEL), 0.02),
        "router_w": nrm(ks[26], (DEPTH, D_MODEL, N_EXPERTS), D_MODEL ** -0.5),
        "router_b": nrm(ks[27], (DEPTH, N_EXPERTS), 0.01),
        "w1": nrm(ks[28], (DEPTH, N_EXPERTS, D_MODEL, 2 * D_FF), D_MODEL ** -0.5),
        "b1": nrm(ks[29], (DEPTH, N_EXPERTS, 2 * D_FF), 0.01),
        "w2": nrm(ks[30], (DEPTH, N_EXPERTS, D_FF, D_MODEL), DEEPNORM_BETA * D_FF ** -0.5),
        "b2": nrm(ks[31], (DEPTH, N_EXPERTS, D_MODEL), 0.01),
        "ln3_g": gain(ks[32], (DEPTH, D_MODEL)),
        "ln3_b": nrm(ks[33], (DEPTH, D_MODEL), 0.02),
    }


def reference(x_prompt, x_sample, mem_prompt, mem_sample, ln_in_g, ln_in_b, w_in, b_gate, q_norm_g, k_norm_g,
              conv_w, lru_conv_w, lru_conv_b, lru_wa, lru_ba, lru_wx, lru_bx, lru_lam, w_mix_out, ln1_g, ln1_b,
              xq_w, xkv_w, xo_w, ln2_g, ln2_b, router_w, router_b, w1, b1, w2, b2, ln3_g, ln3_b):
    y_prompt = encode(x_prompt, mem_prompt, ln_in_g, ln_in_b, w_in, b_gate, q_norm_g, k_norm_g, conv_w,
                      lru_conv_w, lru_conv_b, lru_wa, lru_ba, lru_wx, lru_bx, lru_lam, w_mix_out, ln1_g, ln1_b,
                      xq_w, xkv_w, xo_w, ln2_g, ln2_b, router_w, router_b, w1, b1, w2, b2, ln3_g, ln3_b)
    y_sample = encode(x_sample, mem_sample, ln_in_g, ln_in_b, w_in, b_gate, q_norm_g, k_norm_g, conv_w,
                      lru_conv_w, lru_conv_b, lru_wa, lru_ba, lru_wx, lru_bx, lru_lam, w_mix_out, ln1_g, ln1_b,
                      xq_w, xkv_w, xo_w, ln2_g, ln2_b, router_w, router_b, w1, b1, w2, b2, ln3_g, ln3_b)
    return (y_prompt, y_sample)
```

```python
import functools

import jax
import jax.numpy as jnp
from jax import lax
from jax.experimental import pallas as pl
from jax.experimental.pallas import tpu as pltpu

F32 = jnp.float32
BF16 = jnp.bfloat16
I32 = jnp.int32

D_MODEL = 1024
DEPTH = 2
N_HEADS = 8
N_KV_HEADS = 2
HEAD_DIM = 128
Q_GROUPS = N_HEADS // N_KV_HEADS
D_ATTN_Q = N_HEADS * HEAD_DIM
D_ATTN_KV = N_KV_HEADS * HEAD_DIM
GRID_W = 64
ROPE_AXIS_HALF = HEAD_DIM // 4
ROPE_THETA = 10000.0
QK_EPS = 1e-6
LRU_BLOCKS = 8
LRU_BLOCK_W = D_MODEL // LRU_BLOCKS
LRU_C = 8.0
MEM_HEADS = 4
MEM_HEAD_DIM = D_MODEL // MEM_HEADS
N_EXPERTS = 32
TOP_K = 4
D_FF = D_MODEL // 2
SWIGLU_LIMIT = 7.0
SWIGLU_ALPHA = 1.702
LN_EPS = 1e-5
DEEPNORM_ALPHA = (2 * DEPTH) ** 0.25

LANES = 128
SUBLANES = 8
BF16_SUBLANES = 16
VMEM_LIMIT = 56 * 1024 * 1024

PROJ_TM = 512
ATTN_TQ = 256
ATTN_TK = 512
TOK_TILE = 256
LRU_CHUNK = 512
LRU_SUBSEQ = 8
LRU_PAD = 8
MOE_GROUP = SUBLANES
MOE_BLOCK = 256
MOE_TILE_ROWS = TOK_TILE * TOP_K + N_EXPERTS * MOE_GROUP
NEG_BIG = -1e30


def _cparams(sem, vmem=VMEM_LIMIT):
    return pltpu.CompilerParams(dimension_semantics=sem, vmem_limit_bytes=vmem)


def _layer_norm(x, g, b):
    mu = jnp.mean(x, axis=-1, keepdims=True)
    xc = x - mu
    var = jnp.mean(xc * xc, axis=-1, keepdims=True)
    return xc * lax.rsqrt(var + LN_EPS) * g + b


def _sigmoid(x):
    return 1.0 / (1.0 + jnp.exp(-x))


def _dot(a, b):
    return jnp.dot(a, b, preferred_element_type=F32)


def _dot_nt(a, b):
    return lax.dot_general(a, b, (((1,), (1,)), ((), ())), preferred_element_type=F32)


def _ln_kernel(x_ref, g_ref, b_ref, o_ref):
    o_ref[...] = _layer_norm(x_ref[...], g_ref[...], b_ref[...])


def _ln_call(x, g, b):
    t = x.shape[0]
    tm = PROJ_TM
    return pl.pallas_call(
        _ln_kernel,
        grid=(t // tm,),
        in_specs=[pl.BlockSpec((tm, D_MODEL), lambda i: (i, 0)),
                  pl.BlockSpec((1, D_MODEL), lambda i: (0, 0)),
                  pl.BlockSpec((1, D_MODEL), lambda i: (0, 0))],
        out_specs=pl.BlockSpec((tm, D_MODEL), lambda i: (i, 0)),
        out_shape=jax.ShapeDtypeStruct((t, D_MODEL), F32),
        compiler_params=_cparams(("parallel",)),
        name="ln_in",
    )(x, g.reshape(1, -1), b.reshape(1, -1))


def _qkv_kernel(x_ref, w_ref, cos_ref, sin_ref, qg_ref, kg_ref, q_ref, k_ref, v_ref):
    x = x_ref[...].astype(BF16)
    z = _dot(x, w_ref[...])
    c = cos_ref[...]
    s = sin_ref[...]
    lane = lax.broadcasted_iota(I32, c.shape, 1)
    first_half = (lane % (2 * ROPE_AXIS_HALF)) < ROPE_AXIS_HALF

    def prep(zh, g, scale):
        ms = jnp.mean(zh * zh, axis=-1, keepdims=True)
        y = zh * lax.rsqrt(ms + QK_EPS) * g
        partner = jnp.where(first_half,
                            pltpu.roll(y, HEAD_DIM - ROPE_AXIS_HALF, 1),
                            pltpu.roll(y, ROPE_AXIS_HALF, 1))
        return (y * c + partner * s) * scale

    qg = qg_ref[...]
    kg = kg_ref[...]
    for h in range(N_HEADS):
        sl = slice(h * HEAD_DIM, (h + 1) * HEAD_DIM)
        q_ref[:, sl] = prep(z[:, sl], qg, HEAD_DIM ** -0.5).astype(q_ref.dtype)
    for h in range(N_KV_HEADS):
        sl = slice(h * HEAD_DIM, (h + 1) * HEAD_DIM)
        zsl = slice(D_ATTN_Q + h * HEAD_DIM, D_ATTN_Q + (h + 1) * HEAD_DIM)
        k_ref[:, sl] = prep(z[:, zsl], kg, 1.0).astype(k_ref.dtype)
    v_ref[...] = z[:, D_ATTN_Q + D_ATTN_KV:].astype(v_ref.dtype)


def _qkv_call(x, w, cos_t, sin_t, qg, kg, seq):
    t = x.shape[0]
    tm = PROJ_TM
    nseq = seq // tm
    n = w.shape[1]
    return pl.pallas_call(
        _qkv_kernel,
        grid=(t // tm,),
        in_specs=[pl.BlockSpec((tm, D_MODEL), lambda i: (i, 0)),
                  pl.BlockSpec((D_MODEL, n), lambda i: (0, 0)),
                  pl.BlockSpec((tm, HEAD_DIM), lambda i: (i % nseq, 0)),
                  pl.BlockSpec((tm, HEAD_DIM), lambda i: (i % nseq, 0)),
                  pl.BlockSpec((1, HEAD_DIM), lambda i: (0, 0)),
                  pl.BlockSpec((1, HEAD_DIM), lambda i: (0, 0))],
        out_specs=[pl.BlockSpec((tm, D_ATTN_Q), lambda i: (i, 0)),
                   pl.BlockSpec((tm, D_ATTN_KV), lambda i: (i, 0)),
                   pl.BlockSpec((tm, D_ATTN_KV), lambda i: (i, 0))],
        out_shape=[jax.ShapeDtypeStruct((t, D_ATTN_Q), BF16),
                   jax.ShapeDtypeStruct((t, D_ATTN_KV), BF16),
                   jax.ShapeDtypeStruct((t, D_ATTN_KV), BF16)],
        compiler_params=_cparams(("parallel",)),
        name="qkv_proj",
    )(x, w, cos_t, sin_t, qg.reshape(1, -1), kg.reshape(1, -1))


def _convproj_kernel(x_ref, w_ref, cb_ref, p_ref):
    x = x_ref[...].astype(BF16)
    z = _dot(x, w_ref[...])
    cb_ref[...] = z[:, :D_MODEL].astype(cb_ref.dtype)
    p_ref[...] = (z[:, D_MODEL:2 * D_MODEL] * z[:, 2 * D_MODEL:]).astype(p_ref.dtype)


def _convproj_call(x, w):
    t = x.shape[0]
    tm = PROJ_TM
    return pl.pallas_call(
        _convproj_kernel,
        grid=(t // tm,),
        in_specs=[pl.BlockSpec((tm, D_MODEL), lambda i: (i, 0)),
                  pl.BlockSpec((D_MODEL, 3 * D_MODEL), lambda i: (0, 0))],
        out_specs=[pl.BlockSpec((tm, D_MODEL), lambda i: (i, 0)),
                   pl.BlockSpec((tm, D_MODEL), lambda i: (i, 0))],
        out_shape=[jax.ShapeDtypeStruct((t, D_MODEL), BF16),
                   jax.ShapeDtypeStruct((t, D_MODEL), BF16)],
        compiler_params=_cparams(("parallel",)),
        name="conv_proj",
    )(x, w)


def _rnnproj_kernel(x_ref, w_ref, gy_ref, rx_ref):
    x = x_ref[...].astype(BF16)
    z = _dot(x, w_ref[...])
    gy_ref[...] = jax.nn.gelu(z[:, :D_MODEL]).astype(gy_ref.dtype)
    rx_ref[...] = z[:, D_MODEL:]


def _rnnproj_call(x, w):
    t = x.shape[0]
    tm = PROJ_TM
    return pl.pallas_call(
        _rnnproj_kernel,
        grid=(t // tm,),
        in_specs=[pl.BlockSpec((tm, D_MODEL), lambda i: (i, 0)),
                  pl.BlockSpec((D_MODEL, 2 * D_MODEL), lambda i: (0, 0))],
        out_specs=[pl.BlockSpec((tm, D_MODEL), lambda i: (i, 0)),
                   pl.BlockSpec((tm, D_MODEL), lambda i: (i, 0))],
        out_shape=[jax.ShapeDtypeStruct((t, D_MODEL), BF16),
                   jax.ShapeDtypeStruct((t, D_MODEL), F32)],
        compiler_params=_cparams(("parallel",)),
        name="rnn_proj",
    )(x, w)


def _gateproj_kernel(x_ref, w_ref, b_ref, g_ref):
    x = x_ref[...].astype(BF16)
    z = _dot(x, w_ref[...]) + b_ref[...]
    g_ref[...] = _sigmoid(z).astype(g_ref.dtype)


def _gateproj_call(x, w, b):
    t = x.shape[0]
    tm = PROJ_TM
    return pl.pallas_call(
        _gateproj_kernel,
        grid=(t // tm,),
        in_specs=[pl.BlockSpec((tm, D_MODEL), lambda i: (i, 0)),
                  pl.BlockSpec((D_MODEL, 3 * D_MODEL), lambda i: (0, 0)),
                  pl.BlockSpec((1, 3 * D_MODEL), lambda i: (0, 0))],
        out_specs=pl.BlockSpec((tm, 3 * D_MODEL), lambda i: (i, 0)),
        out_shape=jax.ShapeDtypeStruct((t, 3 * D_MODEL), BF16),
        compiler_params=_cparams(("parallel",)),
        name="gate_proj",
    )(x, w, b.reshape(1, -1))


def _attn_kernel(q_ref, k_ref, v_ref, o_ref, *, tk):
    tq = q_ref.shape[1]
    nk = k_ref.shape[1] // tk
    for g in range(Q_GROUPS):
        sl = slice(g * HEAD_DIM, (g + 1) * HEAD_DIM)
        q = q_ref[0, :, sl]

        def body(j, carry, q=q):
            m, l, acc = carry
            start = pl.multiple_of(j * tk, tk)
            ks = k_ref[0, pl.ds(start, tk), :]
            vs = v_ref[0, pl.ds(start, tk), :]
            s = _dot_nt(q, ks)
            m_new = jnp.maximum(m, jnp.max(s, axis=-1, keepdims=True))
            alpha = jnp.exp(m - m_new)
            p = jnp.exp(s - m_new)
            l = alpha * l + jnp.sum(p, axis=-1, keepdims=True)
            acc = alpha * acc + _dot(p.astype(BF16), vs)
            return m_new, l, acc

        init = (jnp.full((tq, 1), NEG_BIG, F32), jnp.zeros((tq, 1), F32),
                jnp.zeros((tq, HEAD_DIM), F32))
        _, l, acc = lax.fori_loop(0, nk, body, init)
        o_ref[0, :, sl] = (acc / l).astype(o_ref.dtype)


def _attn_call(q, k, v):
    b, s, _ = q.shape
    tq = min(ATTN_TQ, s)
    tk = min(ATTN_TK, s)
    gw = Q_GROUPS * HEAD_DIM
    return pl.pallas_call(
        functools.partial(_attn_kernel, tk=tk),
        grid=(b, N_KV_HEADS, s // tq),
        in_specs=[pl.BlockSpec((1, tq, gw), lambda bi, h, i: (bi, i, h)),
                  pl.BlockSpec((1, s, HEAD_DIM), lambda bi, h, i: (bi, 0, h)),
                  pl.BlockSpec((1, s, HEAD_DIM), lambda bi, h, i: (bi, 0, h))],
        out_specs=pl.BlockSpec((1, tq, gw), lambda bi, h, i: (bi, i, h)),
        out_shape=jax.ShapeDtypeStruct((b, s, D_ATTN_Q), BF16),
        compiler_params=_cparams(("parallel", "parallel", "parallel")),
        name="flash_attn",
    )(q, k, v)


def _lru_kernel(x_ref, cw_ref, cb_ref, wa_ref, wx_ref, ba_ref, bx_ref, lam_ref, o_ref,
                xp_s, a_s, u_s, *, seq, chunk):
    sub = seq // LRU_SUBSEQ
    stride = sub + LRU_PAD
    cps = sub // chunk
    nchunks = seq // chunk

    zeros8 = jnp.zeros((SUBLANES, LANES), F32)
    xp_s[0:SUBLANES, :] = zeros8
    xp_s[seq + SUBLANES:seq + 2 * SUBLANES, :] = zeros8
    xp_s[SUBLANES:seq + SUBLANES, :] = x_ref[0]
    cw = cw_ref[...]
    cb = cb_ref[...]

    for d in range(2):
        lam = lam_ref[d:d + 1, :]
        neg = -lam
        softplus = jnp.maximum(neg, 0.0) + jnp.log1p(jnp.exp(-jnp.abs(neg)))
        decay = -LRU_C * softplus
        wa = wa_ref[d, 0]
        wx = wx_ref[d, 0]
        ba = ba_ref[d:d + 1, :]
        bx = bx_ref[d:d + 1, :]

        def gate_chunk(c, _, decay=decay, wa=wa, wx=wx, ba=ba, bx=bx):
            base = pl.multiple_of(c * chunk, chunk)
            xc = cb
            for k in range(4):
                xc = xc + cw[k:k + 1, :] * xp_s[pl.ds(base + SUBLANES - 1 + k, chunk), :]
            xb = xc.astype(BF16)
            r = _sigmoid(_dot(xb, wa) + ba)
            i = _sigmoid(_dot(xb, wx) + bx)
            a = jnp.exp(decay * r)
            u = jnp.sqrt(1.0 - a * a) * (i * xc)
            dst = pl.multiple_of((c // cps) * stride + (c % cps) * chunk, SUBLANES)
            a_s[pl.ds(dst, chunk), :] = a
            u_s[pl.ds(dst, chunk), :] = u
            return 0

        lax.fori_loop(0, nchunks, gate_chunk, 0)

        def step(tt, carry, d=d):
            h, p = carry
            t = tt if d == 0 else sub - 1 - tt
            av = a_s[pl.ds(t, LRU_SUBSEQ, stride=stride), :]
            uv = u_s[pl.ds(t, LRU_SUBSEQ, stride=stride), :]
            h = av * h + uv
            p = av * p
            u_s[pl.ds(t, LRU_SUBSEQ, stride=stride), :] = h
            a_s[pl.ds(t, LRU_SUBSEQ, stride=stride), :] = p
            return h, p

        h_end, p_end = lax.fori_loop(0, sub, step, (zeros8, jnp.ones((SUBLANES, LANES), F32)))

        order = list(range(LRU_SUBSEQ)) if d == 0 else list(range(LRU_SUBSEQ - 1, -1, -1))
        carry_in = {}
        state = jnp.zeros((1, LANES), F32)
        for j in order:
            carry_in[j] = state
            state = p_end[j:j + 1, :] * state + h_end[j:j + 1, :]

        rows = min(sub, chunk)
        for j in range(LRU_SUBSEQ):
            for c in range(sub // rows):
                src = j * stride + c * rows
                dst = j * sub + c * rows
                val = u_s[src:src + rows, :] + a_s[src:src + rows, :] * carry_in[j]
                if d == 0:
                    o_ref[0, dst:dst + rows, :] = val
                else:
                    o_ref[0, dst:dst + rows, :] = o_ref[0, dst:dst + rows, :] + val


def _lru_call(rx, cw, cb, wa, wx, ba, bx, lam):
    b, s, _ = rx.shape
    chunk = min(LRU_CHUNK, s // LRU_SUBSEQ)
    scr_rows = LRU_SUBSEQ * (s // LRU_SUBSEQ + LRU_PAD)
    w = LRU_BLOCK_W
    return pl.pallas_call(
        functools.partial(_lru_kernel, seq=s, chunk=chunk),
        grid=(b, LRU_BLOCKS),
        in_specs=[pl.BlockSpec((1, s, w), lambda bi, c: (bi, 0, c)),
                  pl.BlockSpec((4, w), lambda bi, c: (0, c)),
                  pl.BlockSpec((1, w), lambda bi, c: (0, c)),
                  pl.BlockSpec((2, 1, w, w), lambda bi, c: (0, c, 0, 0)),
                  pl.BlockSpec((2, 1, w, w), lambda bi, c: (0, c, 0, 0)),
                  pl.BlockSpec((2, w), lambda bi, c: (0, c)),
                  pl.BlockSpec((2, w), lambda bi, c: (0, c)),
                  pl.BlockSpec((2, w), lambda bi, c: (0, c))],
        out_specs=pl.BlockSpec((1, s, w), lambda bi, c: (bi, 0, c)),
        out_shape=jax.ShapeDtypeStruct((b, s, D_MODEL), F32),
        scratch_shapes=[pltpu.VMEM((s + 2 * SUBLANES, w), F32),
                        pltpu.VMEM((scr_rows, w), F32),
                        pltpu.VMEM((scr_rows, w), F32)],
        compiler_params=_cparams(("parallel", "parallel")),
        name="rg_lru",
    )(rx, cw, cb.reshape(1, -1), wa, wx, ba, bx, lam)


def _merge_kernel(x_ref, cb_ref, p_ref, pprev_ref, pnext_ref, ya_ref, h_ref, gy_ref, g_ref,
                  cw_ref, w_ref, lg_ref, lb_ref, o_ref, *, seq):
    tm = x_ref.shape[0]
    t0 = pl.program_id(0) * tm
    p = p_ref[...].astype(F32)
    row = lax.broadcasted_iota(I32, p.shape, 0)
    has_prev = ((t0 % seq) != 0).astype(F32)
    has_next = (((t0 + tm) % seq) != 0).astype(F32)
    prev_row = pprev_ref[BF16_SUBLANES - 1:BF16_SUBLANES, :].astype(F32) * has_prev
    next_row = pnext_ref[0:1, :].astype(F32) * has_next
    p_dn = jnp.where(row == 0, prev_row, pltpu.roll(p, 1, 0))
    p_up = jnp.where(row == tm - 1, next_row, pltpu.roll(p, tm - 1, 0))
    cw = cw_ref[...]
    y_conv = cb_ref[...].astype(F32) * (cw[0:1, :] * p_dn + cw[1:2, :] * p + cw[2:3, :] * p_up)
    y_rnn = h_ref[...] * gy_ref[...].astype(F32)
    merged = (g_ref[:, :D_MODEL].astype(F32) * y_conv
              + g_ref[:, D_MODEL:2 * D_MODEL].astype(F32) * ya_ref[...].astype(F32)
              + g_ref[:, 2 * D_MODEL:].astype(F32) * y_rnn)
    mix = _dot(merged.astype(BF16), w_ref[...])
    o_ref[...] = _layer_norm(DEEPNORM_ALPHA * x_ref[...] + mix, lg_ref[...], lb_ref[...])


def _merge_call(x, cb, p, ya, h, gy, g, cw, w, lg, lb, seq):
    t = x.shape[0]
    tm = TOK_TILE
    hb = BF16_SUBLANES
    per = tm // hb
    last = t // hb - 1
    row_spec = pl.BlockSpec((tm, D_MODEL), lambda i: (i, 0))
    vec_spec = pl.BlockSpec((1, D_MODEL), lambda i: (0, 0))
    return pl.pallas_call(
        functools.partial(_merge_kernel, seq=seq),
        grid=(t // tm,),
        in_specs=[row_spec, row_spec, row_spec,
                  pl.BlockSpec((hb, D_MODEL), lambda i: (jnp.maximum(i * per - 1, 0), 0)),
                  pl.BlockSpec((hb, D_MODEL), lambda i: (jnp.minimum((i + 1) * per, last), 0)),
                  row_spec, row_spec, row_spec,
                  pl.BlockSpec((tm, 3 * D_MODEL), lambda i: (i, 0)),
                  pl.BlockSpec((3, D_MODEL), lambda i: (0, 0)),
                  pl.BlockSpec((D_MODEL, D_MODEL), lambda i: (0, 0)),
                  vec_spec, vec_spec],
        out_specs=row_spec,
        out_shape=jax.ShapeDtypeStruct((t, D_MODEL), F32),
        compiler_params=_cparams(("parallel",)),
        name="merge_outproj_ln1",
    )(x, cb, p, p, p, ya, h, gy, g, cw, w, lg.reshape(1, -1), lb.reshape(1, -1))


def _kv_kernel(m_ref, w_ref, k_ref, v_ref):
    z = _dot(m_ref[0].astype(BF16), w_ref[...])
    k_ref[0] = z[:, :D_MODEL].astype(k_ref.dtype)
    v_ref[0] = z[:, D_MODEL:].astype(v_ref.dtype)


def _kv_call(mem, w):
    b, m, _ = mem.shape
    return pl.pallas_call(
        _kv_kernel,
        grid=(b,),
        in_specs=[pl.BlockSpec((1, m, D_MODEL), lambda i: (i, 0, 0)),
                  pl.BlockSpec((D_MODEL, 2 * D_MODEL), lambda i: (0, 0))],
        out_specs=[pl.BlockSpec((1, m, D_MODEL), lambda i: (i, 0, 0)),
                   pl.BlockSpec((1, m, D_MODEL), lambda i: (i, 0, 0))],
        out_shape=[jax.ShapeDtypeStruct((b, m, D_MODEL), BF16),
                   jax.ShapeDtypeStruct((b, m, D_MODEL), BF16)],
        compiler_params=_cparams(("parallel",)),
        name="mem_kv_proj",
    )(mem, w)


def _cross_kernel(x_ref, k_ref, v_ref, wq_ref, wo_ref, lg_ref, lb_ref, rw_ref, rb_ref,
                  x2_ref, idx_ref, gate_ref, cnt_ref):
    x = x_ref[...]
    q = _dot(x.astype(BF16), wq_ref[...]) * (MEM_HEAD_DIM ** -0.5)
    outs = []
    for h in range(MEM_HEADS):
        sl = slice(h * MEM_HEAD_DIM, (h + 1) * MEM_HEAD_DIM)
        s = _dot_nt(q[:, sl].astype(BF16), k_ref[0, :, sl])
        m = jnp.max(s, axis=-1, keepdims=True)
        p = jnp.exp(s - m)
        p = p / jnp.sum(p, axis=-1, keepdims=True)
        outs.append(_dot(p.astype(BF16), v_ref[0, :, sl]))
    o = jnp.concatenate(outs, axis=1)
    cross = _dot(o.astype(BF16), wo_ref[...])
    x2 = _layer_norm(DEEPNORM_ALPHA * x + cross, lg_ref[...], lb_ref[...])
    x2_ref[...] = x2

    logits = _dot(x2.astype(BF16), rw_ref[...]) + rb_ref[...]
    lane = lax.broadcasted_iota(I32, logits.shape, 1)
    idx_acc = jnp.zeros(logits.shape, I32)
    val_acc = jnp.zeros(logits.shape, F32)
    member = jnp.zeros(logits.shape, F32)
    top0 = None
    for k in range(TOP_K):
        m = jnp.max(logits, axis=-1, keepdims=True)
        pick = jnp.min(jnp.where(logits == m, lane, LANES), axis=-1, keepdims=True)
        hit = lane == pick
        if k == 0:
            top0 = m
        idx_acc = jnp.where(lane == k, pick, idx_acc)
        val_acc = jnp.where(lane == k, jnp.exp(m - top0), val_acc)
        member = member + hit.astype(F32)
        logits = jnp.where(hit, NEG_BIG, logits)
    idx_ref[...] = idx_acc
    gate_ref[...] = val_acc / jnp.sum(val_acc, axis=-1, keepdims=True)
    cnt_ref[0] = jnp.sum(member, axis=0, keepdims=True).astype(I32)


def _cross_call(x, k, v, wq, wo, lg, lb, rw, rb, seq):
    t = x.shape[0]
    tm = TOK_TILE
    per_seq = seq // tm
    m = k.shape[1]
    row_spec = pl.BlockSpec((tm, D_MODEL), lambda i: (i, 0))
    vec_spec = pl.BlockSpec((1, D_MODEL), lambda i: (0, 0))
    lane_spec = pl.BlockSpec((tm, LANES), lambda i: (i, 0))
    return pl.pallas_call(
        _cross_kernel,
        grid=(t // tm,),
        in_specs=[row_spec,
                  pl.BlockSpec((1, m, D_MODEL), lambda i: (i // per_seq, 0, 0)),
                  pl.BlockSpec((1, m, D_MODEL), lambda i: (i // per_seq, 0, 0)),
                  pl.BlockSpec((D_MODEL, D_MODEL), lambda i: (0, 0)),
                  pl.BlockSpec((D_MODEL, D_MODEL), lambda i: (0, 0)),
                  vec_spec, vec_spec,
                  pl.BlockSpec((D_MODEL, LANES), lambda i: (0, 0)),
                  pl.BlockSpec((1, LANES), lambda i: (0, 0))],
        out_specs=[row_spec, lane_spec, lane_spec,
                   pl.BlockSpec((1, 1, LANES), lambda i: (i, 0, 0))],
        out_shape=[jax.ShapeDtypeStruct((t, D_MODEL), F32),
                   jax.ShapeDtypeStruct((t, LANES), I32),
                   jax.ShapeDtypeStruct((t, LANES), F32),
                   jax.ShapeDtypeStruct((t // tm, 1, LANES), I32)],
        compiler_params=_cparams(("parallel",)),
        name="cross_attn_ln2_router",
    )(x, k, v, wq, wo, lg.reshape(1, -1), lb.reshape(1, -1), rw, rb)


def _group_copy(src, dst, sem, s_grp, d_grp):
    return pltpu.make_async_copy(
        src.at[pl.ds(pl.multiple_of(s_grp * MOE_GROUP, MOE_GROUP), MOE_GROUP), :],
        dst.at[pl.ds(pl.multiple_of(d_grp * MOE_GROUP, MOE_GROUP), MOE_GROUP), :], sem)


def _moe_sort_kernel(off8_ref, len8_ref, dst8_ref, x_ref, idx_ref, offv_ref, dest_ref, xs_hbm,
                     xs_s, sem):
    s = pl.program_id(0)
    tm = x_ref.shape[0]
    idx = idx_ref[...]
    e_iota = lax.broadcasted_iota(I32, (N_EXPERTS, tm), 0)
    member = jnp.zeros((N_EXPERTS, tm), F32)
    for k in range(TOP_K):
        member = member + (e_iota == idx[k:k + 1, :]).astype(F32)
    earlier = (lax.broadcasted_iota(I32, (tm, tm), 0)
               < lax.broadcasted_iota(I32, (tm, tm), 1)).astype(BF16)
    rank = _dot(member.astype(BF16), earlier)
    pos = rank + offv_ref[0][:, 0:1]
    r_iota = lax.broadcasted_iota(I32, (MOE_TILE_ROWS, tm), 0)
    onehot = jnp.zeros((MOE_TILE_ROWS, tm), F32)
    dests = []
    for k in range(TOP_K):
        dk = jnp.sum(jnp.where(e_iota == idx[k:k + 1, :], pos, 0.0), axis=0, keepdims=True)
        dk = dk.astype(I32)
        dests.append(dk)
        onehot = onehot + (r_iota == dk).astype(F32)
    dest_ref[...] = jnp.concatenate(dests, axis=0)
    xs_s[...] = _dot(onehot.astype(BF16), x_ref[...].astype(BF16))

    def per_expert(e, total):
        n = len8_ref[s * N_EXPERTS + e]
        src0 = off8_ref[s * N_EXPERTS + e]
        dst0 = dst8_ref[s * N_EXPERTS + e]

        def issue(i, _):
            _group_copy(xs_s, xs_hbm, sem, src0 + i, dst0 + i).start()
            return 0

        lax.fori_loop(0, n, issue, 0)
        return total + n

    total = lax.fori_loop(0, N_EXPERTS, per_expert, 0)

    def drain(i, _):
        _group_copy(xs_s, xs_hbm, sem, 0, 0).wait()
        return 0

    lax.fori_loop(0, total, drain, 0)


def _moe_sort_call(x2, idx_t, offv, off8, len8, dst8, n_rows):
    t = x2.shape[0]
    tm = TOK_TILE
    grid_spec = pltpu.PrefetchScalarGridSpec(
        num_scalar_prefetch=3,
        grid=(t // tm,),
        in_specs=[pl.BlockSpec((tm, D_MODEL), lambda i, *_: (i, 0)),
                  pl.BlockSpec((TOP_K, tm), lambda i, *_: (0, i)),
                  pl.BlockSpec((1, N_EXPERTS, LANES), lambda i, *_: (i, 0, 0))],
        out_specs=[pl.BlockSpec((TOP_K, tm), lambda i, *_: (0, i)),
                   pl.BlockSpec(memory_space=pl.ANY)],
        scratch_shapes=[pltpu.VMEM((MOE_TILE_ROWS, D_MODEL), F32),
                        pltpu.SemaphoreType.DMA(())],
    )
    return pl.pallas_call(
        _moe_sort_kernel,
        grid_spec=grid_spec,
        out_shape=[jax.ShapeDtypeStruct((TOP_K, t), I32),
                   jax.ShapeDtypeStruct((n_rows, D_MODEL), F32)],
        compiler_params=_cparams(("arbitrary",)),
        name="moe_sort",
    )(off8, len8, dst8, x2, idx_t, offv)


def _moe_ffn_kernel(be_ref, nb_ref, xs_ref, w1_ref, b1_ref, w2_ref, b2_ref, o_ref):
    @pl.when(pl.program_id(0) < nb_ref[0])
    def _():
        hcat = _dot(xs_ref[...].astype(BF16), w1_ref[0]) + b1_ref[0]
        glu = jnp.minimum(hcat[:, :D_FF], SWIGLU_LIMIT)
        lin = jnp.clip(hcat[:, D_FF:], -SWIGLU_LIMIT, SWIGLU_LIMIT)
        act = glu * _sigmoid(SWIGLU_ALPHA * glu) * (lin + 1.0)
        o_ref[...] = _dot(act.astype(BF16), w2_ref[0]) + b2_ref[0]


def _moe_ffn_call(xs, block_e, nb_used, w1, b1, w2, b2):
    n_rows = xs.shape[0]
    nb = n_rows // MOE_BLOCK

    def blk(i, be, nbu):
        return (jnp.minimum(i, nbu[0] - 1), 0)

    grid_spec = pltpu.PrefetchScalarGridSpec(
        num_scalar_prefetch=2,
        grid=(nb,),
        in_specs=[pl.BlockSpec((MOE_BLOCK, D_MODEL), blk),
                  pl.BlockSpec((1, D_MODEL, 2 * D_FF), lambda i, be, nbu: (be[i], 0, 0)),
                  pl.BlockSpec((1, 1, 2 * D_FF), lambda i, be, nbu: (be[i], 0, 0)),
                  pl.BlockSpec((1, D_FF, D_MODEL), lambda i, be, nbu: (be[i], 0, 0)),
                  pl.BlockSpec((1, 1, D_MODEL), lambda i, be, nbu: (be[i], 0, 0))],
        out_specs=pl.BlockSpec((MOE_BLOCK, D_MODEL), blk),
    )
    return pl.pallas_call(
        _moe_ffn_kernel,
        grid_spec=grid_spec,
        out_shape=jax.ShapeDtypeStruct((n_rows, D_MODEL), F32),
        compiler_params=_cparams(("arbitrary",)),
        name="moe_ffn",
    )(block_e, nb_used, xs, w1, b1, w2, b2)


def _moe_combine_kernel(off8_ref, len8_ref, dst8_ref, tot8_ref, x_ref, dest_ref, gate_ref,
                        lg_ref, lb_ref, out_hbm, o_ref, buf, sem):
    s = pl.program_id(0)
    tm = x_ref.shape[0]

    def per_expert(e, total):
        n = len8_ref[s * N_EXPERTS + e]
        dst0 = off8_ref[s * N_EXPERTS + e]
        src0 = dst8_ref[s * N_EXPERTS + e]

        def issue(i, _):
            _group_copy(out_hbm, buf, sem, src0 + i, dst0 + i).start()
            return 0

        lax.fori_loop(0, n, issue, 0)
        return total + n

    total = lax.fori_loop(0, N_EXPERTS, per_expert, 0)

    def clear(g, _):
        buf[pl.ds(pl.multiple_of(g * MOE_GROUP, MOE_GROUP), MOE_GROUP), :] = jnp.zeros(
            (MOE_GROUP, D_MODEL), F32)
        return 0

    lax.fori_loop(tot8_ref[s], MOE_TILE_ROWS // MOE_GROUP, clear, 0)

    dest = dest_ref[...]
    gate = gate_ref[...]
    r_iota = lax.broadcasted_iota(I32, (tm, MOE_TILE_ROWS), 1)
    weights = jnp.zeros((tm, MOE_TILE_ROWS), F32)
    for k in range(TOP_K):
        weights = weights + jnp.where(r_iota == dest[:, k:k + 1], gate[:, k:k + 1], 0.0)

    def drain(i, _):
        _group_copy(out_hbm, buf, sem, 0, 0).wait()
        return 0

    lax.fori_loop(0, total, drain, 0)
    ff = _dot(weights.astype(BF16), buf[...].astype(BF16))
    o_ref[...] = _layer_norm(DEEPNORM_ALPHA * x_ref[...] + ff, lg_ref[...], lb_ref[...])


def _moe_combine_call(x2, dest, gate, out_rows, lg, lb, off8, len8, dst8, tot8):
    t = x2.shape[0]
    tm = TOK_TILE
    grid_spec = pltpu.PrefetchScalarGridSpec(
        num_scalar_prefetch=4,
        grid=(t // tm,),
        in_specs=[pl.BlockSpec((tm, D_MODEL), lambda i, *_: (i, 0)),
                  pl.BlockSpec((tm, TOP_K), lambda i, *_: (i, 0)),
                  pl.BlockSpec((tm, TOP_K), lambda i, *_: (i, 0)),
                  pl.BlockSpec((1, D_MODEL), lambda i, *_: (0, 0)),
                  pl.BlockSpec((1, D_MODEL), lambda i, *_: (0, 0)),
                  pl.BlockSpec(memory_space=pl.ANY)],
        out_specs=pl.BlockSpec((tm, D_MODEL), lambda i, *_: (i, 0)),
        scratch_shapes=[pltpu.VMEM((MOE_TILE_ROWS, D_MODEL), F32),
                        pltpu.SemaphoreType.DMA(())],
    )
    return pl.pallas_call(
        _moe_combine_kernel,
        grid_spec=grid_spec,
        out_shape=jax.ShapeDtypeStruct((t, D_MODEL), F32),
        compiler_params=_cparams(("arbitrary",)),
        name="moe_combine_ln3",
    )(off8, len8, dst8, tot8, x2, dest, gate, lg.reshape(1, -1), lb.reshape(1, -1), out_rows)


def _moe_tables(cnt, n_blocks):
    len8 = (cnt + MOE_GROUP - 1) // MOE_GROUP
    off8 = jnp.cumsum(len8, axis=1) - len8
    tot8 = jnp.sum(len8, axis=1)
    per_block = MOE_BLOCK // MOE_GROUP
    blocks_e = (jnp.sum(len8, axis=0) + per_block - 1) // per_block
    ends_e = jnp.cumsum(blocks_e)
    start8_e = (ends_e - blocks_e) * per_block
    dst8 = start8_e[None, :] + jnp.cumsum(len8, axis=0) - len8
    nb_used = ends_e[-1:]
    block_e = jnp.minimum(jnp.searchsorted(ends_e, jnp.arange(n_blocks, dtype=I32), side="right"),
                          N_EXPERTS - 1).astype(I32)
    return (off8.reshape(-1).astype(I32), len8.reshape(-1).astype(I32),
            dst8.reshape(-1).astype(I32), tot8.astype(I32), block_e, nb_used.astype(I32))


def _moe(x2, idx, gate, cnt, w1, b1, w2, b2, lg, lb):
    t = x2.shape[0]
    tiles = t // TOK_TILE
    max_rows = t * TOP_K + tiles * N_EXPERTS * (MOE_GROUP - 1)
    n_blocks = -(-max_rows // MOE_BLOCK) + N_EXPERTS
    off8, len8, dst8, tot8, block_e, nb_used = _moe_tables(cnt, n_blocks)
    offv = jnp.broadcast_to((off8.reshape(tiles, N_EXPERTS, 1) * MOE_GROUP).astype(F32),
                            (tiles, N_EXPERTS, LANES))
    dest_t, xs = _moe_sort_call(x2, idx.T, offv, off8, len8, dst8, n_blocks * MOE_BLOCK)
    out_rows = _moe_ffn_call(xs, block_e, nb_used, w1, b1, w2, b2)
    return _moe_combine_call(x2, dest_t.T, gate, out_rows, lg, lb, off8, len8, dst8, tot8)


def _rope_tables(seq):
    rows = seq // GRID_W
    row = jnp.repeat(jnp.arange(rows, dtype=F32), GRID_W)
    col = jnp.tile(jnp.arange(GRID_W, dtype=F32), rows)
    inv_freq = ROPE_THETA ** (-jnp.arange(ROPE_AXIS_HALF, dtype=F32) / ROPE_AXIS_HALF)
    ar = row[:, None] * inv_freq
    ac = col[:, None] * inv_freq
    cos_t = jnp.concatenate([jnp.cos(ar), jnp.cos(ar), jnp.cos(ac), jnp.cos(ac)], axis=1)
    sin_t = jnp.concatenate([-jnp.sin(ar), jnp.sin(ar), -jnp.sin(ac), jnp.sin(ac)], axis=1)
    return cos_t, sin_t


def _encode(x, mem, p):
    b, seq, _ = x.shape
    t = b * seq
    cos_t, sin_t = _rope_tables(seq)
    x = _ln_call(x.reshape(t, D_MODEL), p["ln_in_g"], p["ln_in_b"])
    c0 = D_ATTN_Q + 2 * D_ATTN_KV
    c1 = c0 + 3 * D_MODEL
    c2 = c1 + 2 * D_MODEL
    for l in range(DEPTH):
        w_in = p["w_in"][l].astype(BF16)
        q, k, v = _qkv_call(x, w_in[:, :c0], cos_t, sin_t, p["q_norm_g"][l], p["k_norm_g"][l], seq)
        cb, pc = _convproj_call(x, w_in[:, c0:c1])
        gy, rx = _rnnproj_call(x, w_in[:, c1:c2])
        g = _gateproj_call(x, w_in[:, c2:], p["b_gate"][l].reshape(-1))
        ya = _attn_call(q.reshape(b, seq, -1), k.reshape(b, seq, -1), v.reshape(b, seq, -1))
        h = _lru_call(rx.reshape(b, seq, -1), p["lru_conv_w"][l], p["lru_conv_b"][l],
                      p["lru_wa"][l].astype(BF16), p["lru_wx"][l].astype(BF16),
                      p["lru_ba"][l], p["lru_bx"][l], p["lru_lam"][l])
        x = _merge_call(x, cb, pc, ya.reshape(t, -1), h.reshape(t, -1), gy, g, p["conv_w"][l],
                        p["w_mix_out"][l].astype(BF16), p["ln1_g"][l], p["ln1_b"][l], seq)
        mk, mv = _kv_call(mem, p["xkv_w"][l].astype(BF16))
        rw = jnp.pad(p["router_w"][l], ((0, 0), (0, LANES - N_EXPERTS))).astype(BF16)
        rb = jnp.pad(p["router_b"][l], (0, LANES - N_EXPERTS), constant_values=NEG_BIG)
        x2, idx, gate, cnt = _cross_call(x, mk, mv, p["xq_w"][l].astype(BF16),
                                         p["xo_w"][l].astype(BF16), p["ln2_g"][l], p["ln2_b"][l],
                                         rw, rb.reshape(1, -1), seq)
        x = _moe(x2, idx[:, :TOP_K], gate[:, :TOP_K], cnt[:, 0, :N_EXPERTS],
                 p["w1"][l].astype(BF16), p["b1"][l][:, None, :], p["w2"][l].astype(BF16),
                 p["b2"][l][:, None, :], p["ln3_g"][l], p["ln3_b"][l])
    return x.reshape(b, seq, D_MODEL)


def kernel(x_prompt, x_sample, mem_prompt, mem_sample, ln_in_g, ln_in_b, w_in, b_gate, q_norm_g, k_norm_g, conv_w, lru_conv_w, lru_conv_b, lru_wa, lru_ba, lru_wx, lru_bx, lru_lam, w_mix_out, ln1_g, ln1_b, xq_w, xkv_w, xo_w, ln2_g, ln2_b, router_w, router_b, w1, b1, w2, b2, ln3_g, ln3_b):
    params = dict(ln_in_g=ln_in_g, ln_in_b=ln_in_b, w_in=w_in, b_gate=b_gate, q_norm_g=q_norm_g,
                  k_norm_g=k_norm_g, conv_w=conv_w, lru_conv_w=lru_conv_w, lru_conv_b=lru_conv_b,
                  lru_wa=lru_wa, lru_ba=lru_ba, lru_wx=lru_wx, lru_bx=lru_bx, lru_lam=lru_lam,
                  w_mix_out=w_mix_out, ln1_g=ln1_g, ln1_b=ln1_b, xq_w=xq_w, xkv_w=xkv_w, xo_w=xo_w,
                  ln2_g=ln2_g, ln2_b=ln2_b, router_w=router_w, router_b=router_b, w1=w1, b1=b1,
                  w2=w2, b2=b2, ln3_g=ln3_g, ln3_b=ln3_b)
    if x_prompt.shape[1:] == x_sample.shape[1:]:
        nb = x_prompt.shape[0]
        y = _encode(jnp.concatenate([x_prompt, x_sample], axis=0),
                    jnp.concatenate([mem_prompt, mem_sample], axis=0), params)
        return (y[:nb], y[nb:])
    return (_encode(x_prompt, mem_prompt, params), _encode(x_sample, mem_sample, params))
```

```python
import functools

import jax
import jax.numpy as jnp
from jax import lax
from jax.experimental import pallas as pl
from jax.experimental.pallas import tpu as pltpu

F32 = jnp.float32
BF16 = jnp.bfloat16
I32 = jnp.int32

D_MODEL = 1024
DEPTH = 2
N_HEADS = 8
N_KV_HEADS = 2
HEAD_DIM = 128
Q_GROUPS = N_HEADS // N_KV_HEADS
D_ATTN_Q = N_HEADS * HEAD_DIM
D_ATTN_KV = N_KV_HEADS * HEAD_DIM
GRID_W = 64
ROPE_AXIS_HALF = HEAD_DIM // 4
ROPE_THETA = 10000.0
QK_EPS = 1e-6
LRU_BLOCKS = 8
LRU_BLOCK_W = D_MODEL // LRU_BLOCKS
LRU_C = 8.0
MEM_HEADS = 4
MEM_HEAD_DIM = D_MODEL // MEM_HEADS
N_EXPERTS = 32
TOP_K = 4
D_FF = D_MODEL // 2
SWIGLU_LIMIT = 7.0
SWIGLU_ALPHA = 1.702
LN_EPS = 1e-5
DEEPNORM_ALPHA = (2 * DEPTH) ** 0.25

LANES = 128
SUBLANES = 8
BF16_SUBLANES = 16
VMEM_LIMIT = 56 * 1024 * 1024

PROJ_TM = 512
ATTN_TQ = 256
ATTN_TK = 512
TOK_TILE = 256
LRU_CHUNK = 512
LRU_SUBSEQ = 8
LRU_PAD = 8
MOE_GROUP = SUBLANES
MOE_BLOCK = 256
MOE_TILE_ROWS = TOK_TILE * TOP_K + N_EXPERTS * MOE_GROUP
NEG_BIG = -1e30
LOG2E = 1.4426950408889634


def _cparams(sem, vmem=VMEM_LIMIT):
    return pltpu.CompilerParams(dimension_semantics=sem, vmem_limit_bytes=vmem)


def _layer_norm(x, g, b):
    mu = jnp.mean(x, axis=-1, keepdims=True)
    xc = x - mu
    var = jnp.mean(xc * xc, axis=-1, keepdims=True)
    return xc * lax.rsqrt(var + LN_EPS) * g + b


def _sigmoid(x):
    return 1.0 / (1.0 + jnp.exp(-x))


def _dot(a, b):
    return jnp.dot(a, b, preferred_element_type=F32)


def _dot_nt(a, b):
    return lax.dot_general(a, b, (((1,), (1,)), ((), ())), preferred_element_type=F32)


def _ln_kernel(x_ref, g_ref, b_ref, o_ref):
    o_ref[...] = _layer_norm(x_ref[...], g_ref[...], b_ref[...])


def _ln_call(x, g, b):
    t = x.shape[0]
    tm = PROJ_TM
    return pl.pallas_call(
        _ln_kernel,
        grid=(t // tm,),
        in_specs=[pl.BlockSpec((tm, D_MODEL), lambda i: (i, 0)),
                  pl.BlockSpec((1, D_MODEL), lambda i: (0, 0)),
                  pl.BlockSpec((1, D_MODEL), lambda i: (0, 0))],
        out_specs=pl.BlockSpec((tm, D_MODEL), lambda i: (i, 0)),
        out_shape=jax.ShapeDtypeStruct((t, D_MODEL), F32),
        compiler_params=_cparams(("parallel",)),
        name="ln_in",
    )(x, g.reshape(1, -1), b.reshape(1, -1))


def _qkv_kernel(x_ref, w_ref, cos_ref, sin_ref, qg_ref, kg_ref, q_ref, k_ref, vt_ref):
    x = x_ref[...].astype(BF16)
    z = _dot(x, w_ref[...])
    c = cos_ref[...]
    s = sin_ref[...]
    lane = lax.broadcasted_iota(I32, c.shape, 1)
    first_half = (lane % (2 * ROPE_AXIS_HALF)) < ROPE_AXIS_HALF

    def prep(zh, g, scale):
        ms = jnp.mean(zh * zh, axis=-1, keepdims=True)
        y = zh * lax.rsqrt(ms + QK_EPS) * g
        partner = jnp.where(first_half,
                            pltpu.roll(y, HEAD_DIM - ROPE_AXIS_HALF, 1),
                            pltpu.roll(y, ROPE_AXIS_HALF, 1))
        return (y * c + partner * s) * scale

    qg = qg_ref[...]
    kg = kg_ref[...]
    for h in range(N_HEADS):
        sl = slice(h * HEAD_DIM, (h + 1) * HEAD_DIM)
        q_ref[:, sl] = prep(z[:, sl], qg, LOG2E * HEAD_DIM ** -0.5).astype(q_ref.dtype)
    for h in range(N_KV_HEADS):
        sl = slice(h * HEAD_DIM, (h + 1) * HEAD_DIM)
        zsl = slice(D_ATTN_Q + h * HEAD_DIM, D_ATTN_Q + (h + 1) * HEAD_DIM)
        k_ref[:, sl] = prep(z[:, zsl], kg, 1.0).astype(k_ref.dtype)
    vt_ref[0] = z[:, D_ATTN_Q + D_ATTN_KV:].T.astype(vt_ref.dtype)


def _qkv_call(x, w, cos_t, sin_t, qg, kg, seq):
    t = x.shape[0]
    tm = PROJ_TM
    nseq = seq // tm
    n = w.shape[1]
    return pl.pallas_call(
        _qkv_kernel,
        grid=(t // tm,),
        in_specs=[pl.BlockSpec((tm, D_MODEL), lambda i: (i, 0)),
                  pl.BlockSpec((D_MODEL, n), lambda i: (0, 0)),
                  pl.BlockSpec((tm, HEAD_DIM), lambda i: (i % nseq, 0)),
                  pl.BlockSpec((tm, HEAD_DIM), lambda i: (i % nseq, 0)),
                  pl.BlockSpec((1, HEAD_DIM), lambda i: (0, 0)),
                  pl.BlockSpec((1, HEAD_DIM), lambda i: (0, 0))],
        out_specs=[pl.BlockSpec((tm, D_ATTN_Q), lambda i: (i, 0)),
                   pl.BlockSpec((tm, D_ATTN_KV), lambda i: (i, 0)),
                   pl.BlockSpec((1, D_ATTN_KV, tm), lambda i: (i // nseq, 0, i % nseq))],
        out_shape=[jax.ShapeDtypeStruct((t, D_ATTN_Q), BF16),
                   jax.ShapeDtypeStruct((t, D_ATTN_KV), BF16),
                   jax.ShapeDtypeStruct((t // seq, D_ATTN_KV, seq), BF16)],
        compiler_params=_cparams(("parallel",)),
        name="qkv_proj",
    )(x, w, cos_t, sin_t, qg.reshape(1, -1), kg.reshape(1, -1))


def _convproj_kernel(x_ref, w_ref, cb_ref, p_ref):
    x = x_ref[...].astype(BF16)
    z = _dot(x, w_ref[...])
    cb_ref[...] = z[:, :D_MODEL].astype(cb_ref.dtype)
    p_ref[...] = (z[:, D_MODEL:2 * D_MODEL] * z[:, 2 * D_MODEL:]).astype(p_ref.dtype)


def _convproj_call(x, w):
    t = x.shape[0]
    tm = PROJ_TM
    return pl.pallas_call(
        _convproj_kernel,
        grid=(t // tm,),
        in_specs=[pl.BlockSpec((tm, D_MODEL), lambda i: (i, 0)),
                  pl.BlockSpec((D_MODEL, 3 * D_MODEL), lambda i: (0, 0))],
        out_specs=[pl.BlockSpec((tm, D_MODEL), lambda i: (i, 0)),
                   pl.BlockSpec((tm, D_MODEL), lambda i: (i, 0))],
        out_shape=[jax.ShapeDtypeStruct((t, D_MODEL), BF16),
                   jax.ShapeDtypeStruct((t, D_MODEL), BF16)],
        compiler_params=_cparams(("parallel",)),
        name="conv_proj",
    )(x, w)


def _rnnproj_kernel(x_ref, w_ref, gy_ref, rx_ref):
    x = x_ref[...].astype(BF16)
    z = _dot(x, w_ref[...])
    gy_ref[...] = jax.nn.gelu(z[:, :D_MODEL]).astype(gy_ref.dtype)
    rx_ref[...] = z[:, D_MODEL:]


def _rnnproj_call(x, w):
    t = x.shape[0]
    tm = PROJ_TM
    return pl.pallas_call(
        _rnnproj_kernel,
        grid=(t // tm,),
        in_specs=[pl.BlockSpec((tm, D_MODEL), lambda i: (i, 0)),
                  pl.BlockSpec((D_MODEL, 2 * D_MODEL), lambda i: (0, 0))],
        out_specs=[pl.BlockSpec((tm, D_MODEL), lambda i: (i, 0)),
                   pl.BlockSpec((tm, D_MODEL), lambda i: (i, 0))],
        out_shape=[jax.ShapeDtypeStruct((t, D_MODEL), BF16),
                   jax.ShapeDtypeStruct((t, D_MODEL), F32)],
        compiler_params=_cparams(("parallel",)),
        name="rnn_proj",
    )(x, w)


def _gateproj_kernel(x_ref, w_ref, b_ref, g_ref):
    x = x_ref[...].astype(BF16)
    z = _dot(x, w_ref[...]) + b_ref[...]
    g_ref[...] = _sigmoid(z).astype(g_ref.dtype)


def _gateproj_call(x, w, b):
    t = x.shape[0]
    tm = PROJ_TM
    return pl.pallas_call(
        _gateproj_kernel,
        grid=(t // tm,),
        in_specs=[pl.BlockSpec((tm, D_MODEL), lambda i: (i, 0)),
                  pl.BlockSpec((D_MODEL, 3 * D_MODEL), lambda i: (0, 0)),
                  pl.BlockSpec((1, 3 * D_MODEL), lambda i: (0, 0))],
        out_specs=pl.BlockSpec((tm, 3 * D_MODEL), lambda i: (i, 0)),
        out_shape=jax.ShapeDtypeStruct((t, 3 * D_MODEL), BF16),
        compiler_params=_cparams(("parallel",)),
        name="gate_proj",
    )(x, w, b.reshape(1, -1))


def _attn_kernel(q_ref, k_ref, vt_ref, o_ref, acc_s, s0_s, s1_s, p0_s, p1_s, *, tk):
    tq = q_ref.shape[1]
    n = Q_GROUPS * tq
    nk = k_ref.shape[1] // tk
    qs = jnp.concatenate([q_ref[0, :, g * HEAD_DIM:(g + 1) * HEAD_DIM] for g in range(Q_GROUPS)],
                         axis=0)
    acc_s[...] = jnp.zeros(acc_s.shape, F32)
    s_bufs = (s0_s, s1_s)
    p_bufs = (p0_s, p1_s)

    def scores(j, slot):
        start = pl.multiple_of(j * tk, tk)
        s_bufs[slot][...] = _dot_nt(k_ref[0, pl.ds(start, tk), :], qs)

    def softmax(slot, m, l):
        st = s_bufs[slot][...]
        m_new = jnp.maximum(m, jnp.max(st, axis=0, keepdims=True))
        alpha = jnp.exp2(m - m_new)
        p = jnp.exp2(st - m_new)
        p_bufs[slot][...] = p.astype(BF16)
        return m_new, alpha * l + jnp.sum(p, axis=0, keepdims=True), alpha

    def accumulate(j, slot, alpha):
        start = pl.multiple_of(j * tk, tk)
        acc_s[...] = alpha * acc_s[...] + _dot(vt_ref[0, :, pl.ds(start, tk)], p_bufs[slot][...])

    def stage(j, m, l, alphas, first, last):
        new_alphas = []
        for slot in range(2):
            if not first:
                accumulate(j - 2 + slot, slot, alphas[slot])
            m, l, a = softmax(slot, m, l)
            new_alphas.append(a)
            if not last:
                scores(j + 2 + slot, slot)
        return m, l, tuple(new_alphas)

    m = jnp.full((1, n), NEG_BIG, F32)
    l = jnp.zeros((1, n), F32)
    scores(0, 0)
    scores(1, 1)
    if nk == 2:
        m, l, alphas = stage(0, m, l, None, True, True)
    else:
        m, l, alphas = stage(0, m, l, None, True, False)
        m, l, alphas = lax.fori_loop(
            1, nk // 2 - 1, lambda i, c: stage(2 * i, c[0], c[1], c[2], False, False), (m, l, alphas))
        m, l, alphas = stage(nk - 2, m, l, alphas, False, True)
    accumulate(nk - 2, 0, alphas[0])
    accumulate(nk - 1, 1, alphas[1])
    ot = acc_s[...] / l
    for g in range(Q_GROUPS):
        o_ref[0, :, g * HEAD_DIM:(g + 1) * HEAD_DIM] = ot[:, g * tq:(g + 1) * tq].T.astype(o_ref.dtype)


def _attn_call(q, k, vt):
    b, s, _ = q.shape
    tq = min(ATTN_TQ, s)
    tk = min(ATTN_TK, s // 2)
    assert (s // tk) % 2 == 0
    gw = Q_GROUPS * HEAD_DIM
    n = Q_GROUPS * tq
    return pl.pallas_call(
        functools.partial(_attn_kernel, tk=tk),
        grid=(b, N_KV_HEADS, s // tq),
        in_specs=[pl.BlockSpec((1, tq, gw), lambda bi, h, i: (bi, i, h)),
                  pl.BlockSpec((1, s, HEAD_DIM), lambda bi, h, i: (bi, 0, h)),
                  pl.BlockSpec((1, HEAD_DIM, s), lambda bi, h, i: (bi, h, 0))],
        out_specs=pl.BlockSpec((1, tq, gw), lambda bi, h, i: (bi, i, h)),
        out_shape=jax.ShapeDtypeStruct((b, s, D_ATTN_Q), BF16),
        scratch_shapes=[pltpu.VMEM((HEAD_DIM, n), F32),
                        pltpu.VMEM((tk, n), F32), pltpu.VMEM((tk, n), F32),
                        pltpu.VMEM((tk, n), BF16), pltpu.VMEM((tk, n), BF16)],
        compiler_params=_cparams(("parallel", "parallel", "parallel")),
        name="flash_attn",
    )(q, k, vt)


def _lru_kernel(x_ref, cw_ref, cb_ref, wa_ref, wx_ref, ba_ref, bx_ref, lam_ref, o_ref,
                xp_s, a_s, u_s, *, seq, chunk):
    sub = seq // LRU_SUBSEQ
    stride = sub + LRU_PAD
    cps = sub // chunk
    nchunks = seq // chunk

    zeros8 = jnp.zeros((SUBLANES, LANES), F32)
    xp_s[0:SUBLANES, :] = zeros8
    xp_s[seq + SUBLANES:seq + 2 * SUBLANES, :] = zeros8
    xp_s[SUBLANES:seq + SUBLANES, :] = x_ref[0]
    cw = cw_ref[...]
    cb = cb_ref[...]

    for d in range(2):
        lam = lam_ref[d:d + 1, :]
        neg = -lam
        softplus = jnp.maximum(neg, 0.0) + jnp.log1p(jnp.exp(-jnp.abs(neg)))
        decay = -LRU_C * softplus
        wa = wa_ref[d, 0]
        wx = wx_ref[d, 0]
        ba = ba_ref[d:d + 1, :]
        bx = bx_ref[d:d + 1, :]

        def gate_chunk(c, _, decay=decay, wa=wa, wx=wx, ba=ba, bx=bx):
            base = pl.multiple_of(c * chunk, chunk)
            xc = cb
            for k in range(4):
                xc = xc + cw[k:k + 1, :] * xp_s[pl.ds(base + SUBLANES - 1 + k, chunk), :]
            xb = xc.astype(BF16)
            r = _sigmoid(_dot(xb, wa) + ba)
            i = _sigmoid(_dot(xb, wx) + bx)
            a = jnp.exp(decay * r)
            u = jnp.sqrt(1.0 - a * a) * (i * xc)
            dst = pl.multiple_of((c // cps) * stride + (c % cps) * chunk, SUBLANES)
            a_s[pl.ds(dst, chunk), :] = a
            u_s[pl.ds(dst, chunk), :] = u
            return 0

        lax.fori_loop(0, nchunks, gate_chunk, 0)

        def step(tt, carry, d=d):
            h, p = carry
            t = tt if d == 0 else sub - 1 - tt
            av = a_s[pl.ds(t, LRU_SUBSEQ, stride=stride), :]
            uv = u_s[pl.ds(t, LRU_SUBSEQ, stride=stride), :]
            h = av * h + uv
            p = av * p
            u_s[pl.ds(t, LRU_SUBSEQ, stride=stride), :] = h
            a_s[pl.ds(t, LRU_SUBSEQ, stride=stride), :] = p
            return h, p

        h_end, p_end = lax.fori_loop(0, sub, step, (zeros8, jnp.ones((SUBLANES, LANES), F32)))

        order = list(range(LRU_SUBSEQ)) if d == 0 else list(range(LRU_SUBSEQ - 1, -1, -1))
        carry_in = {}
        state = jnp.zeros((1, LANES), F32)
        for j in order:
            carry_in[j] = state
            state = p_end[j:j + 1, :] * state + h_end[j:j + 1, :]

        rows = min(sub, chunk)
        for j in range(LRU_SUBSEQ):
            for c in range(sub // rows):
                src = j * stride + c * rows
                dst = j * sub + c * rows
                val = u_s[src:src + rows, :] + a_s[src:src + rows, :] * carry_in[j]
                if d == 0:
                    o_ref[0, dst:dst + rows, :] = val
                else:
                    o_ref[0, dst:dst + rows, :] = o_ref[0, dst:dst + rows, :] + val


def _lru_call(rx, cw, cb, wa, wx, ba, bx, lam):
    b, s, _ = rx.shape
    chunk = min(LRU_CHUNK, s // LRU_SUBSEQ)
    scr_rows = LRU_SUBSEQ * (s // LRU_SUBSEQ + LRU_PAD)
    w = LRU_BLOCK_W
    return pl.pallas_call(
        functools.partial(_lru_kernel, seq=s, chunk=chunk),
        grid=(b, LRU_BLOCKS),
        in_specs=[pl.BlockSpec((1, s, w), lambda bi, c: (bi, 0, c)),
                  pl.BlockSpec((4, w), lambda bi, c: (0, c)),
                  pl.BlockSpec((1, w), lambda bi, c: (0, c)),
                  pl.BlockSpec((2, 1, w, w), lambda bi, c: (0, c, 0, 0)),
                  pl.BlockSpec((2, 1, w, w), lambda bi, c: (0, c, 0, 0)),
                  pl.BlockSpec((2, w), lambda bi, c: (0, c)),
                  pl.BlockSpec((2, w), lambda bi, c: (0, c)),
                  pl.BlockSpec((2, w), lambda bi, c: (0, c))],
        out_specs=pl.BlockSpec((1, s, w), lambda bi, c: (bi, 0, c)),
        out_shape=jax.ShapeDtypeStruct((b, s, D_MODEL), F32),
        scratch_shapes=[pltpu.VMEM((s + 2 * SUBLANES, w), F32),
                        pltpu.VMEM((scr_rows, w), F32),
                        pltpu.VMEM((scr_rows, w), F32)],
        compiler_params=_cparams(("parallel", "parallel")),
        name="rg_lru",
    )(rx, cw, cb.reshape(1, -1), wa, wx, ba, bx, lam)


def _merge_kernel(x_ref, cb_ref, p_ref, pprev_ref, pnext_ref, ya_ref, h_ref, gy_ref, g_ref,
                  cw_ref, w_ref, lg_ref, lb_ref, o_ref, *, seq):
    tm = x_ref.shape[0]
    t0 = pl.program_id(0) * tm
    p = p_ref[...].astype(F32)
    row = lax.broadcasted_iota(I32, p.shape, 0)
    has_prev = ((t0 % seq) != 0).astype(F32)
    has_next = (((t0 + tm) % seq) != 0).astype(F32)
    prev_row = pprev_ref[BF16_SUBLANES - 1:BF16_SUBLANES, :].astype(F32) * has_prev
    next_row = pnext_ref[0:1, :].astype(F32) * has_next
    p_dn = jnp.where(row == 0, prev_row, pltpu.roll(p, 1, 0))
    p_up = jnp.where(row == tm - 1, next_row, pltpu.roll(p, tm - 1, 0))
    cw = cw_ref[...]
    y_conv = cb_ref[...].astype(F32) * (cw[0:1, :] * p_dn + cw[1:2, :] * p + cw[2:3, :] * p_up)
    y_rnn = h_ref[...] * gy_ref[...].astype(F32)
    merged = (g_ref[:, :D_MODEL].astype(F32) * y_conv
              + g_ref[:, D_MODEL:2 * D_MODEL].astype(F32) * ya_ref[...].astype(F32)
              + g_ref[:, 2 * D_MODEL:].astype(F32) * y_rnn)
    mix = _dot(merged.astype(BF16), w_ref[...])
    o_ref[...] = _layer_norm(DEEPNORM_ALPHA * x_ref[...] + mix, lg_ref[...], lb_ref[...])


def _merge_call(x, cb, p, ya, h, gy, g, cw, w, lg, lb, seq):
    t = x.shape[0]
    tm = TOK_TILE
    hb = BF16_SUBLANES
    per = tm // hb
    last = t // hb - 1
    row_spec = pl.BlockSpec((tm, D_MODEL), lambda i: (i, 0))
    vec_spec = pl.BlockSpec((1, D_MODEL), lambda i: (0, 0))
    return pl.pallas_call(
        functools.partial(_merge_kernel, seq=seq),
        grid=(t // tm,),
        in_specs=[row_spec, row_spec, row_spec,
                  pl.BlockSpec((hb, D_MODEL), lambda i: (jnp.maximum(i * per - 1, 0), 0)),
                  pl.BlockSpec((hb, D_MODEL), lambda i: (jnp.minimum((i + 1) * per, last), 0)),
                  row_spec, row_spec, row_spec,
                  pl.BlockSpec((tm, 3 * D_MODEL), lambda i: (i, 0)),
                  pl.BlockSpec((3, D_MODEL), lambda i: (0, 0)),
                  pl.BlockSpec((D_MODEL, D_MODEL), lambda i: (0, 0)),
                  vec_spec, vec_spec],
        out_specs=row_spec,
        out_shape=jax.ShapeDtypeStruct((t, D_MODEL), F32),
        compiler_params=_cparams(("parallel",)),
        name="merge_outproj_ln1",
    )(x, cb, p, p, p, ya, h, gy, g, cw, w, lg.reshape(1, -1), lb.reshape(1, -1))


def _kv_kernel(m_ref, w_ref, k_ref, v_ref):
    z = _dot(m_ref[0].astype(BF16), w_ref[...])
    k_ref[0] = z[:, :D_MODEL].astype(k_ref.dtype)
    v_ref[0] = z[:, D_MODEL:].astype(v_ref.dtype)


def _kv_call(mem, w):
    b, m, _ = mem.shape
    return pl.pallas_call(
        _kv_kernel,
        grid=(b,),
        in_specs=[pl.BlockSpec((1, m, D_MODEL), lambda i: (i, 0, 0)),
                  pl.BlockSpec((D_MODEL, 2 * D_MODEL), lambda i: (0, 0))],
        out_specs=[pl.BlockSpec((1, m, D_MODEL), lambda i: (i, 0, 0)),
                   pl.BlockSpec((1, m, D_MODEL), lambda i: (i, 0, 0))],
        out_shape=[jax.ShapeDtypeStruct((b, m, D_MODEL), BF16),
                   jax.ShapeDtypeStruct((b, m, D_MODEL), BF16)],
        compiler_params=_cparams(("parallel",)),
        name="mem_kv_proj",
    )(mem, w)


def _cross_kernel(x_ref, k_ref, v_ref, wq_ref, wo_ref, lg_ref, lb_ref, rw_ref, rb_ref,
                  x2_ref, idx_ref, gate_ref, cnt_ref):
    x = x_ref[...]
    q = _dot(x.astype(BF16), wq_ref[...]) * (MEM_HEAD_DIM ** -0.5)
    outs = []
    for h in range(MEM_HEADS):
        sl = slice(h * MEM_HEAD_DIM, (h + 1) * MEM_HEAD_DIM)
        s = _dot_nt(q[:, sl].astype(BF16), k_ref[0, :, sl])
        m = jnp.max(s, axis=-1, keepdims=True)
        p = jnp.exp(s - m)
        p = p / jnp.sum(p, axis=-1, keepdims=True)
        outs.append(_dot(p.astype(BF16), v_ref[0, :, sl]))
    o = jnp.concatenate(outs, axis=1)
    cross = _dot(o.astype(BF16), wo_ref[...])
    x2 = _layer_norm(DEEPNORM_ALPHA * x + cross, lg_ref[...], lb_ref[...])
    x2_ref[...] = x2

    logits = _dot(x2.astype(BF16), rw_ref[...]) + rb_ref[...]
    lane = lax.broadcasted_iota(I32, logits.shape, 1)
    idx_acc = jnp.zeros(logits.shape, I32)
    val_acc = jnp.zeros(logits.shape, F32)
    member = jnp.zeros(logits.shape, F32)
    top0 = None
    for k in range(TOP_K):
        m = jnp.max(logits, axis=-1, keepdims=True)
        pick = jnp.min(jnp.where(logits == m, lane, LANES), axis=-1, keepdims=True)
        hit = lane == pick
        if k == 0:
            top0 = m
        idx_acc = jnp.where(lane == k, pick, idx_acc)
        val_acc = jnp.where(lane == k, jnp.exp(m - top0), val_acc)
        member = member + hit.astype(F32)
        logits = jnp.where(hit, NEG_BIG, logits)
    idx_ref[...] = idx_acc
    gate_ref[...] = val_acc / jnp.sum(val_acc, axis=-1, keepdims=True)
    cnt_ref[0] = jnp.sum(member, axis=0, keepdims=True).astype(I32)


def _cross_call(x, k, v, wq, wo, lg, lb, rw, rb, seq):
    t = x.shape[0]
    tm = TOK_TILE
    per_seq = seq // tm
    m = k.shape[1]
    row_spec = pl.BlockSpec((tm, D_MODEL), lambda i: (i, 0))
    vec_spec = pl.BlockSpec((1, D_MODEL), lambda i: (0, 0))
    lane_spec = pl.BlockSpec((tm, LANES), lambda i: (i, 0))
    return pl.pallas_call(
        _cross_kernel,
        grid=(t // tm,),
        in_specs=[row_spec,
                  pl.BlockSpec((1, m, D_MODEL), lambda i: (i // per_seq, 0, 0)),
                  pl.BlockSpec((1, m, D_MODEL), lambda i: (i // per_seq, 0, 0)),
                  pl.BlockSpec((D_MODEL, D_MODEL), lambda i: (0, 0)),
                  pl.BlockSpec((D_MODEL, D_MODEL), lambda i: (0, 0)),
                  vec_spec, vec_spec,
                  pl.BlockSpec((D_MODEL, LANES), lambda i: (0, 0)),
                  pl.BlockSpec((1, LANES), lambda i: (0, 0))],
        out_specs=[row_spec, lane_spec, lane_spec,
                   pl.BlockSpec((1, 1, LANES), lambda i: (i, 0, 0))],
        out_shape=[jax.ShapeDtypeStruct((t, D_MODEL), F32),
                   jax.ShapeDtypeStruct((t, LANES), I32),
                   jax.ShapeDtypeStruct((t, LANES), F32),
                   jax.ShapeDtypeStruct((t // tm, 1, LANES), I32)],
        compiler_params=_cparams(("parallel",)),
        name="cross_attn_ln2_router",
    )(x, k, v, wq, wo, lg.reshape(1, -1), lb.reshape(1, -1), rw, rb)


def _group_copy(src, dst, sem, s_grp, d_grp):
    return pltpu.make_async_copy(
        src.at[pl.ds(pl.multiple_of(s_grp * MOE_GROUP, MOE_GROUP), MOE_GROUP), :],
        dst.at[pl.ds(pl.multiple_of(d_grp * MOE_GROUP, MOE_GROUP), MOE_GROUP), :], sem)


def _moe_sort_kernel(off8_ref, len8_ref, dst8_ref, x_ref, idx_ref, offv_ref, dest_ref, xs_hbm,
                     xs_s, sem):
    s = pl.program_id(0)
    tm = x_ref.shape[0]
    idx = idx_ref[...]
    e_iota = lax.broadcasted_iota(I32, (N_EXPERTS, tm), 0)
    member = jnp.zeros((N_EXPERTS, tm), F32)
    for k in range(TOP_K):
        member = member + (e_iota == idx[k:k + 1, :]).astype(F32)
    earlier = (lax.broadcasted_iota(I32, (tm, tm), 0)
               < lax.broadcasted_iota(I32, (tm, tm), 1)).astype(BF16)
    rank = _dot(member.astype(BF16), earlier)
    pos = rank + offv_ref[0][:, 0:1]
    r_iota = lax.broadcasted_iota(I32, (MOE_TILE_ROWS, tm), 0)
    onehot = jnp.zeros((MOE_TILE_ROWS, tm), F32)
    dests = []
    for k in range(TOP_K):
        dk = jnp.sum(jnp.where(e_iota == idx[k:k + 1, :], pos, 0.0), axis=0, keepdims=True)
        dk = dk.astype(I32)
        dests.append(dk)
        onehot = onehot + (r_iota == dk).astype(F32)
    dest_ref[...] = jnp.concatenate(dests, axis=0)
    xs_s[...] = _dot(onehot.astype(BF16), x_ref[...].astype(BF16))

    def per_expert(e, total):
        n = len8_ref[s * N_EXPERTS + e]
        src0 = off8_ref[s * N_EXPERTS + e]
        dst0 = dst8_ref[s * N_EXPERTS + e]

        def issue(i, _):
            _group_copy(xs_s, xs_hbm, sem, src0 + i, dst0 + i).start()
            return 0

        lax.fori_loop(0, n, issue, 0)
        return total + n

    total = lax.fori_loop(0, N_EXPERTS, per_expert, 0)

    def drain(i, _):
        _group_copy(xs_s, xs_hbm, sem, 0, 0).wait()
        return 0

    lax.fori_loop(0, total, drain, 0)


def _moe_sort_call(x2, idx_t, offv, off8, len8, dst8, n_rows):
    t = x2.shape[0]
    tm = TOK_TILE
    grid_spec = pltpu.PrefetchScalarGridSpec(
        num_scalar_prefetch=3,
        grid=(t // tm,),
        in_specs=[pl.BlockSpec((tm, D_MODEL), lambda i, *_: (i, 0)),
                  pl.BlockSpec((TOP_K, tm), lambda i, *_: (0, i)),
                  pl.BlockSpec((1, N_EXPERTS, LANES), lambda i, *_: (i, 0, 0))],
        out_specs=[pl.BlockSpec((TOP_K, tm), lambda i, *_: (0, i)),
                   pl.BlockSpec(memory_space=pl.ANY)],
        scratch_shapes=[pltpu.VMEM((MOE_TILE_ROWS, D_MODEL), F32),
                        pltpu.SemaphoreType.DMA(())],
    )
    return pl.pallas_call(
        _moe_sort_kernel,
        grid_spec=grid_spec,
        out_shape=[jax.ShapeDtypeStruct((TOP_K, t), I32),
                   jax.ShapeDtypeStruct((n_rows, D_MODEL), F32)],
        compiler_params=_cparams(("arbitrary",)),
        name="moe_sort",
    )(off8, len8, dst8, x2, idx_t, offv)


def _moe_ffn_kernel(be_ref, nb_ref, xs_ref, w1_ref, b1_ref, w2_ref, b2_ref, o_ref):
    @pl.when(pl.program_id(0) < nb_ref[0])
    def _():
        hcat = _dot(xs_ref[...].astype(BF16), w1_ref[0]) + b1_ref[0]
        glu = jnp.minimum(hcat[:, :D_FF], SWIGLU_LIMIT)
        lin = jnp.clip(hcat[:, D_FF:], -SWIGLU_LIMIT, SWIGLU_LIMIT)
        act = glu * _sigmoid(SWIGLU_ALPHA * glu) * (lin + 1.0)
        o_ref[...] = _dot(act.astype(BF16), w2_ref[0]) + b2_ref[0]


def _moe_ffn_call(xs, block_e, nb_used, w1, b1, w2, b2):
    n_rows = xs.shape[0]
    nb = n_rows // MOE_BLOCK

    def blk(i, be, nbu):
        return (jnp.minimum(i, nbu[0] - 1), 0)

    grid_spec = pltpu.PrefetchScalarGridSpec(
        num_scalar_prefetch=2,
        grid=(nb,),
        in_specs=[pl.BlockSpec((MOE_BLOCK, D_MODEL), blk),
                  pl.BlockSpec((1, D_MODEL, 2 * D_FF), lambda i, be, nbu: (be[i], 0, 0)),
                  pl.BlockSpec((1, 1, 2 * D_FF), lambda i, be, nbu: (be[i], 0, 0)),
                  pl.BlockSpec((1, D_FF, D_MODEL), lambda i, be, nbu: (be[i], 0, 0)),
                  pl.BlockSpec((1, 1, D_MODEL), lambda i, be, nbu: (be[i], 0, 0))],
        out_specs=pl.BlockSpec((MOE_BLOCK, D_MODEL), blk),
    )
    return pl.pallas_call(
        _moe_ffn_kernel,
        grid_spec=grid_spec,
        out_shape=jax.ShapeDtypeStruct((n_rows, D_MODEL), F32),
        compiler_params=_cparams(("arbitrary",)),
        name="moe_ffn",
    )(block_e, nb_used, xs, w1, b1, w2, b2)


def _moe_combine_kernel(off8_ref, len8_ref, dst8_ref, tot8_ref, x_ref, dest_ref, gate_ref,
                        lg_ref, lb_ref, out_hbm, o_ref, buf, sem):
    s = pl.program_id(0)
    tm = x_ref.shape[0]

    def per_expert(e, total):
        n = len8_ref[s * N_EXPERTS + e]
        dst0 = off8_ref[s * N_EXPERTS + e]
        src0 = dst8_ref[s * N_EXPERTS + e]

        def issue(i, _):
            _group_copy(out_hbm, buf, sem, src0 + i, dst0 + i).start()
            return 0

        lax.fori_loop(0, n, issue, 0)
        return total + n

    total = lax.fori_loop(0, N_EXPERTS, per_expert, 0)

    def clear(g, _):
        buf[pl.ds(pl.multiple_of(g * MOE_GROUP, MOE_GROUP), MOE_GROUP), :] = jnp.zeros(
            (MOE_GROUP, D_MODEL), F32)
        return 0

    lax.fori_loop(tot8_ref[s], MOE_TILE_ROWS // MOE_GROUP, clear, 0)

    dest = dest_ref[...]
    gate = gate_ref[...]
    r_iota = lax.broadcasted_iota(I32, (tm, MOE_TILE_ROWS), 1)
    weights = jnp.zeros((tm, MOE_TILE_ROWS), F32)
    for k in range(TOP_K):
        weights = weights + jnp.where(r_iota == dest[:, k:k + 1], gate[:, k:k + 1], 0.0)

    def drain(i, _):
        _group_copy(out_hbm, buf, sem, 0, 0).wait()
        return 0

    lax.fori_loop(0, total, drain, 0)
    ff = _dot(weights.astype(BF16), buf[...].astype(BF16))
    o_ref[...] = _layer_norm(DEEPNORM_ALPHA * x_ref[...] + ff, lg_ref[...], lb_ref[...])


def _moe_combine_call(x2, dest, gate, out_rows, lg, lb, off8, len8, dst8, tot8):
    t = x2.shape[0]
    tm = TOK_TILE
    grid_spec = pltpu.PrefetchScalarGridSpec(
        num_scalar_prefetch=4,
        grid=(t // tm,),
        in_specs=[pl.BlockSpec((tm, D_MODEL), lambda i, *_: (i, 0)),
                  pl.BlockSpec((tm, TOP_K), lambda i, *_: (i, 0)),
                  pl.BlockSpec((tm, TOP_K), lambda i, *_: (i, 0)),
                  pl.BlockSpec((1, D_MODEL), lambda i, *_: (0, 0)),
                  pl.BlockSpec((1, D_MODEL), lambda i, *_: (0, 0)),
                  pl.BlockSpec(memory_space=pl.ANY)],
        out_specs=pl.BlockSpec((tm, D_MODEL), lambda i, *_: (i, 0)),
        scratch_shapes=[pltpu.VMEM((MOE_TILE_ROWS, D_MODEL), F32),
                        pltpu.SemaphoreType.DMA(())],
    )
    return pl.pallas_call(
        _moe_combine_kernel,
        grid_spec=grid_spec,
        out_shape=jax.ShapeDtypeStruct((t, D_MODEL), F32),
        compiler_params=_cparams(("arbitrary",)),
        name="moe_combine_ln3",
    )(off8, len8, dst8, tot8, x2, dest, gate, lg.reshape(1, -1), lb.reshape(1, -1), out_rows)


def _moe_tables(cnt, n_blocks):
    len8 = (cnt + MOE_GROUP - 1) // MOE_GROUP
    off8 = jnp.cumsum(len8, axis=1) - len8
    tot8 = jnp.sum(len8, axis=1)
    per_block = MOE_BLOCK // MOE_GROUP
    blocks_e = (jnp.sum(len8, axis=0) + per_block - 1) // per_block
    ends_e = jnp.cumsum(blocks_e)
    start8_e = (ends_e - blocks_e) * per_block
    dst8 = start8_e[None, :] + jnp.cumsum(len8, axis=0) - len8
    nb_used = ends_e[-1:]
    block_e = jnp.minimum(jnp.searchsorted(ends_e, jnp.arange(n_blocks, dtype=I32), side="right"),
                          N_EXPERTS - 1).astype(I32)
    return (off8.reshape(-1).astype(I32), len8.reshape(-1).astype(I32),
            dst8.reshape(-1).astype(I32), tot8.astype(I32), block_e, nb_used.astype(I32))


def _moe(x2, idx, gate, cnt, w1, b1, w2, b2, lg, lb):
    t = x2.shape[0]
    tiles = t // TOK_TILE
    max_rows = t * TOP_K + tiles * N_EXPERTS * (MOE_GROUP - 1)
    n_blocks = -(-max_rows // MOE_BLOCK) + N_EXPERTS
    off8, len8, dst8, tot8, block_e, nb_used = _moe_tables(cnt, n_blocks)
    offv = jnp.broadcast_to((off8.reshape(tiles, N_EXPERTS, 1) * MOE_GROUP).astype(F32),
                            (tiles, N_EXPERTS, LANES))
    dest_t, xs = _moe_sort_call(x2, idx.T, offv, off8, len8, dst8, n_blocks * MOE_BLOCK)
    out_rows = _moe_ffn_call(xs, block_e, nb_used, w1, b1, w2, b2)
    return _moe_combine_call(x2, dest_t.T, gate, out_rows, lg, lb, off8, len8, dst8, tot8)


def _rope_tables(seq):
    rows = seq // GRID_W
    row = jnp.repeat(jnp.arange(rows, dtype=F32), GRID_W)
    col = jnp.tile(jnp.arange(GRID_W, dtype=F32), rows)
    inv_freq = ROPE_THETA ** (-jnp.arange(ROPE_AXIS_HALF, dtype=F32) / ROPE_AXIS_HALF)
    ar = row[:, None] * inv_freq
    ac = col[:, None] * inv_freq
    cos_t = jnp.concatenate([jnp.cos(ar), jnp.cos(ar), jnp.cos(ac), jnp.cos(ac)], axis=1)
    sin_t = jnp.concatenate([-jnp.sin(ar), jnp.sin(ar), -jnp.sin(ac), jnp.sin(ac)], axis=1)
    return cos_t, sin_t


def _encode(x, mem, p):
    b, seq, _ = x.shape
    t = b * seq
    cos_t, sin_t = _rope_tables(seq)
    x = _ln_call(x.reshape(t, D_MODEL), p["ln_in_g"], p["ln_in_b"])
    c0 = D_ATTN_Q + 2 * D_ATTN_KV
    c1 = c0 + 3 * D_MODEL
    c2 = c1 + 2 * D_MODEL
    for l in range(DEPTH):
        w_in = p["w_in"][l].astype(BF16)
        q, k, vt = _qkv_call(x, w_in[:, :c0], cos_t, sin_t, p["q_norm_g"][l], p["k_norm_g"][l], seq)
        cb, pc = _convproj_call(x, w_in[:, c0:c1])
        gy, rx = _rnnproj_call(x, w_in[:, c1:c2])
        g = _gateproj_call(x, w_in[:, c2:], p["b_gate"][l].reshape(-1))
        ya = _attn_call(q.reshape(b, seq, -1), k.reshape(b, seq, -1), vt)
        h = _lru_call(rx.reshape(b, seq, -1), p["lru_conv_w"][l], p["lru_conv_b"][l],
                      p["lru_wa"][l].astype(BF16), p["lru_wx"][l].astype(BF16),
                      p["lru_ba"][l], p["lru_bx"][l], p["lru_lam"][l])
        x = _merge_call(x, cb, pc, ya.reshape(t, -1), h.reshape(t, -1), gy, g, p["conv_w"][l],
                        p["w_mix_out"][l].astype(BF16), p["ln1_g"][l], p["ln1_b"][l], seq)
        mk, mv = _kv_call(mem, p["xkv_w"][l].astype(BF16))
        rw = jnp.pad(p["router_w"][l], ((0, 0), (0, LANES - N_EXPERTS))).astype(BF16)
        rb = jnp.pad(p["router_b"][l], (0, LANES - N_EXPERTS), constant_values=NEG_BIG)
        x2, idx, gate, cnt = _cross_call(x, mk, mv, p["xq_w"][l].astype(BF16),
                                         p["xo_w"][l].astype(BF16), p["ln2_g"][l], p["ln2_b"][l],
                                         rw, rb.reshape(1, -1), seq)
        x = _moe(x2, idx[:, :TOP_K], gate[:, :TOP_K], cnt[:, 0, :N_EXPERTS],
                 p["w1"][l].astype(BF16), p["b1"][l][:, None, :], p["w2"][l].astype(BF16),
                 p["b2"][l][:, None, :], p["ln3_g"][l], p["ln3_b"][l])
    return x.reshape(b, seq, D_MODEL)


def kernel(x_prompt, x_sample, mem_prompt, mem_sample, ln_in_g, ln_in_b, w_in, b_gate, q_norm_g, k_norm_g, conv_w, lru_conv_w, lru_conv_b, lru_wa, lru_ba, lru_wx, lru_bx, lru_lam, w_mix_out, ln1_g, ln1_b, xq_w, xkv_w, xo_w, ln2_g, ln2_b, router_w, router_b, w1, b1, w2, b2, ln3_g, ln3_b):
    params = dict(ln_in_g=ln_in_g, ln_in_b=ln_in_b, w_in=w_in, b_gate=b_gate, q_norm_g=q_norm_g,
                  k_norm_g=k_norm_g, conv_w=conv_w, lru_conv_w=lru_conv_w, lru_conv_b=lru_conv_b,
                  lru_wa=lru_wa, lru_ba=lru_ba, lru_wx=lru_wx, lru_bx=lru_bx, lru_lam=lru_lam,
                  w_mix_out=w_mix_out, ln1_g=ln1_g, ln1_b=ln1_b, xq_w=xq_w, xkv_w=xkv_w, xo_w=xo_w,
                  ln2_g=ln2_g, ln2_b=ln2_b, router_w=router_w, router_b=router_b, w1=w1, b1=b1,
                  w2=w2, b2=b2, ln3_g=ln3_g, ln3_b=ln3_b)
    if x_prompt.shape[1:] == x_sample.shape[1:]:
        nb = x_prompt.shape[0]
        y = _encode(jnp.concatenate([x_prompt, x_sample], axis=0),
                    jnp.concatenate([mem_prompt, mem_sample], axis=0), params)
        return (y[:nb], y[nb:])
    return (_encode(x_prompt, mem_prompt, params), _encode(x_sample, mem_sample, params))
```

```python
import functools

import jax
import jax.numpy as jnp
from jax import lax
from jax.experimental import pallas as pl
from jax.experimental.pallas import tpu as pltpu

F32 = jnp.float32
BF16 = jnp.bfloat16
I32 = jnp.int32
U32 = jnp.uint32

D_MODEL = 1024
DEPTH = 2
N_HEADS = 8
N_KV_HEADS = 2
HEAD_DIM = 128
Q_GROUPS = N_HEADS // N_KV_HEADS
D_ATTN_Q = N_HEADS * HEAD_DIM
D_ATTN_KV = N_KV_HEADS * HEAD_DIM
GRID_W = 64
ROPE_AXIS_HALF = HEAD_DIM // 4
ROPE_THETA = 10000.0
QK_EPS = 1e-6
LRU_BLOCKS = 8
LRU_BLOCK_W = D_MODEL // LRU_BLOCKS
LRU_C = 8.0
MEM_HEADS = 4
MEM_HEAD_DIM = D_MODEL // MEM_HEADS
N_EXPERTS = 32
TOP_K = 4
D_FF = D_MODEL // 2
SWIGLU_LIMIT = 7.0
SWIGLU_ALPHA = 1.702
LN_EPS = 1e-5
DEEPNORM_ALPHA = (2 * DEPTH) ** 0.25

LANES = 128
SUBLANES = 8
BF16_SUBLANES = 16
VMEM_LIMIT = 56 * 1024 * 1024

PROJ_TM = 512
ATTN_TQ = 256
ATTN_TK = 512
TOK_TILE = 256
CROSS_TM = 512
LRU_CHUNK = 512
LRU_SUBSEQ = 16
LRU_UNROLL = 8
MOE_GROUP = SUBLANES
MOE_BLOCK = 512
MOE_HALF = 256
MOE_TILE_ROWS = TOK_TILE * TOP_K + N_EXPERTS * MOE_GROUP
NEG_BIG = -1e30
LOG2E = 1.4426950408889634


def _cparams(sem, vmem=VMEM_LIMIT):
    return pltpu.CompilerParams(dimension_semantics=sem, vmem_limit_bytes=vmem)


def _layer_norm(x, g, b):
    mu = jnp.mean(x, axis=-1, keepdims=True)
    xc = x - mu
    var = jnp.mean(xc * xc, axis=-1, keepdims=True)
    return xc * lax.rsqrt(var + LN_EPS) * g + b


def _sigmoid(x):
    return 1.0 / (1.0 + jnp.exp(-x))


def _sigmoid_tanh(x):
    return 0.5 * jnp.tanh(0.5 * x) + 0.5


def _dot(a, b):
    return jnp.dot(a, b, preferred_element_type=F32)


def _dot_nt(a, b):
    return lax.dot_general(a, b, (((1,), (1,)), ((), ())), preferred_element_type=F32)


def _ln_kernel(xa_ref, xb_ref, g_ref, b_ref, o_ref, *, tiles_a):
    @pl.when(pl.program_id(0) < tiles_a)
    def _():
        o_ref[...] = _layer_norm(xa_ref[...], g_ref[...], b_ref[...])

    @pl.when(pl.program_id(0) >= tiles_a)
    def _():
        o_ref[...] = _layer_norm(xb_ref[...], g_ref[...], b_ref[...])


def _ln_call(xa, xb, g, b):
    tm = PROJ_TM
    ta, tb = xa.shape[0] // tm, xb.shape[0] // tm
    return pl.pallas_call(
        functools.partial(_ln_kernel, tiles_a=ta),
        grid=(ta + tb,),
        in_specs=[pl.BlockSpec((tm, D_MODEL), lambda i: (jnp.minimum(i, ta - 1), 0)),
                  pl.BlockSpec((tm, D_MODEL), lambda i: (jnp.maximum(i - ta, 0), 0)),
                  pl.BlockSpec((1, D_MODEL), lambda i: (0, 0)),
                  pl.BlockSpec((1, D_MODEL), lambda i: (0, 0))],
        out_specs=pl.BlockSpec((tm, D_MODEL), lambda i: (i, 0)),
        out_shape=jax.ShapeDtypeStruct(((ta + tb) * tm, D_MODEL), F32),
        compiler_params=_cparams(("arbitrary",)),
        name="ln_in",
    )(xa, xb, g.reshape(1, -1), b.reshape(1, -1))


def _qkv_kernel(x_ref, w_ref, cos_ref, sin_ref, qg_ref, kg_ref, q_ref, k_ref, vt_ref):
    x = x_ref[...].astype(BF16)
    z = _dot(x, w_ref[...])
    c = cos_ref[...]
    s = sin_ref[...]
    lane = lax.broadcasted_iota(I32, c.shape, 1)
    first_half = (lane % (2 * ROPE_AXIS_HALF)) < ROPE_AXIS_HALF

    def prep(zh, g, scale):
        ms = jnp.mean(zh * zh, axis=-1, keepdims=True)
        y = zh * lax.rsqrt(ms + QK_EPS) * g
        partner = jnp.where(first_half,
                            pltpu.roll(y, HEAD_DIM - ROPE_AXIS_HALF, 1),
                            pltpu.roll(y, ROPE_AXIS_HALF, 1))
        return (y * c + partner * s) * scale

    qg = qg_ref[...]
    kg = kg_ref[...]
    for h in range(N_HEADS):
        sl = slice(h * HEAD_DIM, (h + 1) * HEAD_DIM)
        q_ref[:, sl] = prep(z[:, sl], qg, LOG2E * HEAD_DIM ** -0.5).astype(q_ref.dtype)
    for h in range(N_KV_HEADS):
        sl = slice(h * HEAD_DIM, (h + 1) * HEAD_DIM)
        zsl = slice(D_ATTN_Q + h * HEAD_DIM, D_ATTN_Q + (h + 1) * HEAD_DIM)
        k_ref[:, sl] = prep(z[:, zsl], kg, 1.0).astype(k_ref.dtype)
    vt_ref[0] = z[:, D_ATTN_Q + D_ATTN_KV:].T.astype(vt_ref.dtype)


def _qkv_call(x, w, cos_t, sin_t, qg, kg, seq):
    t = x.shape[0]
    tm = PROJ_TM
    nseq = seq // tm
    n = w.shape[1]
    return pl.pallas_call(
        _qkv_kernel,
        grid=(t // tm,),
        in_specs=[pl.BlockSpec((tm, D_MODEL), lambda i: (i, 0)),
                  pl.BlockSpec((D_MODEL, n), lambda i: (0, 0)),
                  pl.BlockSpec((tm, HEAD_DIM), lambda i: (i % nseq, 0)),
                  pl.BlockSpec((tm, HEAD_DIM), lambda i: (i % nseq, 0)),
                  pl.BlockSpec((1, HEAD_DIM), lambda i: (0, 0)),
                  pl.BlockSpec((1, HEAD_DIM), lambda i: (0, 0))],
        out_specs=[pl.BlockSpec((tm, D_ATTN_Q), lambda i: (i, 0)),
                   pl.BlockSpec((tm, D_ATTN_KV), lambda i: (i, 0)),
                   pl.BlockSpec((1, D_ATTN_KV, tm), lambda i: (i // nseq, 0, i % nseq))],
        out_shape=[jax.ShapeDtypeStruct((t, D_ATTN_Q), BF16),
                   jax.ShapeDtypeStruct((t, D_ATTN_KV), BF16),
                   jax.ShapeDtypeStruct((t // seq, D_ATTN_KV, seq), BF16)],
        compiler_params=_cparams(("parallel",)),
        name="qkv_proj",
    )(x, w, cos_t, sin_t, qg.reshape(1, -1), kg.reshape(1, -1))


def _convproj_kernel(x_ref, w_ref, cb_ref, p_ref):
    x = x_ref[...].astype(BF16)
    z = _dot(x, w_ref[...])
    cb_ref[...] = z[:, :D_MODEL].astype(cb_ref.dtype)
    p_ref[...] = (z[:, D_MODEL:2 * D_MODEL] * z[:, 2 * D_MODEL:]).astype(p_ref.dtype)


def _convproj_call(x, w):
    t = x.shape[0]
    tm = PROJ_TM
    return pl.pallas_call(
        _convproj_kernel,
        grid=(t // tm,),
        in_specs=[pl.BlockSpec((tm, D_MODEL), lambda i: (i, 0)),
                  pl.BlockSpec((D_MODEL, 3 * D_MODEL), lambda i: (0, 0))],
        out_specs=[pl.BlockSpec((tm, D_MODEL), lambda i: (i, 0)),
                   pl.BlockSpec((tm, D_MODEL), lambda i: (i, 0))],
        out_shape=[jax.ShapeDtypeStruct((t, D_MODEL), BF16),
                   jax.ShapeDtypeStruct((t, D_MODEL), BF16)],
        compiler_params=_cparams(("parallel",)),
        name="conv_proj",
    )(x, w)


def _rnnproj_kernel(x_ref, w_ref, gy_ref, rx_ref):
    x = x_ref[...].astype(BF16)
    z = _dot(x, w_ref[...])
    gy_ref[...] = jax.nn.gelu(z[:, :D_MODEL]).astype(gy_ref.dtype)
    rx_ref[...] = z[:, D_MODEL:]


def _rnnproj_call(x, w):
    t = x.shape[0]
    tm = PROJ_TM
    return pl.pallas_call(
        _rnnproj_kernel,
        grid=(t // tm,),
        in_specs=[pl.BlockSpec((tm, D_MODEL), lambda i: (i, 0)),
                  pl.BlockSpec((D_MODEL, 2 * D_MODEL), lambda i: (0, 0))],
        out_specs=[pl.BlockSpec((tm, D_MODEL), lambda i: (i, 0)),
                   pl.BlockSpec((tm, D_MODEL), lambda i: (i, 0))],
        out_shape=[jax.ShapeDtypeStruct((t, D_MODEL), BF16),
                   jax.ShapeDtypeStruct((t, D_MODEL), F32)],
        compiler_params=_cparams(("parallel",)),
        name="rnn_proj",
    )(x, w)


def _gateproj_kernel(x_ref, w_ref, b_ref, g_ref):
    x = x_ref[...].astype(BF16)
    z = _dot(x, w_ref[...]) + b_ref[...]
    g_ref[...] = _sigmoid(z).astype(g_ref.dtype)


def _gateproj_call(x, w, b):
    t = x.shape[0]
    tm = PROJ_TM
    return pl.pallas_call(
        _gateproj_kernel,
        grid=(t // tm,),
        in_specs=[pl.BlockSpec((tm, D_MODEL), lambda i: (i, 0)),
                  pl.BlockSpec((D_MODEL, 3 * D_MODEL), lambda i: (0, 0)),
                  pl.BlockSpec((1, 3 * D_MODEL), lambda i: (0, 0))],
        out_specs=pl.BlockSpec((tm, 3 * D_MODEL), lambda i: (i, 0)),
        out_shape=jax.ShapeDtypeStruct((t, 3 * D_MODEL), BF16),
        compiler_params=_cparams(("parallel",)),
        name="gate_proj",
    )(x, w, b.reshape(1, -1))


def _attn_kernel(q_ref, k_ref, vt_ref, o_ref, acc_s, s0_s, s1_s, p0_s, p1_s, *, tk):
    tq = q_ref.shape[1]
    n = Q_GROUPS * tq
    nk = k_ref.shape[1] // tk
    qs = jnp.concatenate([q_ref[0, :, g * HEAD_DIM:(g + 1) * HEAD_DIM] for g in range(Q_GROUPS)],
                         axis=0)
    acc_s[...] = jnp.zeros(acc_s.shape, F32)
    s_bufs = (s0_s, s1_s)
    p_bufs = (p0_s, p1_s)

    def scores(j, slot):
        start = pl.multiple_of(j * tk, tk)
        s_bufs[slot][...] = _dot_nt(k_ref[0, pl.ds(start, tk), :], qs)

    def softmax(slot, m, l):
        st = s_bufs[slot][...]
        m_new = jnp.maximum(m, jnp.max(st, axis=0, keepdims=True))
        alpha = jnp.exp2(m - m_new)
        p = jnp.exp2(st - m_new)
        p_bufs[slot][...] = p.astype(BF16)
        return m_new, alpha * l + jnp.sum(p, axis=0, keepdims=True), alpha

    def accumulate(j, slot, alpha):
        start = pl.multiple_of(j * tk, tk)
        acc_s[...] = alpha * acc_s[...] + _dot(vt_ref[0, :, pl.ds(start, tk)], p_bufs[slot][...])

    def stage(j, m, l, alphas, first, last):
        new_alphas = []
        for slot in range(2):
            if not first:
                accumulate(j - 2 + slot, slot, alphas[slot])
            m, l, a = softmax(slot, m, l)
            new_alphas.append(a)
            if not last:
                scores(j + 2 + slot, slot)
        return m, l, tuple(new_alphas)

    m = jnp.full((1, n), NEG_BIG, F32)
    l = jnp.zeros((1, n), F32)
    scores(0, 0)
    scores(1, 1)
    if nk == 2:
        m, l, alphas = stage(0, m, l, None, True, True)
    else:
        m, l, alphas = stage(0, m, l, None, True, False)
        m, l, alphas = lax.fori_loop(
            1, nk // 2 - 1, lambda i, c: stage(2 * i, c[0], c[1], c[2], False, False), (m, l, alphas))
        m, l, alphas = stage(nk - 2, m, l, alphas, False, True)
    accumulate(nk - 2, 0, alphas[0])
    accumulate(nk - 1, 1, alphas[1])
    ot = acc_s[...] / l
    for g in range(Q_GROUPS):
        o_ref[0, :, g * HEAD_DIM:(g + 1) * HEAD_DIM] = ot[:, g * tq:(g + 1) * tq].T.astype(o_ref.dtype)


def _attn_call(q, k, vt):
    b, s, _ = q.shape
    tq = min(ATTN_TQ, s)
    tk = min(ATTN_TK, s // 2)
    assert (s // tk) % 2 == 0
    gw = Q_GROUPS * HEAD_DIM
    n = Q_GROUPS * tq
    return pl.pallas_call(
        functools.partial(_attn_kernel, tk=tk),
        grid=(b, N_KV_HEADS, s // tq),
        in_specs=[pl.BlockSpec((1, tq, gw), lambda bi, h, i: (bi, i, h)),
                  pl.BlockSpec((1, s, HEAD_DIM), lambda bi, h, i: (bi, 0, h)),
                  pl.BlockSpec((1, HEAD_DIM, s), lambda bi, h, i: (bi, h, 0))],
        out_specs=pl.BlockSpec((1, tq, gw), lambda bi, h, i: (bi, i, h)),
        out_shape=jax.ShapeDtypeStruct((b, s, D_ATTN_Q), BF16),
        scratch_shapes=[pltpu.VMEM((HEAD_DIM, n), F32),
                        pltpu.VMEM((tk, n), F32), pltpu.VMEM((tk, n), F32),
                        pltpu.VMEM((tk, n), BF16), pltpu.VMEM((tk, n), BF16)],
        compiler_params=_cparams(("parallel", "parallel", "parallel")),
        name="flash_attn",
    )(q, k, vt)


def _lru_kernel(x_ref, cw_ref, cb_ref, wa_ref, wx_ref, ba_ref, bx_ref, lam_ref, o_hbm,
                xp_s, a0_s, u0_s, a1_s, u1_s, h_s, sem, *, seq, chunk):
    sub = seq // LRU_SUBSEQ
    cps = sub // chunk
    nchunks = seq // chunk

    zeros8 = jnp.zeros((SUBLANES, LANES), F32)
    xp_s[0:SUBLANES, :] = zeros8
    xp_s[seq + SUBLANES:seq + 2 * SUBLANES, :] = zeros8
    xp_s[SUBLANES:seq + SUBLANES, :] = x_ref[0]
    cw = cw_ref[...]
    cb = cb_ref[...]
    a_bufs = (a0_s, a1_s)
    u_bufs = (u0_s, u1_s)

    neg = -lam_ref[...]
    softplus = jnp.maximum(neg, 0.0) + jnp.log1p(jnp.exp(-jnp.abs(neg)))
    decay = -LRU_C * softplus

    def gate_chunk(c, _):
        base = pl.multiple_of(c * chunk, chunk)
        xc = cb
        for k in range(4):
            xc = xc + cw[k:k + 1, :] * xp_s[pl.ds(base + SUBLANES - 1 + k, chunk), :]
        xb = xc.astype(BF16)
        dst = pl.ds((c % cps) * (chunk * LRU_SUBSEQ) + c // cps, chunk, stride=LRU_SUBSEQ)
        for d in range(2):
            r = _sigmoid_tanh(_dot(xb, wa_ref[d, 0]) + ba_ref[d:d + 1, :])
            i = _sigmoid_tanh(_dot(xb, wx_ref[d, 0]) + bx_ref[d:d + 1, :])
            a = jnp.exp(decay[d:d + 1, :] * r)
            u_bufs[d][dst, :] = jnp.sqrt(1.0 - a * a) * (i * xc)
            a_bufs[d][dst, :] = a
        return 0

    lax.fori_loop(0, nchunks, gate_chunk, 0)

    def step(tt, carry):
        out = []
        for d in range(2):
            h, p = carry[d]
            t = tt if d == 0 else sub - 1 - tt
            idx = pl.ds(pl.multiple_of(t * LRU_SUBSEQ, LRU_SUBSEQ), LRU_SUBSEQ)
            av = a_bufs[d][idx, :]
            h = av * h + u_bufs[d][idx, :]
            p = av * p
            u_bufs[d][idx, :] = h
            a_bufs[d][idx, :] = p
            out.append((h, p))
        return tuple(out)

    h0 = jnp.zeros((LRU_SUBSEQ, LANES), F32)
    p0 = jnp.ones((LRU_SUBSEQ, LANES), F32)
    ends = lax.fori_loop(0, sub, step, ((h0, p0), (h0, p0)), unroll=LRU_UNROLL)

    carry_in = []
    for d in range(2):
        h_end, p_end = ends[d]
        order = range(LRU_SUBSEQ) if d == 0 else range(LRU_SUBSEQ - 1, -1, -1)
        states = [None] * LRU_SUBSEQ
        state = jnp.zeros((1, LANES), F32)
        for j in order:
            states[j] = state
            state = p_end[j:j + 1, :] * state + h_end[j:j + 1, :]
        carry_in.append(jnp.concatenate(states, axis=0))

    step_id = pl.program_id(0) * pl.num_programs(1) + pl.program_id(1)
    last_id = pl.num_programs(0) * pl.num_programs(1) - 1

    def out_copies(bi, ci):
        return [pltpu.make_async_copy(
            h_s.at[:, j, :],
            o_hbm.at[bi, pl.ds(j * sub, sub), pl.ds(pl.multiple_of(ci * LANES, LANES), LANES)],
            sem) for j in range(LRU_SUBSEQ)]

    @pl.when(step_id > 0)
    def _():
        for cp in out_copies(0, 0):
            cp.wait()

    steps = min(sub, chunk)
    for c in range(sub // steps):
        rows = slice(c * steps * LRU_SUBSEQ, (c + 1) * steps * LRU_SUBSEQ)
        c0 = jnp.tile(carry_in[0], (steps, 1))
        c1 = jnp.tile(carry_in[1], (steps, 1))
        val = u0_s[rows, :] + a0_s[rows, :] * c0 + u1_s[rows, :] + a1_s[rows, :] * c1
        h_s[c * steps:(c + 1) * steps] = val.reshape(steps, LRU_SUBSEQ, LANES)

    for cp in out_copies(pl.program_id(0), pl.program_id(1)):
        cp.start()

    @pl.when(step_id == last_id)
    def _():
        for cp in out_copies(0, 0):
            cp.wait()


def _lru_call(rx, cw, cb, wa, wx, ba, bx, lam):
    b, s, _ = rx.shape
    chunk = min(LRU_CHUNK, s // LRU_SUBSEQ)
    assert LRU_SUBSEQ % SUBLANES == 0
    w = LRU_BLOCK_W
    return pl.pallas_call(
        functools.partial(_lru_kernel, seq=s, chunk=chunk),
        grid=(b, LRU_BLOCKS),
        in_specs=[pl.BlockSpec((1, s, w), lambda bi, c: (bi, 0, c)),
                  pl.BlockSpec((4, w), lambda bi, c: (0, c)),
                  pl.BlockSpec((1, w), lambda bi, c: (0, c)),
                  pl.BlockSpec((2, 1, w, w), lambda bi, c: (0, c, 0, 0)),
                  pl.BlockSpec((2, 1, w, w), lambda bi, c: (0, c, 0, 0)),
                  pl.BlockSpec((2, w), lambda bi, c: (0, c)),
                  pl.BlockSpec((2, w), lambda bi, c: (0, c)),
                  pl.BlockSpec((2, w), lambda bi, c: (0, c))],
        out_specs=pl.BlockSpec(memory_space=pl.ANY),
        out_shape=jax.ShapeDtypeStruct((b, s, D_MODEL), F32),
        scratch_shapes=[pltpu.VMEM((s + 2 * SUBLANES, w), F32)]
        + [pltpu.VMEM((s, w), F32)] * 4
        + [pltpu.VMEM((s // LRU_SUBSEQ, LRU_SUBSEQ, w), F32), pltpu.SemaphoreType.DMA(())],
        compiler_params=_cparams(("arbitrary", "arbitrary")),
        name="rg_lru",
    )(rx, cw, cb.reshape(1, -1), wa, wx, ba, bx, lam)


def _merge_kernel(x_ref, cb_ref, p_ref, pprev_ref, pnext_ref, ya_ref, h_ref, gy_ref, g_ref,
                  cw_ref, w_ref, lg_ref, lb_ref, o_ref, *, seq):
    tm = x_ref.shape[0]
    t0 = pl.program_id(0) * tm
    p = p_ref[...].astype(F32)
    row = lax.broadcasted_iota(I32, p.shape, 0)
    has_prev = ((t0 % seq) != 0).astype(F32)
    has_next = (((t0 + tm) % seq) != 0).astype(F32)
    prev_row = pprev_ref[BF16_SUBLANES - 1:BF16_SUBLANES, :].astype(F32) * has_prev
    next_row = pnext_ref[0:1, :].astype(F32) * has_next
    p_dn = jnp.where(row == 0, prev_row, pltpu.roll(p, 1, 0))
    p_up = jnp.where(row == tm - 1, next_row, pltpu.roll(p, tm - 1, 0))
    cw = cw_ref[...]
    y_conv = cb_ref[...].astype(F32) * (cw[0:1, :] * p_dn + cw[1:2, :] * p + cw[2:3, :] * p_up)
    y_rnn = h_ref[...] * gy_ref[...].astype(F32)
    merged = (g_ref[:, :D_MODEL].astype(F32) * y_conv
              + g_ref[:, D_MODEL:2 * D_MODEL].astype(F32) * ya_ref[...].astype(F32)
              + g_ref[:, 2 * D_MODEL:].astype(F32) * y_rnn)
    mix = _dot(merged.astype(BF16), w_ref[...])
    o_ref[...] = _layer_norm(DEEPNORM_ALPHA * x_ref[...] + mix, lg_ref[...], lb_ref[...])


def _merge_call(x, cb, p, ya, h, gy, g, cw, w, lg, lb, seq):
    t = x.shape[0]
    tm = TOK_TILE
    hb = BF16_SUBLANES
    per = tm // hb
    last = t // hb - 1
    row_spec = pl.BlockSpec((tm, D_MODEL), lambda i: (i, 0))
    vec_spec = pl.BlockSpec((1, D_MODEL), lambda i: (0, 0))
    return pl.pallas_call(
        functools.partial(_merge_kernel, seq=seq),
        grid=(t // tm,),
        in_specs=[row_spec, row_spec, row_spec,
                  pl.BlockSpec((hb, D_MODEL), lambda i: (jnp.maximum(i * per - 1, 0), 0)),
                  pl.BlockSpec((hb, D_MODEL), lambda i: (jnp.minimum((i + 1) * per, last), 0)),
                  row_spec, row_spec, row_spec,
                  pl.BlockSpec((tm, 3 * D_MODEL), lambda i: (i, 0)),
                  pl.BlockSpec((3, D_MODEL), lambda i: (0, 0)),
                  pl.BlockSpec((D_MODEL, D_MODEL), lambda i: (0, 0)),
                  vec_spec, vec_spec],
        out_specs=row_spec,
        out_shape=jax.ShapeDtypeStruct((t, D_MODEL), F32),
        compiler_params=_cparams(("parallel",)),
        name="merge_outproj_ln1",
    )(x, cb, p, p, p, ya, h, gy, g, cw, w, lg.reshape(1, -1), lb.reshape(1, -1))


def _kv_kernel(m_ref, w_ref, k_ref, v_ref):
    z = _dot(m_ref[0].astype(BF16), w_ref[...])
    k_ref[0] = z[:, :D_MODEL].astype(k_ref.dtype)
    v_ref[0] = z[:, D_MODEL:].astype(v_ref.dtype)


def _kv_call(mem, w):
    b, m, _ = mem.shape
    return pl.pallas_call(
        _kv_kernel,
        grid=(b,),
        in_specs=[pl.BlockSpec((1, m, D_MODEL), lambda i: (i, 0, 0)),
                  pl.BlockSpec((D_MODEL, 2 * D_MODEL), lambda i: (0, 0))],
        out_specs=[pl.BlockSpec((1, m, D_MODEL), lambda i: (i, 0, 0)),
                   pl.BlockSpec((1, m, D_MODEL), lambda i: (i, 0, 0))],
        out_shape=[jax.ShapeDtypeStruct((b, m, D_MODEL), BF16),
                   jax.ShapeDtypeStruct((b, m, D_MODEL), BF16)],
        compiler_params=_cparams(("parallel",)),
        name="mem_kv_proj",
    )(mem, w)


def _cross_kernel(x_ref, k_ref, v_ref, wq_ref, wo_ref, lg_ref, lb_ref, rw_ref, rb_ref,
                  x2_ref, idx_ref, gate_ref, cnt_ref):
    for r in range(x_ref.shape[0] // TOK_TILE):
        rows = slice(r * TOK_TILE, (r + 1) * TOK_TILE)
        _cross_rows(x_ref[rows, :], k_ref, v_ref, wq_ref, wo_ref, lg_ref, lb_ref, rw_ref, rb_ref,
                    x2_ref.at[rows, :], idx_ref.at[rows, :], gate_ref.at[rows, :], cnt_ref.at[r])


def _cross_rows(x, k_ref, v_ref, wq_ref, wo_ref, lg_ref, lb_ref, rw_ref, rb_ref,
                x2_ref, idx_ref, gate_ref, cnt_ref):
    q = _dot(x.astype(BF16), wq_ref[...]) * (MEM_HEAD_DIM ** -0.5)
    outs = []
    for h in range(MEM_HEADS):
        sl = slice(h * MEM_HEAD_DIM, (h + 1) * MEM_HEAD_DIM)
        s = _dot_nt(q[:, sl].astype(BF16), k_ref[0, :, sl])
        m = jnp.max(s, axis=-1, keepdims=True)
        p = jnp.exp(s - m)
        p = p / jnp.sum(p, axis=-1, keepdims=True)
        outs.append(_dot(p.astype(BF16), v_ref[0, :, sl]))
    o = jnp.concatenate(outs, axis=1)
    cross = _dot(o.astype(BF16), wo_ref[...])
    x2 = _layer_norm(DEEPNORM_ALPHA * x + cross, lg_ref[...], lb_ref[...])
    x2_ref[...] = x2

    logits = _dot(x2.astype(BF16), rw_ref[...]) + rb_ref[...]
    lane = lax.broadcasted_iota(I32, logits.shape, 1)
    idx_acc = jnp.zeros(logits.shape, I32)
    val_acc = jnp.zeros(logits.shape, F32)
    member = jnp.zeros(logits.shape, F32)
    top0 = None
    for k in range(TOP_K):
        m = jnp.max(logits, axis=-1, keepdims=True)
        pick = jnp.min(jnp.where(logits == m, lane, LANES), axis=-1, keepdims=True)
        hit = lane == pick
        if k == 0:
            top0 = m
        idx_acc = jnp.where(lane == k, pick, idx_acc)
        val_acc = jnp.where(lane == k, jnp.exp(m - top0), val_acc)
        member = member + hit.astype(F32)
        logits = jnp.where(hit, NEG_BIG, logits)
    idx_ref[...] = idx_acc
    gate_ref[...] = val_acc / jnp.sum(val_acc, axis=-1, keepdims=True)
    cnt_ref[...] = jnp.sum(member, axis=0, keepdims=True).astype(I32)


def _cross_call(x, k, v, wq, wo, lg, lb, rw, rb, seq):
    t = x.shape[0]
    tm = CROSS_TM
    halves = tm // TOK_TILE
    per_seq = seq // tm
    m = k.shape[1]
    row_spec = pl.BlockSpec((tm, D_MODEL), lambda i: (i, 0))
    vec_spec = pl.BlockSpec((1, D_MODEL), lambda i: (0, 0))
    lane_spec = pl.BlockSpec((tm, LANES), lambda i: (i, 0))
    return pl.pallas_call(
        _cross_kernel,
        grid=(t // tm,),
        in_specs=[row_spec,
                  pl.BlockSpec((1, m, D_MODEL), lambda i: (i // per_seq, 0, 0)),
                  pl.BlockSpec((1, m, D_MODEL), lambda i: (i // per_seq, 0, 0)),
                  pl.BlockSpec((D_MODEL, D_MODEL), lambda i: (0, 0)),
                  pl.BlockSpec((D_MODEL, D_MODEL), lambda i: (0, 0)),
                  vec_spec, vec_spec,
                  pl.BlockSpec((D_MODEL, LANES), lambda i: (0, 0)),
                  pl.BlockSpec((1, LANES), lambda i: (0, 0))],
        out_specs=[row_spec, lane_spec, lane_spec,
                   pl.BlockSpec((halves, 1, LANES), lambda i: (i, 0, 0))],
        out_shape=[jax.ShapeDtypeStruct((t, D_MODEL), F32),
                   jax.ShapeDtypeStruct((t, LANES), I32),
                   jax.ShapeDtypeStruct((t, LANES), F32),
                   jax.ShapeDtypeStruct((t // TOK_TILE, 1, LANES), I32)],
        compiler_params=_cparams(("parallel",)),
        name="cross_attn_ln2_router",
    )(x, k, v, wq, wo, lg.reshape(1, -1), lb.reshape(1, -1), rw, rb)


def _pack_rows(v):
    bits = pltpu.bitcast(v, U32)
    half = v.shape[1] // 2
    return (bits[:, :half] & jnp.uint32(0xFFFF0000)) | (bits[:, half:] >> 16)


def _unpack_rows(u):
    hi = pltpu.bitcast(u & jnp.uint32(0xFFFF0000), F32).astype(BF16)
    lo = pltpu.bitcast(u << 16, F32).astype(BF16)
    return hi, lo


def _group_copy(src, dst, sem, s_grp, d_grp):
    return pltpu.make_async_copy(
        src.at[pl.ds(pl.multiple_of(s_grp * MOE_GROUP, MOE_GROUP), MOE_GROUP), :],
        dst.at[pl.ds(pl.multiple_of(d_grp * MOE_GROUP, MOE_GROUP), MOE_GROUP), :], sem)


def _moe_sort_kernel(off8_ref, len8_ref, dst8_ref, x_ref, idx_ref, offv_ref, dest_ref, xs_hbm,
                     xs_s, sem, pending_s):
    s = pl.program_id(0)
    tm = x_ref.shape[0]
    idx = idx_ref[...]
    e_iota = lax.broadcasted_iota(I32, (N_EXPERTS, tm), 0)
    member = jnp.zeros((N_EXPERTS, tm), F32)
    for k in range(TOP_K):
        member = member + (e_iota == idx[k:k + 1, :]).astype(F32)
    earlier = (lax.broadcasted_iota(I32, (tm, tm), 0)
               < lax.broadcasted_iota(I32, (tm, tm), 1)).astype(BF16)
    rank = _dot(member.astype(BF16), earlier)
    pos = rank + offv_ref[0][:, 0:1]
    r_iota = lax.broadcasted_iota(I32, (MOE_TILE_ROWS, tm), 0)
    onehot = jnp.zeros((MOE_TILE_ROWS, tm), F32)
    dests = []
    for k in range(TOP_K):
        dk = jnp.sum(jnp.where(e_iota == idx[k:k + 1, :], pos, 0.0), axis=0, keepdims=True)
        dk = dk.astype(I32)
        dests.append(dk)
        onehot = onehot + (r_iota == dk).astype(F32)
    dest_ref[...] = jnp.concatenate(dests, axis=0)
    slot = s % 2
    xs_s[slot] = _pack_rows(_dot(onehot.astype(BF16), x_ref[...].astype(BF16)))

    def per_expert(e, total):
        n = len8_ref[s * N_EXPERTS + e]
        src0 = off8_ref[s * N_EXPERTS + e]
        dst0 = dst8_ref[s * N_EXPERTS + e]

        def issue(i, _):
            _group_copy(xs_s.at[slot], xs_hbm, sem.at[slot], src0 + i, dst0 + i).start()
            return 0

        lax.fori_loop(0, n, issue, 0)
        return total + n

    total = lax.fori_loop(0, N_EXPERTS, per_expert, 0)

    def drain(which, count):
        def body(i, _):
            _group_copy(xs_s.at[which], xs_hbm, sem.at[which], 0, 0).wait()
            return 0

        lax.fori_loop(0, count, body, 0)

    @pl.when(s > 0)
    def _():
        drain(1 - slot, pending_s[0])

    pending_s[0] = total

    @pl.when(s == pl.num_programs(0) - 1)
    def _():
        drain(slot, total)


def _moe_sort_call(x2, idx_t, offv, off8, len8, dst8, n_rows):
    t = x2.shape[0]
    tm = TOK_TILE
    grid_spec = pltpu.PrefetchScalarGridSpec(
        num_scalar_prefetch=3,
        grid=(t // tm,),
        in_specs=[pl.BlockSpec((tm, D_MODEL), lambda i, *_: (i, 0)),
                  pl.BlockSpec((TOP_K, tm), lambda i, *_: (0, i)),
                  pl.BlockSpec((1, N_EXPERTS, LANES), lambda i, *_: (i, 0, 0))],
        out_specs=[pl.BlockSpec((TOP_K, tm), lambda i, *_: (0, i)),
                   pl.BlockSpec(memory_space=pl.ANY)],
        scratch_shapes=[pltpu.VMEM((2, MOE_TILE_ROWS, D_MODEL // 2), U32),
                        pltpu.SemaphoreType.DMA((2,)),
                        pltpu.SMEM((1,), I32)],
    )
    return pl.pallas_call(
        _moe_sort_kernel,
        grid_spec=grid_spec,
        out_shape=[jax.ShapeDtypeStruct((TOP_K, t), I32),
                   jax.ShapeDtypeStruct((n_rows, D_MODEL // 2), U32)],
        compiler_params=_cparams(("arbitrary",)),
        name="moe_sort",
    )(off8, len8, dst8, x2, idx_t, offv)


def _moe_ffn_kernel(be_ref, nb_ref, xs_ref, w1_ref, b1_ref, w2_ref, b2_ref, o_ref):
    @pl.when(pl.program_id(0) < nb_ref[0])
    def _():
        half = D_MODEL // 2
        for r in range(MOE_BLOCK // MOE_HALF):
            rows = slice(r * MOE_HALF, (r + 1) * MOE_HALF)
            hi, lo = _unpack_rows(xs_ref[rows, :])
            hcat = _dot(hi, w1_ref[0, :half, :]) + _dot(lo, w1_ref[0, half:, :]) + b1_ref[0]
            glu = jnp.minimum(hcat[:, :D_FF], SWIGLU_LIMIT)
            lin = jnp.clip(hcat[:, D_FF:], -SWIGLU_LIMIT, SWIGLU_LIMIT)
            act = glu * _sigmoid_tanh(SWIGLU_ALPHA * glu) * (lin + 1.0)
            out = _dot(act.astype(BF16), w2_ref[0]) + b2_ref[0]
            o_ref[rows, :] = _pack_rows(out.astype(BF16).astype(F32))


def _moe_ffn_call(xs, block_e, nb_used, w1, b1, w2, b2):
    n_rows = xs.shape[0]
    nb = n_rows // MOE_BLOCK

    def blk(i, be, nbu):
        return (jnp.minimum(i, nbu[0] - 1), 0)

    grid_spec = pltpu.PrefetchScalarGridSpec(
        num_scalar_prefetch=2,
        grid=(nb,),
        in_specs=[pl.BlockSpec((MOE_BLOCK, D_MODEL // 2), blk),
                  pl.BlockSpec((1, D_MODEL, 2 * D_FF), lambda i, be, nbu: (be[i], 0, 0)),
                  pl.BlockSpec((1, 1, 2 * D_FF), lambda i, be, nbu: (be[i], 0, 0)),
                  pl.BlockSpec((1, D_FF, D_MODEL), lambda i, be, nbu: (be[i], 0, 0)),
                  pl.BlockSpec((1, 1, D_MODEL), lambda i, be, nbu: (be[i], 0, 0))],
        out_specs=pl.BlockSpec((MOE_BLOCK, D_MODEL // 2), blk),
    )
    return pl.pallas_call(
        _moe_ffn_kernel,
        grid_spec=grid_spec,
        out_shape=jax.ShapeDtypeStruct((n_rows, D_MODEL // 2), U32),
        compiler_params=_cparams(("arbitrary",)),
        name="moe_ffn",
    )(block_e, nb_used, xs, w1, b1, w2, b2)


def _moe_combine_kernel(off8_ref, len8_ref, dst8_ref, tot8_ref, x_ref, dest_ref, gate_ref,
                        lg_ref, lb_ref, out_hbm, *refs, tiles_a):
    o_refs, (buf, sem) = refs[:-2], refs[-2:]
    s = pl.program_id(0)
    tm = x_ref.shape[0]
    slot = s % 2

    def gather(tile):
        which = tile % 2

        def per_expert(e, _):
            n = len8_ref[tile * N_EXPERTS + e]
            dst0 = off8_ref[tile * N_EXPERTS + e]
            src0 = dst8_ref[tile * N_EXPERTS + e]

            def issue(i, _):
                _group_copy(out_hbm, buf.at[which], sem.at[which], src0 + i, dst0 + i).start()
                return 0

            lax.fori_loop(0, n, issue, 0)
            return 0

        lax.fori_loop(0, N_EXPERTS, per_expert, 0)

    @pl.when(s == 0)
    def _():
        gather(s)

    @pl.when(s + 1 < pl.num_programs(0))
    def _():
        gather(s + 1)

    def clear(g, _):
        buf[slot, pl.ds(pl.multiple_of(g * MOE_GROUP, MOE_GROUP), MOE_GROUP), :] = jnp.zeros(
            (MOE_GROUP, D_MODEL // 2), U32)
        return 0

    lax.fori_loop(tot8_ref[s], MOE_TILE_ROWS // MOE_GROUP, clear, 0)

    dest = dest_ref[...]
    gate = gate_ref[...]
    r_iota = lax.broadcasted_iota(I32, (tm, MOE_TILE_ROWS), 1)
    weights = jnp.zeros((tm, MOE_TILE_ROWS), F32)
    for k in range(TOP_K):
        weights = weights + jnp.where(r_iota == dest[:, k:k + 1], gate[:, k:k + 1], 0.0)

    def drain(i, _):
        _group_copy(out_hbm, buf.at[slot], sem.at[slot], 0, 0).wait()
        return 0

    lax.fori_loop(0, tot8_ref[s], drain, 0)
    hi, lo = _unpack_rows(buf[slot])
    wb = weights.astype(BF16)
    ff = jnp.concatenate([_dot(wb, hi), _dot(wb, lo)], axis=1)
    y = _layer_norm(DEEPNORM_ALPHA * x_ref[...] + ff, lg_ref[...], lb_ref[...])
    if tiles_a is None:
        o_refs[0][...] = y
    else:
        @pl.when(s < tiles_a)
        def _():
            o_refs[0][...] = y

        @pl.when(s >= tiles_a)
        def _():
            o_refs[1][...] = y


def _moe_combine_call(x2, dest, gate, out_rows, lg, lb, off8, len8, dst8, tot8, split_rows):
    t = x2.shape[0]
    tm = TOK_TILE
    if split_rows is None:
        tiles_a = None
        out_specs = pl.BlockSpec((tm, D_MODEL), lambda i, *_: (i, 0))
        out_shape = jax.ShapeDtypeStruct((t, D_MODEL), F32)
    else:
        tiles_a = split_rows // tm
        out_specs = [pl.BlockSpec((tm, D_MODEL), lambda i, *_: (jnp.minimum(i, tiles_a - 1), 0)),
                     pl.BlockSpec((tm, D_MODEL), lambda i, *_: (jnp.maximum(i - tiles_a, 0), 0))]
        out_shape = [jax.ShapeDtypeStruct((split_rows, D_MODEL), F32),
                     jax.ShapeDtypeStruct((t - split_rows, D_MODEL), F32)]
    grid_spec = pltpu.PrefetchScalarGridSpec(
        num_scalar_prefetch=4,
        grid=(t // tm,),
        in_specs=[pl.BlockSpec((tm, D_MODEL), lambda i, *_: (i, 0)),
                  pl.BlockSpec((tm, TOP_K), lambda i, *_: (i, 0)),
                  pl.BlockSpec((tm, TOP_K), lambda i, *_: (i, 0)),
                  pl.BlockSpec((1, D_MODEL), lambda i, *_: (0, 0)),
                  pl.BlockSpec((1, D_MODEL), lambda i, *_: (0, 0)),
                  pl.BlockSpec(memory_space=pl.ANY)],
        out_specs=out_specs,
        scratch_shapes=[pltpu.VMEM((2, MOE_TILE_ROWS, D_MODEL // 2), U32),
                        pltpu.SemaphoreType.DMA((2,))],
    )
    return pl.pallas_call(
        functools.partial(_moe_combine_kernel, tiles_a=tiles_a),
        grid_spec=grid_spec,
        out_shape=out_shape,
        compiler_params=_cparams(("arbitrary",)),
        name="moe_combine_ln3",
    )(off8, len8, dst8, tot8, x2, dest, gate, lg.reshape(1, -1), lb.reshape(1, -1), out_rows)


def _moe_tables(cnt, n_blocks):
    len8 = (cnt + MOE_GROUP - 1) // MOE_GROUP
    off8 = jnp.cumsum(len8, axis=1) - len8
    tot8 = jnp.sum(len8, axis=1)
    per_block = MOE_BLOCK // MOE_GROUP
    blocks_e = (jnp.sum(len8, axis=0) + per_block - 1) // per_block
    ends_e = jnp.cumsum(blocks_e)
    start8_e = (ends_e - blocks_e) * per_block
    dst8 = start8_e[None, :] + jnp.cumsum(len8, axis=0) - len8
    nb_used = ends_e[-1:]
    block_e = jnp.minimum(jnp.sum(ends_e[None, :] <= jnp.arange(n_blocks, dtype=I32)[:, None], axis=1),
                          N_EXPERTS - 1).astype(I32)
    return (off8.reshape(-1).astype(I32), len8.reshape(-1).astype(I32),
            dst8.reshape(-1).astype(I32), tot8.astype(I32), block_e, nb_used.astype(I32))


def _moe(x2, idx, gate, cnt, w1, b1, w2, b2, lg, lb, split_rows=None):
    t = x2.shape[0]
    tiles = t // TOK_TILE
    max_rows = t * TOP_K + tiles * N_EXPERTS * (MOE_GROUP - 1)
    n_blocks = -(-max_rows // MOE_BLOCK) + N_EXPERTS
    off8, len8, dst8, tot8, block_e, nb_used = _moe_tables(cnt, n_blocks)
    offv = jnp.broadcast_to((off8.reshape(tiles, N_EXPERTS, 1) * MOE_GROUP).astype(F32),
                            (tiles, N_EXPERTS, LANES))
    dest_t, xs = _moe_sort_call(x2, idx.T, offv, off8, len8, dst8, n_blocks * MOE_BLOCK)
    out_rows = _moe_ffn_call(xs, block_e, nb_used, w1, b1, w2, b2)
    return _moe_combine_call(x2, dest_t.T, gate, out_rows, lg, lb, off8, len8, dst8, tot8,
                             split_rows)


def _rope_tables(seq):
    rows = seq // GRID_W
    row = jnp.repeat(jnp.arange(rows, dtype=F32), GRID_W)
    col = jnp.tile(jnp.arange(GRID_W, dtype=F32), rows)
    inv_freq = ROPE_THETA ** (-jnp.arange(ROPE_AXIS_HALF, dtype=F32) / ROPE_AXIS_HALF)
    ar = row[:, None] * inv_freq
    ac = col[:, None] * inv_freq
    cos_t = jnp.concatenate([jnp.cos(ar), jnp.cos(ar), jnp.cos(ac), jnp.cos(ac)], axis=1)
    sin_t = jnp.concatenate([-jnp.sin(ar), jnp.sin(ar), -jnp.sin(ac), jnp.sin(ac)], axis=1)
    return cos_t, sin_t


def _encode(xa, xb, mem, p):
    seq = xa.shape[1]
    b = xa.shape[0] + xb.shape[0]
    t = b * seq
    ta = xa.shape[0] * seq
    cos_t, sin_t = _rope_tables(seq)
    x = _ln_call(xa.reshape(ta, D_MODEL), xb.reshape(t - ta, D_MODEL), p["ln_in_g"], p["ln_in_b"])
    c0 = D_ATTN_Q + 2 * D_ATTN_KV
    c1 = c0 + 3 * D_MODEL
    c2 = c1 + 2 * D_MODEL
    for l in range(DEPTH):
        w_in = p["w_in"][l].astype(BF16)
        q, k, vt = _qkv_call(x, w_in[:, :c0], cos_t, sin_t, p["q_norm_g"][l], p["k_norm_g"][l], seq)
        cb, pc = _convproj_call(x, w_in[:, c0:c1])
        gy, rx = _rnnproj_call(x, w_in[:, c1:c2])
        g = _gateproj_call(x, w_in[:, c2:], p["b_gate"][l].reshape(-1))
        ya = _attn_call(q.reshape(b, seq, -1), k.reshape(b, seq, -1), vt)
        h = _lru_call(rx.reshape(b, seq, -1), p["lru_conv_w"][l], p["lru_conv_b"][l],
                      p["lru_wa"][l].astype(BF16), p["lru_wx"][l].astype(BF16),
                      p["lru_ba"][l], p["lru_bx"][l], p["lru_lam"][l])
        x = _merge_call(x, cb, pc, ya.reshape(t, -1), h.reshape(t, -1), gy, g, p["conv_w"][l],
                        p["w_mix_out"][l].astype(BF16), p["ln1_g"][l], p["ln1_b"][l], seq)
        mk, mv = _kv_call(mem, p["xkv_w"][l].astype(BF16))
        rw = jnp.pad(p["router_w"][l], ((0, 0), (0, LANES - N_EXPERTS))).astype(BF16)
        rb = jnp.pad(p["router_b"][l], (0, LANES - N_EXPERTS), constant_values=NEG_BIG)
        x2, idx, gate, cnt = _cross_call(x, mk, mv, p["xq_w"][l].astype(BF16),
                                         p["xo_w"][l].astype(BF16), p["ln2_g"][l], p["ln2_b"][l],
                                         rw, rb.reshape(1, -1), seq)
        x = _moe(x2, idx[:, :TOP_K], gate[:, :TOP_K], cnt[:, 0, :N_EXPERTS],
                 p["w1"][l].astype(BF16), p["b1"][l][:, None, :], p["w2"][l].astype(BF16),
                 p["b2"][l][:, None, :], p["ln3_g"][l], p["ln3_b"][l],
                 split_rows=ta if l == DEPTH - 1 else None)
    ya, yb = x
    return ya.reshape(xa.shape), yb.reshape(xb.shape)


def kernel(x_prompt, x_sample, mem_prompt, mem_sample, ln_in_g, ln_in_b, w_in, b_gate, q_norm_g, k_norm_g, conv_w, lru_conv_w, lru_conv_b, lru_wa, lru_ba, lru_wx, lru_bx, lru_lam, w_mix_out, ln1_g, ln1_b, xq_w, xkv_w, xo_w, ln2_g, ln2_b, router_w, router_b, w1, b1, w2, b2, ln3_g, ln3_b):
    params = dict(ln_in_g=ln_in_g, ln_in_b=ln_in_b, w_in=w_in, b_gate=b_gate, q_norm_g=q_norm_g,
                  k_norm_g=k_norm_g, conv_w=conv_w, lru_conv_w=lru_conv_w, lru_conv_b=lru_conv_b,
                  lru_wa=lru_wa, lru_ba=lru_ba, lru_wx=lru_wx, lru_bx=lru_bx, lru_lam=lru_lam,
                  w_mix_out=w_mix_out, ln1_g=ln1_g, ln1_b=ln1_b, xq_w=xq_w, xkv_w=xkv_w, xo_w=xo_w,
                  ln2_g=ln2_g, ln2_b=ln2_b, router_w=router_w, router_b=router_b, w1=w1, b1=b1,
                  w2=w2, b2=b2, ln3_g=ln3_g, ln3_b=ln3_b)
    assert x_prompt.shape[1:] == x_sample.shape[1:]
    return _encode(x_prompt, x_sample, jnp.concatenate([mem_prompt, mem_sample], axis=0), params)
```

```python
import functools

import jax
import jax.numpy as jnp
from jax import lax
from jax.experimental import pallas as pl
from jax.experimental.pallas import tpu as pltpu

F32 = jnp.float32
BF16 = jnp.bfloat16
I32 = jnp.int32
U32 = jnp.uint32

D_MODEL = 1024
DEPTH = 2
N_HEADS = 8
N_KV_HEADS = 2
HEAD_DIM = 128
Q_GROUPS = N_HEADS // N_KV_HEADS
D_ATTN_Q = N_HEADS * HEAD_DIM
D_ATTN_KV = N_KV_HEADS * HEAD_DIM
GRID_W = 64
ROPE_AXIS_HALF = HEAD_DIM // 4
ROPE_THETA = 10000.0
QK_EPS = 1e-6
LRU_BLOCKS = 8
LRU_BLOCK_W = D_MODEL // LRU_BLOCKS
LRU_C = 8.0
MEM_HEADS = 4
MEM_HEAD_DIM = D_MODEL // MEM_HEADS
N_EXPERTS = 32
TOP_K = 4
D_FF = D_MODEL // 2
SWIGLU_LIMIT = 7.0
SWIGLU_ALPHA = 1.702
LN_EPS = 1e-5
DEEPNORM_ALPHA = (2 * DEPTH) ** 0.25

LANES = 128
SUBLANES = 8
BF16_SUBLANES = 16
VMEM_LIMIT = 56 * 1024 * 1024

PROJ_TM = 512
ATTN_TQ = 256
ATTN_TK = 512
ATTN_PAIRS_PER_BODY = 8
TOK_TILE = 256
CROSS_TM = 512
LRU_CHUNK = 512
LRU_SUBSEQ = 16
LRU_UNROLL = 8
MOE_GROUP = SUBLANES
MOE_BLOCK = 512
MOE_HALF = 256
MOE_TILE_ROWS = TOK_TILE * TOP_K + N_EXPERTS * MOE_GROUP
HEAD_AUG = 2 * HEAD_DIM
ATTN_SHIFT_LIMIT = 60.0
ATTN_BOUND_MARGIN = 1.02
NEG_BIG = -1e30
LOG2E = 1.4426950408889634


def _cparams(sem, vmem=VMEM_LIMIT):
    return pltpu.CompilerParams(dimension_semantics=sem, vmem_limit_bytes=vmem)


def _layer_norm(x, g, b):
    mu = jnp.mean(x, axis=-1, keepdims=True)
    xc = x - mu
    var = jnp.mean(xc * xc, axis=-1, keepdims=True)
    return xc * lax.rsqrt(var + LN_EPS) * g + b


def _sigmoid(x):
    return 1.0 / (1.0 + jnp.exp(-x))


def _sigmoid_tanh(x):
    return 0.5 * jnp.tanh(0.5 * x) + 0.5


def _dot(a, b):
    return jnp.dot(a, b, preferred_element_type=F32)


def _dot_nt(a, b):
    return lax.dot_general(a, b, (((1,), (1,)), ((), ())), preferred_element_type=F32)


def _ln_kernel(xa_ref, xb_ref, g_ref, b_ref, o_ref, *, tiles_a):
    @pl.when(pl.program_id(0) < tiles_a)
    def _():
        o_ref[...] = _layer_norm(xa_ref[...], g_ref[...], b_ref[...])

    @pl.when(pl.program_id(0) >= tiles_a)
    def _():
        o_ref[...] = _layer_norm(xb_ref[...], g_ref[...], b_ref[...])


def _ln_call(xa, xb, g, b):
    tm = PROJ_TM
    ta, tb = xa.shape[0] // tm, xb.shape[0] // tm
    return pl.pallas_call(
        functools.partial(_ln_kernel, tiles_a=ta),
        grid=(ta + tb,),
        in_specs=[pl.BlockSpec((tm, D_MODEL), lambda i: (jnp.minimum(i, ta - 1), 0)),
                  pl.BlockSpec((tm, D_MODEL), lambda i: (jnp.maximum(i - ta, 0), 0)),
                  pl.BlockSpec((1, D_MODEL), lambda i: (0, 0)),
                  pl.BlockSpec((1, D_MODEL), lambda i: (0, 0))],
        out_specs=pl.BlockSpec((tm, D_MODEL), lambda i: (i, 0)),
        out_shape=jax.ShapeDtypeStruct(((ta + tb) * tm, D_MODEL), F32),
        compiler_params=_cparams(("arbitrary",)),
        name="ln_in",
    )(xa, xb, g.reshape(1, -1), b.reshape(1, -1))


def _qkv_kernel(x_ref, w_ref, cos_ref, sin_ref, qg_ref, kg_ref, shift_ref, q_ref, k_ref, vt_ref):
    x = x_ref[...].astype(BF16)
    z = _dot(x, w_ref[...])
    c = cos_ref[...]
    s = sin_ref[...]
    lane = lax.broadcasted_iota(I32, c.shape, 1)
    first_half = (lane % (2 * ROPE_AXIS_HALF)) < ROPE_AXIS_HALF

    def prep(zh, g, scale):
        ms = jnp.mean(zh * zh, axis=-1, keepdims=True)
        y = zh * lax.rsqrt(ms + QK_EPS) * g
        partner = jnp.where(first_half,
                            pltpu.roll(y, HEAD_DIM - ROPE_AXIS_HALF, 1),
                            pltpu.roll(y, ROPE_AXIS_HALF, 1))
        return (y * c + partner * s) * scale

    qg = qg_ref[...]
    kg = kg_ref[...]
    q_extra = jnp.broadcast_to(shift_ref[...], c.shape).astype(q_ref.dtype)
    k_extra = jnp.where(lane == 0, 1.0, 0.0).astype(k_ref.dtype)
    for h in range(N_HEADS):
        sl = slice(h * HEAD_DIM, (h + 1) * HEAD_DIM)
        q_ref[:, h * HEAD_AUG:h * HEAD_AUG + HEAD_DIM] = prep(
            z[:, sl], qg, LOG2E * HEAD_DIM ** -0.5).astype(q_ref.dtype)
        q_ref[:, h * HEAD_AUG + HEAD_DIM:(h + 1) * HEAD_AUG] = q_extra
    for h in range(N_KV_HEADS):
        zsl = slice(D_ATTN_Q + h * HEAD_DIM, D_ATTN_Q + (h + 1) * HEAD_DIM)
        k_ref[:, h * HEAD_AUG:h * HEAD_AUG + HEAD_DIM] = prep(z[:, zsl], kg, 1.0).astype(k_ref.dtype)
        k_ref[:, h * HEAD_AUG + HEAD_DIM:(h + 1) * HEAD_AUG] = k_extra
    vt_ref[0] = z[:, D_ATTN_Q + D_ATTN_KV:].T.astype(vt_ref.dtype)


def _qkv_call(x, w, cos_t, sin_t, qg, kg, shift, seq):
    t = x.shape[0]
    tm = PROJ_TM
    nseq = seq // tm
    n = w.shape[1]
    shift_row = jnp.zeros((1, HEAD_DIM), F32).at[0, 0].set(-shift)
    vec_spec = pl.BlockSpec((1, HEAD_DIM), lambda i: (0, 0))
    return pl.pallas_call(
        _qkv_kernel,
        grid=(t // tm,),
        in_specs=[pl.BlockSpec((tm, D_MODEL), lambda i: (i, 0)),
                  pl.BlockSpec((D_MODEL, n), lambda i: (0, 0)),
                  pl.BlockSpec((tm, HEAD_DIM), lambda i: (i % nseq, 0)),
                  pl.BlockSpec((tm, HEAD_DIM), lambda i: (i % nseq, 0)),
                  vec_spec, vec_spec, vec_spec],
        out_specs=[pl.BlockSpec((tm, N_HEADS * HEAD_AUG), lambda i: (i, 0)),
                   pl.BlockSpec((tm, N_KV_HEADS * HEAD_AUG), lambda i: (i, 0)),
                   pl.BlockSpec((1, D_ATTN_KV, tm), lambda i: (i // nseq, 0, i % nseq))],
        out_shape=[jax.ShapeDtypeStruct((t, N_HEADS * HEAD_AUG), BF16),
                   jax.ShapeDtypeStruct((t, N_KV_HEADS * HEAD_AUG), BF16),
                   jax.ShapeDtypeStruct((t // seq, D_ATTN_KV, seq), BF16)],
        compiler_params=_cparams(("parallel",)),
        name="qkv_proj",
    )(x, w, cos_t, sin_t, qg.reshape(1, -1), kg.reshape(1, -1), shift_row)


def _convproj_kernel(x_ref, w_ref, cb_ref, p_ref):
    x = x_ref[...].astype(BF16)
    z = _dot(x, w_ref[...])
    cb_ref[...] = z[:, :D_MODEL].astype(cb_ref.dtype)
    p_ref[...] = (z[:, D_MODEL:2 * D_MODEL] * z[:, 2 * D_MODEL:]).astype(p_ref.dtype)


def _convproj_call(x, w):
    t = x.shape[0]
    tm = PROJ_TM
    return pl.pallas_call(
        _convproj_kernel,
        grid=(t // tm,),
        in_specs=[pl.BlockSpec((tm, D_MODEL), lambda i: (i, 0)),
                  pl.BlockSpec((D_MODEL, 3 * D_MODEL), lambda i: (0, 0))],
        out_specs=[pl.BlockSpec((tm, D_MODEL), lambda i: (i, 0)),
                   pl.BlockSpec((tm, D_MODEL), lambda i: (i, 0))],
        out_shape=[jax.ShapeDtypeStruct((t, D_MODEL), BF16),
                   jax.ShapeDtypeStruct((t, D_MODEL), BF16)],
        compiler_params=_cparams(("parallel",)),
        name="conv_proj",
    )(x, w)


def _rnnproj_kernel(x_ref, w_ref, gy_ref, rx_ref):
    x = x_ref[...].astype(BF16)
    z = _dot(x, w_ref[...])
    gy_ref[...] = jax.nn.gelu(z[:, :D_MODEL]).astype(gy_ref.dtype)
    rx_ref[...] = z[:, D_MODEL:]


def _rnnproj_call(x, w):
    t = x.shape[0]
    tm = PROJ_TM
    return pl.pallas_call(
        _rnnproj_kernel,
        grid=(t // tm,),
        in_specs=[pl.BlockSpec((tm, D_MODEL), lambda i: (i, 0)),
                  pl.BlockSpec((D_MODEL, 2 * D_MODEL), lambda i: (0, 0))],
        out_specs=[pl.BlockSpec((tm, D_MODEL), lambda i: (i, 0)),
                   pl.BlockSpec((tm, D_MODEL), lambda i: (i, 0))],
        out_shape=[jax.ShapeDtypeStruct((t, D_MODEL), BF16),
                   jax.ShapeDtypeStruct((t, D_MODEL), F32)],
        compiler_params=_cparams(("parallel",)),
        name="rnn_proj",
    )(x, w)


def _gateproj_kernel(x_ref, w_ref, b_ref, g_ref):
    x = x_ref[...].astype(BF16)
    z = _dot(x, w_ref[...]) + b_ref[...]
    g_ref[...] = _sigmoid(z).astype(g_ref.dtype)


def _gateproj_call(x, w, b):
    t = x.shape[0]
    tm = PROJ_TM
    return pl.pallas_call(
        _gateproj_kernel,
        grid=(t // tm,),
        in_specs=[pl.BlockSpec((tm, D_MODEL), lambda i: (i, 0)),
                  pl.BlockSpec((D_MODEL, 3 * D_MODEL), lambda i: (0, 0)),
                  pl.BlockSpec((1, 3 * D_MODEL), lambda i: (0, 0))],
        out_specs=pl.BlockSpec((tm, 3 * D_MODEL), lambda i: (i, 0)),
        out_shape=jax.ShapeDtypeStruct((t, 3 * D_MODEL), BF16),
        compiler_params=_cparams(("parallel",)),
        name="gate_proj",
    )(x, w, b.reshape(1, -1))


def _attn_kernel(q_ref, k_ref, vt_ref, o_ref, acc_s, s0_s, s1_s, p0_s, p1_s, *, tk):
    tq = q_ref.shape[1]
    n = Q_GROUPS * tq
    nk = k_ref.shape[1] // tk
    qs = _stack_heads(q_ref)
    acc_s[...] = jnp.zeros(acc_s.shape, F32)
    s_bufs = (s0_s, s1_s)
    p_bufs = (p0_s, p1_s)

    def scores(j, slot):
        start = pl.multiple_of(j * tk, tk)
        s_bufs[slot][...] = _dot_nt(k_ref[0, pl.ds(start, tk), :], qs)

    def softmax(slot, m, l):
        st = s_bufs[slot][...]
        m_new = jnp.maximum(m, jnp.max(st, axis=0, keepdims=True))
        alpha = jnp.exp2(m - m_new)
        p = jnp.exp2(st - m_new)
        p_bufs[slot][...] = p.astype(BF16)
        return m_new, alpha * l + jnp.sum(p, axis=0, keepdims=True), alpha

    def accumulate(j, slot, alpha):
        start = pl.multiple_of(j * tk, tk)
        acc_s[...] = alpha * acc_s[...] + _dot(vt_ref[0, :, pl.ds(start, tk)], p_bufs[slot][...])

    def stage(j, m, l, alphas, first, last):
        new_alphas = []
        for slot in range(2):
            if not first:
                accumulate(j - 2 + slot, slot, alphas[slot])
            m, l, a = softmax(slot, m, l)
            new_alphas.append(a)
            if not last:
                scores(j + 2 + slot, slot)
        return m, l, tuple(new_alphas)

    m = jnp.full((1, n), NEG_BIG, F32)
    l = jnp.zeros((1, n), F32)
    scores(0, 0)
    scores(1, 1)
    if nk == 2:
        m, l, alphas = stage(0, m, l, None, True, True)
    else:
        m, l, alphas = stage(0, m, l, None, True, False)
        m, l, alphas = lax.fori_loop(
            1, nk // 2 - 1, lambda i, c: stage(2 * i, c[0], c[1], c[2], False, False), (m, l, alphas))
        m, l, alphas = stage(nk - 2, m, l, alphas, False, True)
    accumulate(nk - 2, 0, alphas[0])
    accumulate(nk - 1, 1, alphas[1])
    _write_heads(o_ref, acc_s[...] / l)


def _stack_heads(q_ref):
    return jnp.concatenate([q_ref[0, :, g * HEAD_AUG:(g + 1) * HEAD_AUG] for g in range(Q_GROUPS)],
                           axis=0)


def _write_heads(o_ref, ot):
    tq = o_ref.shape[1]
    for g in range(Q_GROUPS):
        o_ref[0, :, g * HEAD_DIM:(g + 1) * HEAD_DIM] = ot[:, g * tq:(g + 1) * tq].T.astype(o_ref.dtype)


def _attn_shifted_kernel(q_ref, k_ref, vt_ref, o_ref, acc_s, p0_s, p1_s, *, tk):
    n = Q_GROUPS * q_ref.shape[1]
    nk = k_ref.shape[1] // tk
    qs = _stack_heads(q_ref)
    acc_s[...] = jnp.zeros(acc_s.shape, F32)
    p_bufs = (p0_s, p1_s)

    def probs(j, slot, l):
        start = pl.multiple_of(j * tk, tk)
        p = jnp.exp2(_dot_nt(k_ref[0, pl.ds(start, tk), :], qs))
        p_bufs[slot][...] = p.astype(BF16)
        return l + jnp.sum(p, axis=0, keepdims=True)

    def accumulate(j, slot):
        start = pl.multiple_of(j * tk, tk)
        acc_s[...] += _dot(vt_ref[0, :, pl.ds(start, tk)], p_bufs[slot][...])

    def pair(i, l):
        j = 2 * i
        l = probs(j + 1, 1, l)
        accumulate(j, 0)
        l = probs(j + 2, 0, l)
        accumulate(j + 1, 1)
        return l

    l = probs(0, 0, jnp.zeros((1, n), F32))
    n_pairs = nk // 2 - 1
    trips = n_pairs // ATTN_PAIRS_PER_BODY

    def body(i, l):
        for r in range(ATTN_PAIRS_PER_BODY):
            l = pair(i * ATTN_PAIRS_PER_BODY + r, l)
        return l

    if trips > 1:
        l = lax.fori_loop(0, trips, body, l)
    else:
        trips = 0
    for i in range(trips * ATTN_PAIRS_PER_BODY, n_pairs):
        l = pair(i, l)
    l = probs(nk - 1, 1, l)
    accumulate(nk - 2, 0)
    accumulate(nk - 1, 1)
    _write_heads(o_ref, acc_s[...] / l)


def _attn_call(q, k, vt, shifted):
    b, s, _ = q.shape
    tq = min(ATTN_TQ, s)
    tk = min(ATTN_TK, s // 2)
    assert (s // tk) % 2 == 0
    n = Q_GROUPS * tq
    p_bufs = [pltpu.VMEM((tk, n), BF16)] * 2
    s_bufs = [] if shifted else [pltpu.VMEM((tk, n), F32)] * 2
    return pl.pallas_call(
        functools.partial(_attn_shifted_kernel if shifted else _attn_kernel, tk=tk),
        grid=(b, N_KV_HEADS, s // tq),
        in_specs=[pl.BlockSpec((1, tq, Q_GROUPS * HEAD_AUG), lambda bi, h, i: (bi, i, h)),
                  pl.BlockSpec((1, s, HEAD_AUG), lambda bi, h, i: (bi, 0, h)),
                  pl.BlockSpec((1, HEAD_DIM, s), lambda bi, h, i: (bi, h, 0))],
        out_specs=pl.BlockSpec((1, tq, Q_GROUPS * HEAD_DIM), lambda bi, h, i: (bi, i, h)),
        out_shape=jax.ShapeDtypeStruct((b, s, D_ATTN_Q), BF16),
        scratch_shapes=[pltpu.VMEM((HEAD_DIM, n), F32)] + s_bufs + p_bufs,
        compiler_params=_cparams(("parallel", "parallel", "parallel")),
        name="flash_attn_shifted" if shifted else "flash_attn",
    )(q, k, vt)


def _lru_kernel(x_ref, cw_ref, cb_ref, wa_ref, wx_ref, ba_ref, bx_ref, lam_ref, o_hbm,
                xp_s, a0_s, u0_s, a1_s, u1_s, h_s, sem, *, seq, chunk):
    sub = seq // LRU_SUBSEQ
    cps = sub // chunk
    nchunks = seq // chunk

    zeros8 = jnp.zeros((SUBLANES, LANES), F32)
    xp_s[0:SUBLANES, :] = zeros8
    xp_s[seq + SUBLANES:seq + 2 * SUBLANES, :] = zeros8
    xp_s[SUBLANES:seq + SUBLANES, :] = x_ref[0]
    cw = cw_ref[...]
    cb = cb_ref[...]
    a_bufs = (a0_s, a1_s)
    u_bufs = (u0_s, u1_s)

    neg = -lam_ref[...]
    softplus = jnp.maximum(neg, 0.0) + jnp.log1p(jnp.exp(-jnp.abs(neg)))
    decay = -LRU_C * softplus

    def gate_chunk(c, _):
        base = pl.multiple_of(c * chunk, chunk)
        xc = cb
        for k in range(4):
            xc = xc + cw[k:k + 1, :] * xp_s[pl.ds(base + SUBLANES - 1 + k, chunk), :]
        xb = xc.astype(BF16)
        dst = pl.ds((c % cps) * (chunk * LRU_SUBSEQ) + c // cps, chunk, stride=LRU_SUBSEQ)
        for d in range(2):
            r = _sigmoid_tanh(_dot(xb, wa_ref[d, 0]) + ba_ref[d:d + 1, :])
            i = _sigmoid_tanh(_dot(xb, wx_ref[d, 0]) + bx_ref[d:d + 1, :])
            a = jnp.exp(decay[d:d + 1, :] * r)
            u_bufs[d][dst, :] = jnp.sqrt(1.0 - a * a) * (i * xc)
            a_bufs[d][dst, :] = a
        return 0

    lax.fori_loop(0, nchunks, gate_chunk, 0)

    def step(tt, carry):
        out = []
        for d in range(2):
            h, p = carry[d]
            t = tt if d == 0 else sub - 1 - tt
            idx = pl.ds(pl.multiple_of(t * LRU_SUBSEQ, LRU_SUBSEQ), LRU_SUBSEQ)
            av = a_bufs[d][idx, :]
            h = av * h + u_bufs[d][idx, :]
            p = av * p
            u_bufs[d][idx, :] = h
            a_bufs[d][idx, :] = p
            out.append((h, p))
        return tuple(out)

    h0 = jnp.zeros((LRU_SUBSEQ, LANES), F32)
    p0 = jnp.ones((LRU_SUBSEQ, LANES), F32)
    ends = lax.fori_loop(0, sub, step, ((h0, p0), (h0, p0)), unroll=LRU_UNROLL)

    carry_in = []
    for d in range(2):
        h_end, p_end = ends[d]
        order = range(LRU_SUBSEQ) if d == 0 else range(LRU_SUBSEQ - 1, -1, -1)
        states = [None] * LRU_SUBSEQ
        state = jnp.zeros((1, LANES), F32)
        for j in order:
            states[j] = state
            state = p_end[j:j + 1, :] * state + h_end[j:j + 1, :]
        carry_in.append(jnp.concatenate(states, axis=0))

    step_id = pl.program_id(0) * pl.num_programs(1) + pl.program_id(1)
    last_id = pl.num_programs(0) * pl.num_programs(1) - 1

    def out_copies(bi, ci):
        return [pltpu.make_async_copy(
            h_s.at[:, j, :],
            o_hbm.at[bi, pl.ds(j * sub, sub), pl.ds(pl.multiple_of(ci * LANES, LANES), LANES)],
            sem) for j in range(LRU_SUBSEQ)]

    @pl.when(step_id > 0)
    def _():
        for cp in out_copies(0, 0):
            cp.wait()

    steps = min(sub, chunk)
    for c in range(sub // steps):
        rows = slice(c * steps * LRU_SUBSEQ, (c + 1) * steps * LRU_SUBSEQ)
        c0 = jnp.tile(carry_in[0], (steps, 1))
        c1 = jnp.tile(carry_in[1], (steps, 1))
        val = u0_s[rows, :] + a0_s[rows, :] * c0 + u1_s[rows, :] + a1_s[rows, :] * c1
        h_s[c * steps:(c + 1) * steps] = val.reshape(steps, LRU_SUBSEQ, LANES)

    for cp in out_copies(pl.program_id(0), pl.program_id(1)):
        cp.start()

    @pl.when(step_id == last_id)
    def _():
        for cp in out_copies(0, 0):
            cp.wait()


def _lru_call(rx, cw, cb, wa, wx, ba, bx, lam):
    b, s, _ = rx.shape
    chunk = min(LRU_CHUNK, s // LRU_SUBSEQ)
    assert LRU_SUBSEQ % SUBLANES == 0
    w = LRU_BLOCK_W
    return pl.pallas_call(
        functools.partial(_lru_kernel, seq=s, chunk=chunk),
        grid=(b, LRU_BLOCKS),
        in_specs=[pl.BlockSpec((1, s, w), lambda bi, c: (bi, 0, c)),
                  pl.BlockSpec((4, w), lambda bi, c: (0, c)),
                  pl.BlockSpec((1, w), lambda bi, c: (0, c)),
                  pl.BlockSpec((2, 1, w, w), lambda bi, c: (0, c, 0, 0)),
                  pl.BlockSpec((2, 1, w, w), lambda bi, c: (0, c, 0, 0)),
                  pl.BlockSpec((2, w), lambda bi, c: (0, c)),
                  pl.BlockSpec((2, w), lambda bi, c: (0, c)),
                  pl.BlockSpec((2, w), lambda bi, c: (0, c))],
        out_specs=pl.BlockSpec(memory_space=pl.ANY),
        out_shape=jax.ShapeDtypeStruct((b, s, D_MODEL), F32),
        scratch_shapes=[pltpu.VMEM((s + 2 * SUBLANES, w), F32)]
        + [pltpu.VMEM((s, w), F32)] * 4
        + [pltpu.VMEM((s // LRU_SUBSEQ, LRU_SUBSEQ, w), F32), pltpu.SemaphoreType.DMA(())],
        compiler_params=_cparams(("arbitrary", "arbitrary")),
        name="rg_lru",
    )(rx, cw, cb.reshape(1, -1), wa, wx, ba, bx, lam)


def _merge_kernel(x_ref, cb_ref, p_ref, pprev_ref, pnext_ref, ya_ref, h_ref, gy_ref, g_ref,
                  cw_ref, w_ref, lg_ref, lb_ref, o_ref, *, seq):
    tm = x_ref.shape[0]
    t0 = pl.program_id(0) * tm
    p = p_ref[...].astype(F32)
    row = lax.broadcasted_iota(I32, p.shape, 0)
    has_prev = ((t0 % seq) != 0).astype(F32)
    has_next = (((t0 + tm) % seq) != 0).astype(F32)
    prev_row = pprev_ref[BF16_SUBLANES - 1:BF16_SUBLANES, :].astype(F32) * has_prev
    next_row = pnext_ref[0:1, :].astype(F32) * has_next
    p_dn = jnp.where(row == 0, prev_row, pltpu.roll(p, 1, 0))
    p_up = jnp.where(row == tm - 1, next_row, pltpu.roll(p, tm - 1, 0))
    cw = cw_ref[...]
    y_conv = cb_ref[...].astype(F32) * (cw[0:1, :] * p_dn + cw[1:2, :] * p + cw[2:3, :] * p_up)
    y_rnn = h_ref[...] * gy_ref[...].astype(F32)
    merged = (g_ref[:, :D_MODEL].astype(F32) * y_conv
              + g_ref[:, D_MODEL:2 * D_MODEL].astype(F32) * ya_ref[...].astype(F32)
              + g_ref[:, 2 * D_MODEL:].astype(F32) * y_rnn)
    mix = _dot(merged.astype(BF16), w_ref[...])
    o_ref[...] = _layer_norm(DEEPNORM_ALPHA * x_ref[...] + mix, lg_ref[...], lb_ref[...])


def _merge_call(x, cb, p, ya, h, gy, g, cw, w, lg, lb, seq):
    t = x.shape[0]
    tm = TOK_TILE
    hb = BF16_SUBLANES
    per = tm // hb
    last = t // hb - 1
    row_spec = pl.BlockSpec((tm, D_MODEL), lambda i: (i, 0))
    vec_spec = pl.BlockSpec((1, D_MODEL), lambda i: (0, 0))
    return pl.pallas_call(
        functools.partial(_merge_kernel, seq=seq),
        grid=(t // tm,),
        in_specs=[row_spec, row_spec, row_spec,
                  pl.BlockSpec((hb, D_MODEL), lambda i: (jnp.maximum(i * per - 1, 0), 0)),
                  pl.BlockSpec((hb, D_MODEL), lambda i: (jnp.minimum((i + 1) * per, last), 0)),
                  row_spec, row_spec, row_spec,
                  pl.BlockSpec((tm, 3 * D_MODEL), lambda i: (i, 0)),
                  pl.BlockSpec((3, D_MODEL), lambda i: (0, 0)),
                  pl.BlockSpec((D_MODEL, D_MODEL), lambda i: (0, 0)),
                  vec_spec, vec_spec],
        out_specs=row_spec,
        out_shape=jax.ShapeDtypeStruct((t, D_MODEL), F32),
        compiler_params=_cparams(("parallel",)),
        name="merge_outproj_ln1",
    )(x, cb, p, p, p, ya, h, gy, g, cw, w, lg.reshape(1, -1), lb.reshape(1, -1))


def _kv_kernel(m_ref, w_ref, k_ref, v_ref):
    z = _dot(m_ref[0].astype(BF16), w_ref[...])
    k_ref[0] = z[:, :D_MODEL].astype(k_ref.dtype)
    v_ref[0] = z[:, D_MODEL:].astype(v_ref.dtype)


def _kv_call(mem, w):
    b, m, _ = mem.shape
    return pl.pallas_call(
        _kv_kernel,
        grid=(b,),
        in_specs=[pl.BlockSpec((1, m, D_MODEL), lambda i: (i, 0, 0)),
                  pl.BlockSpec((D_MODEL, 2 * D_MODEL), lambda i: (0, 0))],
        out_specs=[pl.BlockSpec((1, m, D_MODEL), lambda i: (i, 0, 0)),
                   pl.BlockSpec((1, m, D_MODEL), lambda i: (i, 0, 0))],
        out_shape=[jax.ShapeDtypeStruct((b, m, D_MODEL), BF16),
                   jax.ShapeDtypeStruct((b, m, D_MODEL), BF16)],
        compiler_params=_cparams(("parallel",)),
        name="mem_kv_proj",
    )(mem, w)


def _cross_kernel(x_ref, k_ref, v_ref, wq_ref, wo_ref, lg_ref, lb_ref, rw_ref, rb_ref,
                  x2_ref, idx_ref, gate_ref, cnt_ref):
    for r in range(x_ref.shape[0] // TOK_TILE):
        rows = slice(r * TOK_TILE, (r + 1) * TOK_TILE)
        _cross_rows(x_ref[rows, :], k_ref, v_ref, wq_ref, wo_ref, lg_ref, lb_ref, rw_ref, rb_ref,
                    x2_ref.at[rows, :], idx_ref.at[rows, :], gate_ref.at[rows, :], cnt_ref.at[r])


def _cross_rows(x, k_ref, v_ref, wq_ref, wo_ref, lg_ref, lb_ref, rw_ref, rb_ref,
                x2_ref, idx_ref, gate_ref, cnt_ref):
    q = _dot(x.astype(BF16), wq_ref[...]) * (MEM_HEAD_DIM ** -0.5)
    outs = []
    for h in range(MEM_HEADS):
        sl = slice(h * MEM_HEAD_DIM, (h + 1) * MEM_HEAD_DIM)
        s = _dot_nt(q[:, sl].astype(BF16), k_ref[0, :, sl])
        m = jnp.max(s, axis=-1, keepdims=True)
        p = jnp.exp(s - m)
        p = p / jnp.sum(p, axis=-1, keepdims=True)
        outs.append(_dot(p.astype(BF16), v_ref[0, :, sl]))
    o = jnp.concatenate(outs, axis=1)
    cross = _dot(o.astype(BF16), wo_ref[...])
    x2 = _layer_norm(DEEPNORM_ALPHA * x + cross, lg_ref[...], lb_ref[...])
    x2_ref[...] = x2

    logits = _dot(x2.astype(BF16), rw_ref[...]) + rb_ref[...]
    lane = lax.broadcasted_iota(I32, logits.shape, 1)
    idx_acc = jnp.zeros(logits.shape, I32)
    val_acc = jnp.zeros(logits.shape, F32)
    member = jnp.zeros(logits.shape, F32)
    top0 = None
    for k in range(TOP_K):
        m = jnp.max(logits, axis=-1, keepdims=True)
        pick = jnp.min(jnp.where(logits == m, lane, LANES), axis=-1, keepdims=True)
        hit = lane == pick
        if k == 0:
            top0 = m
        idx_acc = jnp.where(lane == k, pick, idx_acc)
        val_acc = jnp.where(lane == k, jnp.exp(m - top0), val_acc)
        member = member + hit.astype(F32)
        logits = jnp.where(hit, NEG_BIG, logits)
    idx_ref[...] = idx_acc
    gate_ref[...] = val_acc / jnp.sum(val_acc, axis=-1, keepdims=True)
    cnt_ref[...] = jnp.sum(member, axis=0, keepdims=True).astype(I32)


def _cross_call(x, k, v, wq, wo, lg, lb, rw, rb, seq):
    t = x.shape[0]
    tm = CROSS_TM
    halves = tm // TOK_TILE
    per_seq = seq // tm
    m = k.shape[1]
    row_spec = pl.BlockSpec((tm, D_MODEL), lambda i: (i, 0))
    vec_spec = pl.BlockSpec((1, D_MODEL), lambda i: (0, 0))
    lane_spec = pl.BlockSpec((tm, LANES), lambda i: (i, 0))
    return pl.pallas_call(
        _cross_kernel,
        grid=(t // tm,),
        in_specs=[row_spec,
                  pl.BlockSpec((1, m, D_MODEL), lambda i: (i // per_seq, 0, 0)),
                  pl.BlockSpec((1, m, D_MODEL), lambda i: (i // per_seq, 0, 0)),
                  pl.BlockSpec((D_MODEL, D_MODEL), lambda i: (0, 0)),
                  pl.BlockSpec((D_MODEL, D_MODEL), lambda i: (0, 0)),
                  vec_spec, vec_spec,
                  pl.BlockSpec((D_MODEL, LANES), lambda i: (0, 0)),
                  pl.BlockSpec((1, LANES), lambda i: (0, 0))],
        out_specs=[row_spec, lane_spec, lane_spec,
                   pl.BlockSpec((halves, 1, LANES), lambda i: (i, 0, 0))],
        out_shape=[jax.ShapeDtypeStruct((t, D_MODEL), F32),
                   jax.ShapeDtypeStruct((t, LANES), I32),
                   jax.ShapeDtypeStruct((t, LANES), F32),
                   jax.ShapeDtypeStruct((t // TOK_TILE, 1, LANES), I32)],
        compiler_params=_cparams(("parallel",)),
        name="cross_attn_ln2_router",
    )(x, k, v, wq, wo, lg.reshape(1, -1), lb.reshape(1, -1), rw, rb)


def _pack_rows(v):
    bits = pltpu.bitcast(v, U32)
    half = v.shape[1] // 2
    return (bits[:, :half] & jnp.uint32(0xFFFF0000)) | (bits[:, half:] >> 16)


def _unpack_rows(u):
    hi = pltpu.bitcast(u & jnp.uint32(0xFFFF0000), F32).astype(BF16)
    lo = pltpu.bitcast(u << 16, F32).astype(BF16)
    return hi, lo


def _group_copy(src, dst, sem, s_grp, d_grp):
    return pltpu.make_async_copy(
        src.at[pl.ds(pl.multiple_of(s_grp * MOE_GROUP, MOE_GROUP), MOE_GROUP), :],
        dst.at[pl.ds(pl.multiple_of(d_grp * MOE_GROUP, MOE_GROUP), MOE_GROUP), :], sem)


def _moe_sort_kernel(off8_ref, len8_ref, dst8_ref, x_ref, idx_ref, offv_ref, dest_ref, xs_hbm,
                     xs_s, sem, pending_s):
    s = pl.program_id(0)
    tm = x_ref.shape[0]
    idx = idx_ref[...]
    e_iota = lax.broadcasted_iota(I32, (N_EXPERTS, tm), 0)
    member = jnp.zeros((N_EXPERTS, tm), F32)
    for k in range(TOP_K):
        member = member + (e_iota == idx[k:k + 1, :]).astype(F32)
    earlier = (lax.broadcasted_iota(I32, (tm, tm), 0)
               < lax.broadcasted_iota(I32, (tm, tm), 1)).astype(BF16)
    rank = _dot(member.astype(BF16), earlier)
    pos = rank + offv_ref[0][:, 0:1]
    r_iota = lax.broadcasted_iota(I32, (MOE_TILE_ROWS, tm), 0)
    onehot = jnp.zeros((MOE_TILE_ROWS, tm), F32)
    dests = []
    for k in range(TOP_K):
        dk = jnp.sum(jnp.where(e_iota == idx[k:k + 1, :], pos, 0.0), axis=0, keepdims=True)
        dk = dk.astype(I32)
        dests.append(dk)
        onehot = onehot + (r_iota == dk).astype(F32)
    dest_ref[...] = jnp.concatenate(dests, axis=0)
    slot = s % 2
    xs_s[slot] = _pack_rows(_dot(onehot.astype(BF16), x_ref[...].astype(BF16)))

    def per_expert(e, total):
        n = len8_ref[s * N_EXPERTS + e]
        src0 = off8_ref[s * N_EXPERTS + e]
        dst0 = dst8_ref[s * N_EXPERTS + e]

        def issue(i, _):
            _group_copy(xs_s.at[slot], xs_hbm, sem.at[slot], src0 + i, dst0 + i).start()
            return 0

        lax.fori_loop(0, n, issue, 0)
        return total + n

    total = lax.fori_loop(0, N_EXPERTS, per_expert, 0)

    def drain(which, count):
        def body(i, _):
            _group_copy(xs_s.at[which], xs_hbm, sem.at[which], 0, 0).wait()
            return 0

        lax.fori_loop(0, count, body, 0)

    @pl.when(s > 0)
    def _():
        drain(1 - slot, pending_s[0])

    pending_s[0] = total

    @pl.when(s == pl.num_programs(0) - 1)
    def _():
        drain(slot, total)


def _moe_sort_call(x2, idx_t, offv, off8, len8, dst8, n_rows):
    t = x2.shape[0]
    tm = TOK_TILE
    grid_spec = pltpu.PrefetchScalarGridSpec(
        num_scalar_prefetch=3,
        grid=(t // tm,),
        in_specs=[pl.BlockSpec((tm, D_MODEL), lambda i, *_: (i, 0)),
                  pl.BlockSpec((TOP_K, tm), lambda i, *_: (0, i)),
                  pl.BlockSpec((1, N_EXPERTS, LANES), lambda i, *_: (i, 0, 0))],
        out_specs=[pl.BlockSpec((TOP_K, tm), lambda i, *_: (0, i)),
                   pl.BlockSpec(memory_space=pl.ANY)],
        scratch_shapes=[pltpu.VMEM((2, MOE_TILE_ROWS, D_MODEL // 2), U32),
                        pltpu.SemaphoreType.DMA((2,)),
                        pltpu.SMEM((1,), I32)],
    )
    return pl.pallas_call(
        _moe_sort_kernel,
        grid_spec=grid_spec,
        out_shape=[jax.ShapeDtypeStruct((TOP_K, t), I32),
                   jax.ShapeDtypeStruct((n_rows, D_MODEL // 2), U32)],
        compiler_params=_cparams(("arbitrary",)),
        name="moe_sort",
    )(off8, len8, dst8, x2, idx_t, offv)


def _moe_ffn_kernel(be_ref, nb_ref, xs_ref, w1_ref, b1_ref, w2_ref, b2_ref, o_ref):
    @pl.when(pl.program_id(0) < nb_ref[0])
    def _():
        half = D_MODEL // 2
        for r in range(MOE_BLOCK // MOE_HALF):
            rows = slice(r * MOE_HALF, (r + 1) * MOE_HALF)
            hi, lo = _unpack_rows(xs_ref[rows, :])
            hcat = _dot(hi, w1_ref[0, :half, :]) + _dot(lo, w1_ref[0, half:, :]) + b1_ref[0]
            glu = jnp.minimum(hcat[:, :D_FF], SWIGLU_LIMIT)
            lin = jnp.clip(hcat[:, D_FF:], -SWIGLU_LIMIT, SWIGLU_LIMIT)
            act = glu * _sigmoid_tanh(SWIGLU_ALPHA * glu) * (lin + 1.0)
            out = _dot(act.astype(BF16), w2_ref[0]) + b2_ref[0]
            o_ref[rows, :] = _pack_rows(out.astype(BF16).astype(F32))


def _moe_ffn_call(xs, block_e, nb_used, w1, b1, w2, b2):
    n_rows = xs.shape[0]
    nb = n_rows // MOE_BLOCK

    def blk(i, be, nbu):
        return (jnp.minimum(i, nbu[0] - 1), 0)

    grid_spec = pltpu.PrefetchScalarGridSpec(
        num_scalar_prefetch=2,
        grid=(nb,),
        in_specs=[pl.BlockSpec((MOE_BLOCK, D_MODEL // 2), blk),
                  pl.BlockSpec((1, D_MODEL, 2 * D_FF), lambda i, be, nbu: (be[i], 0, 0)),
                  pl.BlockSpec((1, 1, 2 * D_FF), lambda i, be, nbu: (be[i], 0, 0)),
                  pl.BlockSpec((1, D_FF, D_MODEL), lambda i, be, nbu: (be[i], 0, 0)),
                  pl.BlockSpec((1, 1, D_MODEL), lambda i, be, nbu: (be[i], 0, 0))],
        out_specs=pl.BlockSpec((MOE_BLOCK, D_MODEL // 2), blk),
    )
    return pl.pallas_call(
        _moe_ffn_kernel,
        grid_spec=grid_spec,
        out_shape=jax.ShapeDtypeStruct((n_rows, D_MODEL // 2), U32),
        compiler_params=_cparams(("arbitrary",)),
        name="moe_ffn",
    )(block_e, nb_used, xs, w1, b1, w2, b2)


def _moe_combine_kernel(off8_ref, len8_ref, dst8_ref, tot8_ref, x_ref, dest_ref, gate_ref,
                        lg_ref, lb_ref, out_hbm, *refs, tiles_a):
    o_refs, (buf, sem) = refs[:-2], refs[-2:]
    s = pl.program_id(0)
    tm = x_ref.shape[0]
    slot = s % 2

    def gather(tile):
        which = tile % 2

        def per_expert(e, _):
            n = len8_ref[tile * N_EXPERTS + e]
            dst0 = off8_ref[tile * N_EXPERTS + e]
            src0 = dst8_ref[tile * N_EXPERTS + e]

            def issue(i, _):
                _group_copy(out_hbm, buf.at[which], sem.at[which], src0 + i, dst0 + i).start()
                return 0

            lax.fori_loop(0, n, issue, 0)
            return 0

        lax.fori_loop(0, N_EXPERTS, per_expert, 0)

    @pl.when(s == 0)
    def _():
        gather(s)

    @pl.when(s + 1 < pl.num_programs(0))
    def _():
        gather(s + 1)

    def clear(g, _):
        buf[slot, pl.ds(pl.multiple_of(g * MOE_GROUP, MOE_GROUP), MOE_GROUP), :] = jnp.zeros(
            (MOE_GROUP, D_MODEL // 2), U32)
        return 0

    lax.fori_loop(tot8_ref[s], MOE_TILE_ROWS // MOE_GROUP, clear, 0)

    dest = dest_ref[...]
    gate = gate_ref[...]
    r_iota = lax.broadcasted_iota(I32, (tm, MOE_TILE_ROWS), 1)
    weights = jnp.zeros((tm, MOE_TILE_ROWS), F32)
    for k in range(TOP_K):
        weights = weights + jnp.where(r_iota == dest[:, k:k + 1], gate[:, k:k + 1], 0.0)

    def drain(i, _):
        _group_copy(out_hbm, buf.at[slot], sem.at[slot], 0, 0).wait()
        return 0

    lax.fori_loop(0, tot8_ref[s], drain, 0)
    hi, lo = _unpack_rows(buf[slot])
    wb = weights.astype(BF16)
    ff = jnp.concatenate([_dot(wb, hi), _dot(wb, lo)], axis=1)
    y = _layer_norm(DEEPNORM_ALPHA * x_ref[...] + ff, lg_ref[...], lb_ref[...])
    if tiles_a is None:
        o_refs[0][...] = y
    else:
        @pl.when(s < tiles_a)
        def _():
            o_refs[0][...] = y

        @pl.when(s >= tiles_a)
        def _():
            o_refs[1][...] = y


def _moe_combine_call(x2, dest, gate, out_rows, lg, lb, off8, len8, dst8, tot8, split_rows):
    t = x2.shape[0]
    tm = TOK_TILE
    if split_rows is None:
        tiles_a = None
        out_specs = pl.BlockSpec((tm, D_MODEL), lambda i, *_: (i, 0))
        out_shape = jax.ShapeDtypeStruct((t, D_MODEL), F32)
    else:
        tiles_a = split_rows // tm
        out_specs = [pl.BlockSpec((tm, D_MODEL), lambda i, *_: (jnp.minimum(i, tiles_a - 1), 0)),
                     pl.BlockSpec((tm, D_MODEL), lambda i, *_: (jnp.maximum(i - tiles_a, 0), 0))]
        out_shape = [jax.ShapeDtypeStruct((split_rows, D_MODEL), F32),
                     jax.ShapeDtypeStruct((t - split_rows, D_MODEL), F32)]
    grid_spec = pltpu.PrefetchScalarGridSpec(
        num_scalar_prefetch=4,
        grid=(t // tm,),
        in_specs=[pl.BlockSpec((tm, D_MODEL), lambda i, *_: (i, 0)),
                  pl.BlockSpec((tm, TOP_K), lambda i, *_: (i, 0)),
                  pl.BlockSpec((tm, TOP_K), lambda i, *_: (i, 0)),
                  pl.BlockSpec((1, D_MODEL), lambda i, *_: (0, 0)),
                  pl.BlockSpec((1, D_MODEL), lambda i, *_: (0, 0)),
                  pl.BlockSpec(memory_space=pl.ANY)],
        out_specs=out_specs,
        scratch_shapes=[pltpu.VMEM((2, MOE_TILE_ROWS, D_MODEL // 2), U32),
                        pltpu.SemaphoreType.DMA((2,))],
    )
    return pl.pallas_call(
        functools.partial(_moe_combine_kernel, tiles_a=tiles_a),
        grid_spec=grid_spec,
        out_shape=out_shape,
        compiler_params=_cparams(("arbitrary",)),
        name="moe_combine_ln3",
    )(off8, len8, dst8, tot8, x2, dest, gate, lg.reshape(1, -1), lb.reshape(1, -1), out_rows)


def _moe_tables(cnt, n_blocks):
    len8 = (cnt + MOE_GROUP - 1) // MOE_GROUP
    off8 = jnp.cumsum(len8, axis=1) - len8
    tot8 = jnp.sum(len8, axis=1)
    per_block = MOE_BLOCK // MOE_GROUP
    blocks_e = (jnp.sum(len8, axis=0) + per_block - 1) // per_block
    ends_e = jnp.cumsum(blocks_e)
    start8_e = (ends_e - blocks_e) * per_block
    dst8 = start8_e[None, :] + jnp.cumsum(len8, axis=0) - len8
    nb_used = ends_e[-1:]
    block_e = jnp.minimum(jnp.sum(ends_e[None, :] <= jnp.arange(n_blocks, dtype=I32)[:, None], axis=1),
                          N_EXPERTS - 1).astype(I32)
    return (off8.reshape(-1).astype(I32), len8.reshape(-1).astype(I32),
            dst8.reshape(-1).astype(I32), tot8.astype(I32), block_e, nb_used.astype(I32))


def _moe(x2, idx, gate, cnt, w1, b1, w2, b2, lg, lb, split_rows=None):
    t = x2.shape[0]
    tiles = t // TOK_TILE
    max_rows = t * TOP_K + tiles * N_EXPERTS * (MOE_GROUP - 1)
    n_blocks = -(-max_rows // MOE_BLOCK) + N_EXPERTS
    off8, len8, dst8, tot8, block_e, nb_used = _moe_tables(cnt, n_blocks)
    offv = jnp.broadcast_to((off8.reshape(tiles, N_EXPERTS, 1) * MOE_GROUP).astype(F32),
                            (tiles, N_EXPERTS, LANES))
    dest_t, xs = _moe_sort_call(x2, idx.T, offv, off8, len8, dst8, n_blocks * MOE_BLOCK)
    out_rows = _moe_ffn_call(xs, block_e, nb_used, w1, b1, w2, b2)
    return _moe_combine_call(x2, dest_t.T, gate, out_rows, lg, lb, off8, len8, dst8, tot8,
                             split_rows)


def _rope_tables(seq):
    rows = seq // GRID_W
    row = jnp.repeat(jnp.arange(rows, dtype=F32), GRID_W)
    col = jnp.tile(jnp.arange(GRID_W, dtype=F32), rows)
    inv_freq = ROPE_THETA ** (-jnp.arange(ROPE_AXIS_HALF, dtype=F32) / ROPE_AXIS_HALF)
    ar = row[:, None] * inv_freq
    ac = col[:, None] * inv_freq
    cos_t = jnp.concatenate([jnp.cos(ar), jnp.cos(ar), jnp.cos(ac), jnp.cos(ac)], axis=1)
    sin_t = jnp.concatenate([-jnp.sin(ar), jnp.sin(ar), -jnp.sin(ac), jnp.sin(ac)], axis=1)
    return cos_t, sin_t


def _encode(xa, xb, mem, p):
    seq = xa.shape[1]
    b = xa.shape[0] + xb.shape[0]
    t = b * seq
    ta = xa.shape[0] * seq
    cos_t, sin_t = _rope_tables(seq)
    x = _ln_call(xa.reshape(ta, D_MODEL), xb.reshape(t - ta, D_MODEL), p["ln_in_g"], p["ln_in_b"])
    c0 = D_ATTN_Q + 2 * D_ATTN_KV
    c1 = c0 + 3 * D_MODEL
    c2 = c1 + 2 * D_MODEL
    for l in range(DEPTH):
        w_in = p["w_in"][l].astype(BF16)
        qg, kg = p["q_norm_g"][l], p["k_norm_g"][l]
        bound = (ATTN_BOUND_MARGIN * LOG2E * HEAD_DIM ** 0.5
                 * jnp.max(jnp.abs(qg)) * jnp.max(jnp.abs(kg)))
        use_shift = bound <= ATTN_SHIFT_LIMIT
        q, k, vt = _qkv_call(x, w_in[:, :c0], cos_t, sin_t, qg, kg,
                             jnp.where(use_shift, bound, 0.0), seq)
        cb, pc = _convproj_call(x, w_in[:, c0:c1])
        gy, rx = _rnnproj_call(x, w_in[:, c1:c2])
        g = _gateproj_call(x, w_in[:, c2:], p["b_gate"][l].reshape(-1))
        ya = lax.cond(use_shift,
                      functools.partial(_attn_call, shifted=True),
                      functools.partial(_attn_call, shifted=False),
                      q.reshape(b, seq, -1), k.reshape(b, seq, -1), vt)
        h = _lru_call(rx.reshape(b, seq, -1), p["lru_conv_w"][l], p["lru_conv_b"][l],
                      p["lru_wa"][l].astype(BF16), p["lru_wx"][l].astype(BF16),
                      p["lru_ba"][l], p["lru_bx"][l], p["lru_lam"][l])
        x = _merge_call(x, cb, pc, ya.reshape(t, -1), h.reshape(t, -1), gy, g, p["conv_w"][l],
                        p["w_mix_out"][l].astype(BF16), p["ln1_g"][l], p["ln1_b"][l], seq)
        mk, mv = _kv_call(mem, p["xkv_w"][l].astype(BF16))
        rw = jnp.pad(p["router_w"][l], ((0, 0), (0, LANES - N_EXPERTS))).astype(BF16)
        rb = jnp.pad(p["router_b"][l], (0, LANES - N_EXPERTS), constant_values=NEG_BIG)
        x2, idx, gate, cnt = _cross_call(x, mk, mv, p["xq_w"][l].astype(BF16),
                                         p["xo_w"][l].astype(BF16), p["ln2_g"][l], p["ln2_b"][l],
                                         rw, rb.reshape(1, -1), seq)
        x = _moe(x2, idx[:, :TOP_K], gate[:, :TOP_K], cnt[:, 0, :N_EXPERTS],
                 p["w1"][l].astype(BF16), p["b1"][l][:, None, :], p["w2"][l].astype(BF16),
                 p["b2"][l][:, None, :], p["ln3_g"][l], p["ln3_b"][l],
                 split_rows=ta if l == DEPTH - 1 else None)
    ya, yb = x
    return ya.reshape(xa.shape), yb.reshape(xb.shape)


def kernel(x_prompt, x_sample, mem_prompt, mem_sample, ln_in_g, ln_in_b, w_in, b_gate, q_norm_g, k_norm_g, conv_w, lru_conv_w, lru_conv_b, lru_wa, lru_ba, lru_wx, lru_bx, lru_lam, w_mix_out, ln1_g, ln1_b, xq_w, xkv_w, xo_w, ln2_g, ln2_b, router_w, router_b, w1, b1, w2, b2, ln3_g, ln3_b):
    params = dict(ln_in_g=ln_in_g, ln_in_b=ln_in_b, w_in=w_in, b_gate=b_gate, q_norm_g=q_norm_g,
                  k_norm_g=k_norm_g, conv_w=conv_w, lru_conv_w=lru_conv_w, lru_conv_b=lru_conv_b,
                  lru_wa=lru_wa, lru_ba=lru_ba, lru_wx=lru_wx, lru_bx=lru_bx, lru_lam=lru_lam,
                  w_mix_out=w_mix_out, ln1_g=ln1_g, ln1_b=ln1_b, xq_w=xq_w, xkv_w=xkv_w, xo_w=xo_w,
                  ln2_g=ln2_g, ln2_b=ln2_b, router_w=router_w, router_b=router_b, w1=w1, b1=b1,
                  w2=w2, b2=b2, ln3_g=ln3_g, ln3_b=ln3_b)
    assert x_prompt.shape[1:] == x_sample.shape[1:]
    return _encode(x_prompt, x_sample, jnp.concatenate([mem_prompt, mem_sample], axis=0), params)
```

```python
import functools

import jax
import jax.numpy as jnp
from jax import lax
from jax.experimental import pallas as pl
from jax.experimental.pallas import tpu as pltpu

F32 = jnp.float32
BF16 = jnp.bfloat16
I32 = jnp.int32
U32 = jnp.uint32

D_MODEL = 1024
DEPTH = 2
N_HEADS = 8
N_KV_HEADS = 2
HEAD_DIM = 128
Q_GROUPS = N_HEADS // N_KV_HEADS
D_ATTN_Q = N_HEADS * HEAD_DIM
D_ATTN_KV = N_KV_HEADS * HEAD_DIM
GRID_W = 64
ROPE_AXIS_HALF = HEAD_DIM // 4
ROPE_THETA = 10000.0
QK_EPS = 1e-6
LRU_BLOCKS = 8
LRU_BLOCK_W = D_MODEL // LRU_BLOCKS
LRU_C = 8.0
MEM_HEADS = 4
MEM_HEAD_DIM = D_MODEL // MEM_HEADS
N_EXPERTS = 32
TOP_K = 4
D_FF = D_MODEL // 2
SWIGLU_LIMIT = 7.0
SWIGLU_ALPHA = 1.702
LN_EPS = 1e-5
DEEPNORM_ALPHA = (2 * DEPTH) ** 0.25

LANES = 128
SUBLANES = 8
BF16_SUBLANES = 16
VMEM_LIMIT = 56 * 1024 * 1024

PROJ_TM = 512
ATTN_TQ = 256
ATTN_TK = 512
ATTN_PAIRS_PER_BODY = 8
TOK_TILE = 256
CROSS_TM = 512
LRU_CHUNK = 512
LRU_SUBSEQ = 16
LRU_UNROLL = 8
MOE_GROUP = SUBLANES
MOE_BLOCK = 512
MOE_HALF = 256
MOE_TILE_ROWS = TOK_TILE * TOP_K + N_EXPERTS * MOE_GROUP
HEAD_AUG = 2 * HEAD_DIM
ATTN_SHIFT_LIMIT = 60.0
ATTN_BOUND_MARGIN = 1.02
NEG_BIG = -1e30
LOG2E = 1.4426950408889634


def _cparams(sem, vmem=VMEM_LIMIT):
    return pltpu.CompilerParams(dimension_semantics=sem, vmem_limit_bytes=vmem)


def _layer_norm(x, g, b):
    mu = jnp.mean(x, axis=-1, keepdims=True)
    xc = x - mu
    var = jnp.mean(xc * xc, axis=-1, keepdims=True)
    return xc * lax.rsqrt(var + LN_EPS) * g + b


def _sigmoid(x):
    return 1.0 / (1.0 + jnp.exp(-x))


def _sigmoid_tanh(x):
    return 0.5 * jnp.tanh(0.5 * x) + 0.5


def _dot(a, b):
    return jnp.dot(a, b, preferred_element_type=F32)


def _dot_nt(a, b):
    return lax.dot_general(a, b, (((1,), (1,)), ((), ())), preferred_element_type=F32)


def _ln_kernel(xa_ref, xb_ref, g_ref, b_ref, o_ref, *, tiles_a):
    @pl.when(pl.program_id(0) < tiles_a)
    def _():
        o_ref[...] = _layer_norm(xa_ref[...], g_ref[...], b_ref[...])

    @pl.when(pl.program_id(0) >= tiles_a)
    def _():
        o_ref[...] = _layer_norm(xb_ref[...], g_ref[...], b_ref[...])


def _ln_call(xa, xb, g, b):
    tm = PROJ_TM
    ta, tb = xa.shape[0] // tm, xb.shape[0] // tm
    return pl.pallas_call(
        functools.partial(_ln_kernel, tiles_a=ta),
        grid=(ta + tb,),
        in_specs=[pl.BlockSpec((tm, D_MODEL), lambda i: (jnp.minimum(i, ta - 1), 0)),
                  pl.BlockSpec((tm, D_MODEL), lambda i: (jnp.maximum(i - ta, 0), 0)),
                  pl.BlockSpec((1, D_MODEL), lambda i: (0, 0)),
                  pl.BlockSpec((1, D_MODEL), lambda i: (0, 0))],
        out_specs=pl.BlockSpec((tm, D_MODEL), lambda i: (i, 0)),
        out_shape=jax.ShapeDtypeStruct(((ta + tb) * tm, D_MODEL), F32),
        compiler_params=_cparams(("arbitrary",)),
        name="ln_in",
    )(xa, xb, g.reshape(1, -1), b.reshape(1, -1))


def _qkv_kernel(x_ref, w_ref, cos_ref, sin_ref, qg_ref, kg_ref, shift_ref, q_ref, k_ref, vt_ref):
    x = x_ref[...].astype(BF16)
    z = _dot(x, w_ref[...])
    c = cos_ref[...]
    s = sin_ref[...]
    lane = lax.broadcasted_iota(I32, c.shape, 1)
    first_half = (lane % (2 * ROPE_AXIS_HALF)) < ROPE_AXIS_HALF

    def prep(zh, g, scale):
        ms = jnp.mean(zh * zh, axis=-1, keepdims=True)
        y = zh * lax.rsqrt(ms + QK_EPS) * g
        partner = jnp.where(first_half,
                            pltpu.roll(y, HEAD_DIM - ROPE_AXIS_HALF, 1),
                            pltpu.roll(y, ROPE_AXIS_HALF, 1))
        return (y * c + partner * s) * scale

    qg = qg_ref[...]
    kg = kg_ref[...]
    q_extra = jnp.broadcast_to(shift_ref[...], c.shape).astype(q_ref.dtype)
    k_extra = jnp.where(lane == 0, 1.0, 0.0).astype(k_ref.dtype)
    for h in range(N_HEADS):
        sl = slice(h * HEAD_DIM, (h + 1) * HEAD_DIM)
        q_ref[:, h * HEAD_AUG:h * HEAD_AUG + HEAD_DIM] = prep(
            z[:, sl], qg, LOG2E * HEAD_DIM ** -0.5).astype(q_ref.dtype)
        q_ref[:, h * HEAD_AUG + HEAD_DIM:(h + 1) * HEAD_AUG] = q_extra
    for h in range(N_KV_HEADS):
        zsl = slice(D_ATTN_Q + h * HEAD_DIM, D_ATTN_Q + (h + 1) * HEAD_DIM)
        k_ref[:, h * HEAD_AUG:h * HEAD_AUG + HEAD_DIM] = prep(z[:, zsl], kg, 1.0).astype(k_ref.dtype)
        k_ref[:, h * HEAD_AUG + HEAD_DIM:(h + 1) * HEAD_AUG] = k_extra
    vt_ref[0] = z[:, D_ATTN_Q + D_ATTN_KV:].T.astype(vt_ref.dtype)


def _qkv_call(x, w, cos_t, sin_t, qg, kg, shift, seq):
    t = x.shape[0]
    tm = PROJ_TM
    nseq = seq // tm
    n = w.shape[1]
    shift_row = jnp.zeros((1, HEAD_DIM), F32).at[0, 0].set(-shift)
    vec_spec = pl.BlockSpec((1, HEAD_DIM), lambda i: (0, 0))
    return pl.pallas_call(
        _qkv_kernel,
        grid=(t // tm,),
        in_specs=[pl.BlockSpec((tm, D_MODEL), lambda i: (i, 0)),
                  pl.BlockSpec((D_MODEL, n), lambda i: (0, 0)),
                  pl.BlockSpec((tm, HEAD_DIM), lambda i: (i % nseq, 0)),
                  pl.BlockSpec((tm, HEAD_DIM), lambda i: (i % nseq, 0)),
                  vec_spec, vec_spec, vec_spec],
        out_specs=[pl.BlockSpec((tm, N_HEADS * HEAD_AUG), lambda i: (i, 0)),
                   pl.BlockSpec((tm, N_KV_HEADS * HEAD_AUG), lambda i: (i, 0)),
                   pl.BlockSpec((1, D_ATTN_KV, tm), lambda i: (i // nseq, 0, i % nseq))],
        out_shape=[jax.ShapeDtypeStruct((t, N_HEADS * HEAD_AUG), BF16),
                   jax.ShapeDtypeStruct((t, N_KV_HEADS * HEAD_AUG), BF16),
                   jax.ShapeDtypeStruct((t // seq, D_ATTN_KV, seq), BF16)],
        compiler_params=_cparams(("parallel",)),
        name="qkv_proj",
    )(x, w, cos_t, sin_t, qg.reshape(1, -1), kg.reshape(1, -1), shift_row)


def _convproj_kernel(x_ref, w_ref, cb_ref, p_ref):
    x = x_ref[...].astype(BF16)
    z = _dot(x, w_ref[...])
    cb_ref[...] = z[:, :D_MODEL].astype(cb_ref.dtype)
    p_ref[...] = (z[:, D_MODEL:2 * D_MODEL] * z[:, 2 * D_MODEL:]).astype(p_ref.dtype)


def _convproj_call(x, w):
    t = x.shape[0]
    tm = PROJ_TM
    return pl.pallas_call(
        _convproj_kernel,
        grid=(t // tm,),
        in_specs=[pl.BlockSpec((tm, D_MODEL), lambda i: (i, 0)),
                  pl.BlockSpec((D_MODEL, 3 * D_MODEL), lambda i: (0, 0))],
        out_specs=[pl.BlockSpec((tm, D_MODEL), lambda i: (i, 0)),
                   pl.BlockSpec((tm, D_MODEL), lambda i: (i, 0))],
        out_shape=[jax.ShapeDtypeStruct((t, D_MODEL), BF16),
                   jax.ShapeDtypeStruct((t, D_MODEL), BF16)],
        compiler_params=_cparams(("parallel",)),
        name="conv_proj",
    )(x, w)


def _rnnproj_kernel(x_ref, w_ref, gy_ref, rx_ref):
    x = x_ref[...].astype(BF16)
    z = _dot(x, w_ref[...])
    gy_ref[...] = jax.nn.gelu(z[:, :D_MODEL]).astype(gy_ref.dtype)
    rx_ref[...] = z[:, D_MODEL:]


def _rnnproj_call(x, w):
    t = x.shape[0]
    tm = PROJ_TM
    return pl.pallas_call(
        _rnnproj_kernel,
        grid=(t // tm,),
        in_specs=[pl.BlockSpec((tm, D_MODEL), lambda i: (i, 0)),
                  pl.BlockSpec((D_MODEL, 2 * D_MODEL), lambda i: (0, 0))],
        out_specs=[pl.BlockSpec((tm, D_MODEL), lambda i: (i, 0)),
                   pl.BlockSpec((tm, D_MODEL), lambda i: (i, 0))],
        out_shape=[jax.ShapeDtypeStruct((t, D_MODEL), BF16),
                   jax.ShapeDtypeStruct((t, D_MODEL), F32)],
        compiler_params=_cparams(("parallel",)),
        name="rnn_proj",
    )(x, w)


def _gateproj_kernel(x_ref, w_ref, b_ref, g_ref):
    x = x_ref[...].astype(BF16)
    z = _dot(x, w_ref[...]) + b_ref[...]
    g_ref[...] = _sigmoid(z).astype(g_ref.dtype)


def _gateproj_call(x, w, b):
    t = x.shape[0]
    tm = PROJ_TM
    return pl.pallas_call(
        _gateproj_kernel,
        grid=(t // tm,),
        in_specs=[pl.BlockSpec((tm, D_MODEL), lambda i: (i, 0)),
                  pl.BlockSpec((D_MODEL, 3 * D_MODEL), lambda i: (0, 0)),
                  pl.BlockSpec((1, 3 * D_MODEL), lambda i: (0, 0))],
        out_specs=pl.BlockSpec((tm, 3 * D_MODEL), lambda i: (i, 0)),
        out_shape=jax.ShapeDtypeStruct((t, 3 * D_MODEL), BF16),
        compiler_params=_cparams(("parallel",)),
        name="gate_proj",
    )(x, w, b.reshape(1, -1))


def _attn_kernel(q_ref, k_ref, vt_ref, o_ref, acc_s, s0_s, s1_s, p0_s, p1_s, *, tk):
    tq = q_ref.shape[1]
    n = Q_GROUPS * tq
    nk = k_ref.shape[1] // tk
    qs = _stack_heads(q_ref)
    acc_s[...] = jnp.zeros(acc_s.shape, F32)
    s_bufs = (s0_s, s1_s)
    p_bufs = (p0_s, p1_s)

    def scores(j, slot):
        start = pl.multiple_of(j * tk, tk)
        s_bufs[slot][...] = _dot_nt(k_ref[0, pl.ds(start, tk), :], qs)

    def softmax(slot, m, l):
        st = s_bufs[slot][...]
        m_new = jnp.maximum(m, jnp.max(st, axis=0, keepdims=True))
        alpha = jnp.exp2(m - m_new)
        p = jnp.exp2(st - m_new)
        p_bufs[slot][...] = p.astype(BF16)
        return m_new, alpha * l + jnp.sum(p, axis=0, keepdims=True), alpha

    def accumulate(j, slot, alpha):
        start = pl.multiple_of(j * tk, tk)
        acc_s[...] = alpha * acc_s[...] + _dot(vt_ref[0, :, pl.ds(start, tk)], p_bufs[slot][...])

    def stage(j, m, l, alphas, first, last):
        new_alphas = []
        for slot in range(2):
            if not first:
                accumulate(j - 2 + slot, slot, alphas[slot])
            m, l, a = softmax(slot, m, l)
            new_alphas.append(a)
            if not last:
                scores(j + 2 + slot, slot)
        return m, l, tuple(new_alphas)

    m = jnp.full((1, n), NEG_BIG, F32)
    l = jnp.zeros((1, n), F32)
    scores(0, 0)
    scores(1, 1)
    if nk == 2:
        m, l, alphas = stage(0, m, l, None, True, True)
    else:
        m, l, alphas = stage(0, m, l, None, True, False)
        m, l, alphas = lax.fori_loop(
            1, nk // 2 - 1, lambda i, c: stage(2 * i, c[0], c[1], c[2], False, False), (m, l, alphas))
        m, l, alphas = stage(nk - 2, m, l, alphas, False, True)
    accumulate(nk - 2, 0, alphas[0])
    accumulate(nk - 1, 1, alphas[1])
    _write_heads(o_ref, acc_s[...] / l)


def _stack_heads(q_ref):
    return jnp.concatenate([q_ref[0, :, g * HEAD_AUG:(g + 1) * HEAD_AUG] for g in range(Q_GROUPS)],
                           axis=0)


def _write_heads(o_ref, ot):
    tq = o_ref.shape[1]
    for g in range(Q_GROUPS):
        o_ref[0, :, g * HEAD_DIM:(g + 1) * HEAD_DIM] = ot[:, g * tq:(g + 1) * tq].T.astype(o_ref.dtype)


def _attn_shifted_kernel(q_ref, k_ref, vt_ref, o_ref, acc_s, p0_s, p1_s, *, tk):
    n = Q_GROUPS * q_ref.shape[1]
    nk = k_ref.shape[1] // tk
    qs = _stack_heads(q_ref)
    acc_s[...] = jnp.zeros(acc_s.shape, F32)
    p_bufs = (p0_s, p1_s)

    def probs(j, slot, l):
        start = pl.multiple_of(j * tk, tk)
        p = jnp.exp2(_dot_nt(k_ref[0, pl.ds(start, tk), :], qs))
        p_bufs[slot][...] = p.astype(BF16)
        return l + jnp.sum(p, axis=0, keepdims=True)

    def accumulate(j, slot):
        start = pl.multiple_of(j * tk, tk)
        acc_s[...] += _dot(vt_ref[0, :, pl.ds(start, tk)], p_bufs[slot][...])

    def pair(i, l):
        j = 2 * i
        l = probs(j + 1, 1, l)
        accumulate(j, 0)
        l = probs(j + 2, 0, l)
        accumulate(j + 1, 1)
        return l

    l = probs(0, 0, jnp.zeros((1, n), F32))
    n_pairs = nk // 2 - 1
    trips = n_pairs // ATTN_PAIRS_PER_BODY

    def body(i, l):
        for r in range(ATTN_PAIRS_PER_BODY):
            l = pair(i * ATTN_PAIRS_PER_BODY + r, l)
        return l

    if trips > 1:
        l = lax.fori_loop(0, trips, body, l)
    else:
        trips = 0
    for i in range(trips * ATTN_PAIRS_PER_BODY, n_pairs):
        l = pair(i, l)
    l = probs(nk - 1, 1, l)
    accumulate(nk - 2, 0)
    accumulate(nk - 1, 1)
    _write_heads(o_ref, acc_s[...] / l)


def _attn_call(q, k, vt, shifted):
    b, s, _ = q.shape
    tq = min(ATTN_TQ, s)
    tk = min(ATTN_TK, s // 2)
    assert (s // tk) % 2 == 0
    n = Q_GROUPS * tq
    p_bufs = [pltpu.VMEM((tk, n), BF16)] * 2
    s_bufs = [] if shifted else [pltpu.VMEM((tk, n), F32)] * 2
    return pl.pallas_call(
        functools.partial(_attn_shifted_kernel if shifted else _attn_kernel, tk=tk),
        grid=(b, N_KV_HEADS, s // tq),
        in_specs=[pl.BlockSpec((1, tq, Q_GROUPS * HEAD_AUG), lambda bi, h, i: (bi, i, h)),
                  pl.BlockSpec((1, s, HEAD_AUG), lambda bi, h, i: (bi, 0, h)),
                  pl.BlockSpec((1, HEAD_DIM, s), lambda bi, h, i: (bi, h, 0))],
        out_specs=pl.BlockSpec((1, tq, Q_GROUPS * HEAD_DIM), lambda bi, h, i: (bi, i, h)),
        out_shape=jax.ShapeDtypeStruct((b, s, D_ATTN_Q), BF16),
        scratch_shapes=[pltpu.VMEM((HEAD_DIM, n), F32)] + s_bufs + p_bufs,
        compiler_params=_cparams(("parallel", "parallel", "parallel")),
        name="flash_attn_shifted" if shifted else "flash_attn",
    )(q, k, vt)


def _lru_kernel(x_ref, cw_ref, cb_ref, wa_ref, wx_ref, ba_ref, bx_ref, lam_ref, o_hbm,
                xp_s, a0_s, u0_s, a1_s, u1_s, h_s, sem, *, seq, chunk):
    sub = seq // LRU_SUBSEQ
    cps = sub // chunk
    nchunks = seq // chunk

    zeros8 = jnp.zeros((SUBLANES, LANES), F32)
    xp_s[0:SUBLANES, :] = zeros8
    xp_s[seq + SUBLANES:seq + 2 * SUBLANES, :] = zeros8
    xp_s[SUBLANES:seq + SUBLANES, :] = x_ref[0]
    cw = cw_ref[...]
    cb = cb_ref[...]
    a_bufs = (a0_s, a1_s)
    u_bufs = (u0_s, u1_s)

    neg = -lam_ref[...]
    softplus = jnp.maximum(neg, 0.0) + jnp.log1p(jnp.exp(-jnp.abs(neg)))
    half_decay = -0.5 * LRU_C * softplus

    def gate_chunk(c, _):
        base = pl.multiple_of(c * chunk, chunk)
        xc = cb
        for k in range(4):
            xc = xc + cw[k:k + 1, :] * xp_s[pl.ds(base + SUBLANES - 1 + k, chunk), :]
        xb = xc.astype(BF16)
        xh = 0.5 * xc
        dst = pl.ds((c % cps) * (chunk * LRU_SUBSEQ) + c // cps, chunk, stride=LRU_SUBSEQ)
        for d in range(2):
            tr = jnp.tanh(_dot(xb, wa_ref[d, 0]) + ba_ref[d:d + 1, :])
            ti = jnp.tanh(_dot(xb, wx_ref[d, 0]) + bx_ref[d:d + 1, :])
            a = jnp.exp(half_decay[d:d + 1, :] * tr + half_decay[d:d + 1, :])
            u_bufs[d][dst, :] = jnp.sqrt(1.0 - a * a) * (xh * (ti + 1.0))
            a_bufs[d][dst, :] = a
        return 0

    lax.fori_loop(0, nchunks, gate_chunk, 0)

    def step(tt, carry):
        out = []
        for d in range(2):
            h, p = carry[d]
            t = tt if d == 0 else sub - 1 - tt
            idx = pl.ds(pl.multiple_of(t * LRU_SUBSEQ, LRU_SUBSEQ), LRU_SUBSEQ)
            av = a_bufs[d][idx, :]
            h = av * h + u_bufs[d][idx, :]
            p = av * p
            u_bufs[d][idx, :] = h
            a_bufs[d][idx, :] = p
            out.append((h, p))
        return tuple(out)

    h0 = jnp.zeros((LRU_SUBSEQ, LANES), F32)
    p0 = jnp.ones((LRU_SUBSEQ, LANES), F32)
    ends = lax.fori_loop(0, sub, step, ((h0, p0), (h0, p0)), unroll=LRU_UNROLL)

    carry_in = []
    for d in range(2):
        h_end, p_end = ends[d]
        order = range(LRU_SUBSEQ) if d == 0 else range(LRU_SUBSEQ - 1, -1, -1)
        states = [None] * LRU_SUBSEQ
        state = jnp.zeros((1, LANES), F32)
        for j in order:
            states[j] = state
            state = p_end[j:j + 1, :] * state + h_end[j:j + 1, :]
        carry_in.append(jnp.concatenate(states, axis=0))

    step_id = pl.program_id(0) * pl.num_programs(1) + pl.program_id(1)
    last_id = pl.num_programs(0) * pl.num_programs(1) - 1

    def out_copies(bi, ci):
        return [pltpu.make_async_copy(
            h_s.at[:, j, :],
            o_hbm.at[bi, pl.ds(j * sub, sub), pl.ds(pl.multiple_of(ci * LANES, LANES), LANES)],
            sem) for j in range(LRU_SUBSEQ)]

    @pl.when(step_id > 0)
    def _():
        for cp in out_copies(0, 0):
            cp.wait()

    steps = min(sub, chunk)
    for c in range(sub // steps):
        rows = slice(c * steps * LRU_SUBSEQ, (c + 1) * steps * LRU_SUBSEQ)
        c0 = jnp.tile(carry_in[0], (steps, 1))
        c1 = jnp.tile(carry_in[1], (steps, 1))
        val = u0_s[rows, :] + a0_s[rows, :] * c0 + u1_s[rows, :] + a1_s[rows, :] * c1
        h_s[c * steps:(c + 1) * steps] = val.reshape(steps, LRU_SUBSEQ, LANES)

    for cp in out_copies(pl.program_id(0), pl.program_id(1)):
        cp.start()

    @pl.when(step_id == last_id)
    def _():
        for cp in out_copies(0, 0):
            cp.wait()


def _lru_call(rx, cw, cb, wa, wx, ba, bx, lam):
    b, s, _ = rx.shape
    chunk = min(LRU_CHUNK, s // LRU_SUBSEQ)
    assert LRU_SUBSEQ % SUBLANES == 0
    w = LRU_BLOCK_W
    return pl.pallas_call(
        functools.partial(_lru_kernel, seq=s, chunk=chunk),
        grid=(b, LRU_BLOCKS),
        in_specs=[pl.BlockSpec((1, s, w), lambda bi, c: (bi, 0, c)),
                  pl.BlockSpec((4, w), lambda bi, c: (0, c)),
                  pl.BlockSpec((1, w), lambda bi, c: (0, c)),
                  pl.BlockSpec((2, 1, w, w), lambda bi, c: (0, c, 0, 0)),
                  pl.BlockSpec((2, 1, w, w), lambda bi, c: (0, c, 0, 0)),
                  pl.BlockSpec((2, w), lambda bi, c: (0, c)),
                  pl.BlockSpec((2, w), lambda bi, c: (0, c)),
                  pl.BlockSpec((2, w), lambda bi, c: (0, c))],
        out_specs=pl.BlockSpec(memory_space=pl.ANY),
        out_shape=jax.ShapeDtypeStruct((b, s, D_MODEL), F32),
        scratch_shapes=[pltpu.VMEM((s + 2 * SUBLANES, w), F32)]
        + [pltpu.VMEM((s, w), F32)] * 4
        + [pltpu.VMEM((s // LRU_SUBSEQ, LRU_SUBSEQ, w), F32), pltpu.SemaphoreType.DMA(())],
        compiler_params=_cparams(("arbitrary", "arbitrary")),
        name="rg_lru",
    )(rx, cw, cb.reshape(1, -1), wa, wx, ba, bx, lam)


def _merge_kernel(x_ref, cb_ref, p_ref, pprev_ref, pnext_ref, ya_ref, h_ref, gy_ref, g_ref,
                  cw_ref, w_ref, lg_ref, lb_ref, o_ref, *, seq):
    tm = x_ref.shape[0]
    t0 = pl.program_id(0) * tm
    p = p_ref[...].astype(F32)
    row = lax.broadcasted_iota(I32, p.shape, 0)
    has_prev = ((t0 % seq) != 0).astype(F32)
    has_next = (((t0 + tm) % seq) != 0).astype(F32)
    prev_row = pprev_ref[BF16_SUBLANES - 1:BF16_SUBLANES, :].astype(F32) * has_prev
    next_row = pnext_ref[0:1, :].astype(F32) * has_next
    p_dn = jnp.where(row == 0, prev_row, pltpu.roll(p, 1, 0))
    p_up = jnp.where(row == tm - 1, next_row, pltpu.roll(p, tm - 1, 0))
    cw = cw_ref[...]
    y_conv = cb_ref[...].astype(F32) * (cw[0:1, :] * p_dn + cw[1:2, :] * p + cw[2:3, :] * p_up)
    y_rnn = h_ref[...] * gy_ref[...].astype(F32)
    merged = (g_ref[:, :D_MODEL].astype(F32) * y_conv
              + g_ref[:, D_MODEL:2 * D_MODEL].astype(F32) * ya_ref[...].astype(F32)
              + g_ref[:, 2 * D_MODEL:].astype(F32) * y_rnn)
    mix = _dot(merged.astype(BF16), w_ref[...])
    o_ref[...] = _layer_norm(DEEPNORM_ALPHA * x_ref[...] + mix, lg_ref[...], lb_ref[...])


def _merge_call(x, cb, p, ya, h, gy, g, cw, w, lg, lb, seq):
    t = x.shape[0]
    tm = TOK_TILE
    hb = BF16_SUBLANES
    per = tm // hb
    last = t // hb - 1
    row_spec = pl.BlockSpec((tm, D_MODEL), lambda i: (i, 0))
    vec_spec = pl.BlockSpec((1, D_MODEL), lambda i: (0, 0))
    return pl.pallas_call(
        functools.partial(_merge_kernel, seq=seq),
        grid=(t // tm,),
        in_specs=[row_spec, row_spec, row_spec,
                  pl.BlockSpec((hb, D_MODEL), lambda i: (jnp.maximum(i * per - 1, 0), 0)),
                  pl.BlockSpec((hb, D_MODEL), lambda i: (jnp.minimum((i + 1) * per, last), 0)),
                  row_spec, row_spec, row_spec,
                  pl.BlockSpec((tm, 3 * D_MODEL), lambda i: (i, 0)),
                  pl.BlockSpec((3, D_MODEL), lambda i: (0, 0)),
                  pl.BlockSpec((D_MODEL, D_MODEL), lambda i: (0, 0)),
                  vec_spec, vec_spec],
        out_specs=row_spec,
        out_shape=jax.ShapeDtypeStruct((t, D_MODEL), F32),
        compiler_params=_cparams(("parallel",)),
        name="merge_outproj_ln1",
    )(x, cb, p, p, p, ya, h, gy, g, cw, w, lg.reshape(1, -1), lb.reshape(1, -1))


def _kv_kernel(m_ref, w_ref, k_ref, v_ref):
    z = _dot(m_ref[0].astype(BF16), w_ref[...])
    k_ref[0] = z[:, :D_MODEL].astype(k_ref.dtype)
    v_ref[0] = z[:, D_MODEL:].astype(v_ref.dtype)


def _kv_call(mem, w):
    b, m, _ = mem.shape
    return pl.pallas_call(
        _kv_kernel,
        grid=(b,),
        in_specs=[pl.BlockSpec((1, m, D_MODEL), lambda i: (i, 0, 0)),
                  pl.BlockSpec((D_MODEL, 2 * D_MODEL), lambda i: (0, 0))],
        out_specs=[pl.BlockSpec((1, m, D_MODEL), lambda i: (i, 0, 0)),
                   pl.BlockSpec((1, m, D_MODEL), lambda i: (i, 0, 0))],
        out_shape=[jax.ShapeDtypeStruct((b, m, D_MODEL), BF16),
                   jax.ShapeDtypeStruct((b, m, D_MODEL), BF16)],
        compiler_params=_cparams(("parallel",)),
        name="mem_kv_proj",
    )(mem, w)


def _cross_kernel(x_ref, k_ref, v_ref, wq_ref, wo_ref, lg_ref, lb_ref, rw_ref, rb_ref,
                  x2_ref, idx_ref, gate_ref, cnt_ref):
    for r in range(x_ref.shape[0] // TOK_TILE):
        rows = slice(r * TOK_TILE, (r + 1) * TOK_TILE)
        _cross_rows(x_ref[rows, :], k_ref, v_ref, wq_ref, wo_ref, lg_ref, lb_ref, rw_ref, rb_ref,
                    x2_ref.at[rows, :], idx_ref.at[rows, :], gate_ref.at[rows, :], cnt_ref.at[r])


def _cross_rows(x, k_ref, v_ref, wq_ref, wo_ref, lg_ref, lb_ref, rw_ref, rb_ref,
                x2_ref, idx_ref, gate_ref, cnt_ref):
    q = _dot(x.astype(BF16), wq_ref[...]) * (MEM_HEAD_DIM ** -0.5)
    outs = []
    for h in range(MEM_HEADS):
        sl = slice(h * MEM_HEAD_DIM, (h + 1) * MEM_HEAD_DIM)
        s = _dot_nt(q[:, sl].astype(BF16), k_ref[0, :, sl])
        m = jnp.max(s, axis=-1, keepdims=True)
        p = jnp.exp(s - m)
        p = p / jnp.sum(p, axis=-1, keepdims=True)
        outs.append(_dot(p.astype(BF16), v_ref[0, :, sl]))
    o = jnp.concatenate(outs, axis=1)
    cross = _dot(o.astype(BF16), wo_ref[...])
    x2 = _layer_norm(DEEPNORM_ALPHA * x + cross, lg_ref[...], lb_ref[...])
    x2_ref[...] = x2

    logits = _dot(x2.astype(BF16), rw_ref[...]) + rb_ref[...]
    lane = lax.broadcasted_iota(I32, logits.shape, 1)
    idx_acc = jnp.zeros(logits.shape, I32)
    val_acc = jnp.zeros(logits.shape, F32)
    member = jnp.zeros(logits.shape, F32)
    top0 = None
    for k in range(TOP_K):
        m = jnp.max(logits, axis=-1, keepdims=True)
        pick = jnp.min(jnp.where(logits == m, lane, LANES), axis=-1, keepdims=True)
        hit = lane == pick
        if k == 0:
            top0 = m
        idx_acc = jnp.where(lane == k, pick, idx_acc)
        val_acc = jnp.where(lane == k, jnp.exp(m - top0), val_acc)
        member = member + hit.astype(F32)
        logits = jnp.where(hit, NEG_BIG, logits)
    idx_ref[...] = idx_acc
    gate_ref[...] = val_acc / jnp.sum(val_acc, axis=-1, keepdims=True)
    cnt_ref[...] = jnp.sum(member, axis=0, keepdims=True).astype(I32)


def _cross_call(x, k, v, wq, wo, lg, lb, rw, rb, seq):
    t = x.shape[0]
    tm = CROSS_TM
    halves = tm // TOK_TILE
    per_seq = seq // tm
    m = k.shape[1]
    row_spec = pl.BlockSpec((tm, D_MODEL), lambda i: (i, 0))
    vec_spec = pl.BlockSpec((1, D_MODEL), lambda i: (0, 0))
    lane_spec = pl.BlockSpec((tm, LANES), lambda i: (i, 0))
    return pl.pallas_call(
        _cross_kernel,
        grid=(t // tm,),
        in_specs=[row_spec,
                  pl.BlockSpec((1, m, D_MODEL), lambda i: (i // per_seq, 0, 0)),
                  pl.BlockSpec((1, m, D_MODEL), lambda i: (i // per_seq, 0, 0)),
                  pl.BlockSpec((D_MODEL, D_MODEL), lambda i: (0, 0)),
                  pl.BlockSpec((D_MODEL, D_MODEL), lambda i: (0, 0)),
                  vec_spec, vec_spec,
                  pl.BlockSpec((D_MODEL, LANES), lambda i: (0, 0)),
                  pl.BlockSpec((1, LANES), lambda i: (0, 0))],
        out_specs=[row_spec, lane_spec, lane_spec,
                   pl.BlockSpec((halves, 1, LANES), lambda i: (i, 0, 0))],
        out_shape=[jax.ShapeDtypeStruct((t, D_MODEL), F32),
                   jax.ShapeDtypeStruct((t, LANES), I32),
                   jax.ShapeDtypeStruct((t, LANES), F32),
                   jax.ShapeDtypeStruct((t // TOK_TILE, 1, LANES), I32)],
        compiler_params=_cparams(("parallel",)),
        name="cross_attn_ln2_router",
    )(x, k, v, wq, wo, lg.reshape(1, -1), lb.reshape(1, -1), rw, rb)


def _pack_rows(v):
    bits = pltpu.bitcast(v, U32)
    half = v.shape[1] // 2
    return (bits[:, :half] & jnp.uint32(0xFFFF0000)) | (bits[:, half:] >> 16)


def _unpack_rows(u):
    hi = pltpu.bitcast(u & jnp.uint32(0xFFFF0000), F32).astype(BF16)
    lo = pltpu.bitcast(u << 16, F32).astype(BF16)
    return hi, lo


def _groups_copy(src, dst, sem, s_grp, d_grp, n_grp):
    rows = n_grp * MOE_GROUP
    return pltpu.make_async_copy(
        src.at[pl.ds(pl.multiple_of(s_grp * MOE_GROUP, MOE_GROUP), rows), :],
        dst.at[pl.ds(pl.multiple_of(d_grp * MOE_GROUP, MOE_GROUP), rows), :], sem)


def _moe_sort_kernel(off8_ref, len8_ref, dst8_ref, x_ref, idx_ref, offv_ref, dest_ref, xs_hbm,
                     xs_s, sem, pending_s):
    s = pl.program_id(0)
    tm = x_ref.shape[0]
    idx = idx_ref[...]
    e_iota = lax.broadcasted_iota(I32, (N_EXPERTS, tm), 0)
    member = jnp.zeros((N_EXPERTS, tm), F32)
    for k in range(TOP_K):
        member = member + (e_iota == idx[k:k + 1, :]).astype(F32)
    earlier = (lax.broadcasted_iota(I32, (tm, tm), 0)
               < lax.broadcasted_iota(I32, (tm, tm), 1)).astype(BF16)
    rank = _dot(member.astype(BF16), earlier)
    pos = rank + offv_ref[0][:, 0:1]
    r_iota = lax.broadcasted_iota(I32, (MOE_TILE_ROWS, tm), 0)
    onehot = jnp.zeros((MOE_TILE_ROWS, tm), F32)
    dests = []
    for k in range(TOP_K):
        dk = jnp.sum(jnp.where(e_iota == idx[k:k + 1, :], pos, 0.0), axis=0, keepdims=True)
        dk = dk.astype(I32)
        dests.append(dk)
        onehot = jnp.where(r_iota == dk, 1.0, onehot)
    dest_ref[...] = jnp.concatenate(dests, axis=0)
    slot = s % 2
    xs_s[slot] = _pack_rows(_dot(onehot.astype(BF16), x_ref[...].astype(BF16)))

    def per_expert(e, total):
        n = len8_ref[s * N_EXPERTS + e]

        @pl.when(n > 0)
        def _():
            _groups_copy(xs_s.at[slot], xs_hbm, sem.at[slot], off8_ref[s * N_EXPERTS + e],
                         dst8_ref[s * N_EXPERTS + e], n).start()

        return total + n

    total = lax.fori_loop(0, N_EXPERTS, per_expert, 0)

    def drain(which, count):
        @pl.when(count > 0)
        def _():
            _groups_copy(xs_s.at[which], xs_hbm, sem.at[which], 0, 0, count).wait()

    @pl.when(s > 0)
    def _():
        drain(1 - slot, pending_s[0])

    pending_s[0] = total

    @pl.when(s == pl.num_programs(0) - 1)
    def _():
        drain(slot, total)


def _moe_sort_call(x2, idx_t, offv, off8, len8, dst8, n_rows):
    t = x2.shape[0]
    tm = TOK_TILE
    grid_spec = pltpu.PrefetchScalarGridSpec(
        num_scalar_prefetch=3,
        grid=(t // tm,),
        in_specs=[pl.BlockSpec((tm, D_MODEL), lambda i, *_: (i, 0)),
                  pl.BlockSpec((TOP_K, tm), lambda i, *_: (0, i)),
                  pl.BlockSpec((1, N_EXPERTS, LANES), lambda i, *_: (i, 0, 0))],
        out_specs=[pl.BlockSpec((TOP_K, tm), lambda i, *_: (0, i)),
                   pl.BlockSpec(memory_space=pl.ANY)],
        scratch_shapes=[pltpu.VMEM((2, MOE_TILE_ROWS, D_MODEL // 2), U32),
                        pltpu.SemaphoreType.DMA((2,)),
                        pltpu.SMEM((1,), I32)],
    )
    return pl.pallas_call(
        _moe_sort_kernel,
        grid_spec=grid_spec,
        out_shape=[jax.ShapeDtypeStruct((TOP_K, t), I32),
                   jax.ShapeDtypeStruct((n_rows, D_MODEL // 2), U32)],
        compiler_params=_cparams(("arbitrary",)),
        name="moe_sort",
    )(off8, len8, dst8, x2, idx_t, offv)


def _moe_ffn_kernel(be_ref, nb_ref, xs_ref, w1_ref, b1_ref, w2_ref, b2_ref, o_ref):
    @pl.when(pl.program_id(0) < nb_ref[0])
    def _():
        half = D_MODEL // 2
        for r in range(MOE_BLOCK // MOE_HALF):
            rows = slice(r * MOE_HALF, (r + 1) * MOE_HALF)
            hi, lo = _unpack_rows(xs_ref[rows, :])
            hcat = _dot(hi, w1_ref[0, :half, :]) + _dot(lo, w1_ref[0, half:, :]) + b1_ref[0]
            glu = jnp.minimum(hcat[:, :D_FF], SWIGLU_LIMIT)
            lin = jnp.clip(hcat[:, D_FF:], -SWIGLU_LIMIT, SWIGLU_LIMIT)
            act = glu * _sigmoid_tanh(SWIGLU_ALPHA * glu) * (lin + 1.0)
            out = _dot(act.astype(BF16), w2_ref[0]) + b2_ref[0]
            o_ref[rows, :] = _pack_rows(out.astype(BF16).astype(F32))


def _moe_ffn_call(xs, block_e, nb_used, w1, b1, w2, b2):
    n_rows = xs.shape[0]
    nb = n_rows // MOE_BLOCK

    def blk(i, be, nbu):
        return (jnp.minimum(i, nbu[0] - 1), 0)

    grid_spec = pltpu.PrefetchScalarGridSpec(
        num_scalar_prefetch=2,
        grid=(nb,),
        in_specs=[pl.BlockSpec((MOE_BLOCK, D_MODEL // 2), blk),
                  pl.BlockSpec((1, D_MODEL, 2 * D_FF), lambda i, be, nbu: (be[i], 0, 0)),
                  pl.BlockSpec((1, 1, 2 * D_FF), lambda i, be, nbu: (be[i], 0, 0)),
                  pl.BlockSpec((1, D_FF, D_MODEL), lambda i, be, nbu: (be[i], 0, 0)),
                  pl.BlockSpec((1, 1, D_MODEL), lambda i, be, nbu: (be[i], 0, 0))],
        out_specs=pl.BlockSpec((MOE_BLOCK, D_MODEL // 2), blk),
    )
    return pl.pallas_call(
        _moe_ffn_kernel,
        grid_spec=grid_spec,
        out_shape=jax.ShapeDtypeStruct((n_rows, D_MODEL // 2), U32),
        compiler_params=_cparams(("arbitrary",)),
        name="moe_ffn",
    )(block_e, nb_used, xs, w1, b1, w2, b2)


def _moe_combine_kernel(off8_ref, len8_ref, dst8_ref, tot8_ref, x_ref, dest_ref, gate_ref,
                        lg_ref, lb_ref, out_hbm, *refs, tiles_a):
    o_refs, (buf, sem) = refs[:-2], refs[-2:]
    s = pl.program_id(0)
    tm = x_ref.shape[0]
    slot = s % 2

    def gather(tile):
        which = tile % 2

        def per_expert(e, _):
            n = len8_ref[tile * N_EXPERTS + e]

            @pl.when(n > 0)
            def _():
                _groups_copy(out_hbm, buf.at[which], sem.at[which], dst8_ref[tile * N_EXPERTS + e],
                             off8_ref[tile * N_EXPERTS + e], n).start()

            return 0

        lax.fori_loop(0, N_EXPERTS, per_expert, 0)

    @pl.when(s == 0)
    def _():
        gather(s)

    @pl.when(s + 1 < pl.num_programs(0))
    def _():
        gather(s + 1)

    def clear(g, _):
        buf[slot, pl.ds(pl.multiple_of(g * MOE_GROUP, MOE_GROUP), MOE_GROUP), :] = jnp.zeros(
            (MOE_GROUP, D_MODEL // 2), U32)
        return 0

    lax.fori_loop(tot8_ref[s], MOE_TILE_ROWS // MOE_GROUP, clear, 0)

    dest = dest_ref[...]
    gate = gate_ref[...]
    r_iota = lax.broadcasted_iota(I32, (tm, MOE_TILE_ROWS), 1)
    weights = jnp.zeros((tm, MOE_TILE_ROWS), F32)
    for k in range(TOP_K):
        weights = jnp.where(r_iota == dest[:, k:k + 1], gate[:, k:k + 1], weights)

    _groups_copy(out_hbm, buf.at[slot], sem.at[slot], 0, 0, tot8_ref[s]).wait()
    hi, lo = _unpack_rows(buf[slot])
    wb = weights.astype(BF16)
    ff = jnp.concatenate([_dot(wb, hi), _dot(wb, lo)], axis=1)
    y = _layer_norm(DEEPNORM_ALPHA * x_ref[...] + ff, lg_ref[...], lb_ref[...])
    if tiles_a is None:
        o_refs[0][...] = y
    else:
        @pl.when(s < tiles_a)
        def _():
            o_refs[0][...] = y

        @pl.when(s >= tiles_a)
        def _():
            o_refs[1][...] = y


def _moe_combine_call(x2, dest, gate, out_rows, lg, lb, off8, len8, dst8, tot8, split_rows):
    t = x2.shape[0]
    tm = TOK_TILE
    if split_rows is None:
        tiles_a = None
        out_specs = pl.BlockSpec((tm, D_MODEL), lambda i, *_: (i, 0))
        out_shape = jax.ShapeDtypeStruct((t, D_MODEL), F32)
    else:
        tiles_a = split_rows // tm
        out_specs = [pl.BlockSpec((tm, D_MODEL), lambda i, *_: (jnp.minimum(i, tiles_a - 1), 0)),
                     pl.BlockSpec((tm, D_MODEL), lambda i, *_: (jnp.maximum(i - tiles_a, 0), 0))]
        out_shape = [jax.ShapeDtypeStruct((split_rows, D_MODEL), F32),
                     jax.ShapeDtypeStruct((t - split_rows, D_MODEL), F32)]
    grid_spec = pltpu.PrefetchScalarGridSpec(
        num_scalar_prefetch=4,
        grid=(t // tm,),
        in_specs=[pl.BlockSpec((tm, D_MODEL), lambda i, *_: (i, 0)),
                  pl.BlockSpec((tm, TOP_K), lambda i, *_: (i, 0)),
                  pl.BlockSpec((tm, TOP_K), lambda i, *_: (i, 0)),
                  pl.BlockSpec((1, D_MODEL), lambda i, *_: (0, 0)),
                  pl.BlockSpec((1, D_MODEL), lambda i, *_: (0, 0)),
                  pl.BlockSpec(memory_space=pl.ANY)],
        out_specs=out_specs,
        scratch_shapes=[pltpu.VMEM((2, MOE_TILE_ROWS, D_MODEL // 2), U32),
                        pltpu.SemaphoreType.DMA((2,))],
    )
    return pl.pallas_call(
        functools.partial(_moe_combine_kernel, tiles_a=tiles_a),
        grid_spec=grid_spec,
        out_shape=out_shape,
        compiler_params=_cparams(("arbitrary",)),
        name="moe_combine_ln3",
    )(off8, len8, dst8, tot8, x2, dest, gate, lg.reshape(1, -1), lb.reshape(1, -1), out_rows)


def _moe_tables(cnt, n_blocks):
    len8 = (cnt + MOE_GROUP - 1) // MOE_GROUP
    off8 = jnp.cumsum(len8, axis=1) - len8
    tot8 = jnp.sum(len8, axis=1)
    per_block = MOE_BLOCK // MOE_GROUP
    blocks_e = (jnp.sum(len8, axis=0) + per_block - 1) // per_block
    ends_e = jnp.cumsum(blocks_e)
    start8_e = (ends_e - blocks_e) * per_block
    dst8 = start8_e[None, :] + jnp.cumsum(len8, axis=0) - len8
    nb_used = ends_e[-1:]
    block_e = jnp.minimum(jnp.sum(ends_e[None, :] <= jnp.arange(n_blocks, dtype=I32)[:, None], axis=1),
                          N_EXPERTS - 1).astype(I32)
    return (off8.reshape(-1).astype(I32), len8.reshape(-1).astype(I32),
            dst8.reshape(-1).astype(I32), tot8.astype(I32), block_e, nb_used.astype(I32))


def _moe(x2, idx, gate, cnt, w1, b1, w2, b2, lg, lb, split_rows=None):
    t = x2.shape[0]
    tiles = t // TOK_TILE
    max_rows = t * TOP_K + tiles * N_EXPERTS * (MOE_GROUP - 1)
    n_blocks = -(-max_rows // MOE_BLOCK) + N_EXPERTS
    off8, len8, dst8, tot8, block_e, nb_used = _moe_tables(cnt, n_blocks)
    offv = jnp.broadcast_to((off8.reshape(tiles, N_EXPERTS, 1) * MOE_GROUP).astype(F32),
                            (tiles, N_EXPERTS, LANES))
    dest_t, xs = _moe_sort_call(x2, idx.T, offv, off8, len8, dst8, n_blocks * MOE_BLOCK)
    out_rows = _moe_ffn_call(xs, block_e, nb_used, w1, b1, w2, b2)
    return _moe_combine_call(x2, dest_t.T, gate, out_rows, lg, lb, off8, len8, dst8, tot8,
                             split_rows)


def _rope_tables(seq):
    rows = seq // GRID_W
    row = jnp.repeat(jnp.arange(rows, dtype=F32), GRID_W)
    col = jnp.tile(jnp.arange(GRID_W, dtype=F32), rows)
    inv_freq = ROPE_THETA ** (-jnp.arange(ROPE_AXIS_HALF, dtype=F32) / ROPE_AXIS_HALF)
    ar = row[:, None] * inv_freq
    ac = col[:, None] * inv_freq
    cos_t = jnp.concatenate([jnp.cos(ar), jnp.cos(ar), jnp.cos(ac), jnp.cos(ac)], axis=1)
    sin_t = jnp.concatenate([-jnp.sin(ar), jnp.sin(ar), -jnp.sin(ac), jnp.sin(ac)], axis=1)
    return cos_t, sin_t


def _encode(xa, xb, mem, p):
    seq = xa.shape[1]
    b = xa.shape[0] + xb.shape[0]
    t = b * seq
    ta = xa.shape[0] * seq
    cos_t, sin_t = _rope_tables(seq)
    x = _ln_call(xa.reshape(ta, D_MODEL), xb.reshape(t - ta, D_MODEL), p["ln_in_g"], p["ln_in_b"])
    c0 = D_ATTN_Q + 2 * D_ATTN_KV
    c1 = c0 + 3 * D_MODEL
    c2 = c1 + 2 * D_MODEL
    for l in range(DEPTH):
        w_in = p["w_in"][l].astype(BF16)
        qg, kg = p["q_norm_g"][l], p["k_norm_g"][l]
        bound = (ATTN_BOUND_MARGIN * LOG2E * HEAD_DIM ** 0.5
                 * jnp.max(jnp.abs(qg)) * jnp.max(jnp.abs(kg)))
        use_shift = bound <= ATTN_SHIFT_LIMIT
        q, k, vt = _qkv_call(x, w_in[:, :c0], cos_t, sin_t, qg, kg,
                             jnp.where(use_shift, bound, 0.0), seq)
        cb, pc = _convproj_call(x, w_in[:, c0:c1])
        gy, rx = _rnnproj_call(x, w_in[:, c1:c2])
        g = _gateproj_call(x, w_in[:, c2:], p["b_gate"][l].reshape(-1))
        ya = lax.cond(use_shift,
                      functools.partial(_attn_call, shifted=True),
                      functools.partial(_attn_call, shifted=False),
                      q.reshape(b, seq, -1), k.reshape(b, seq, -1), vt)
        h = _lru_call(rx.reshape(b, seq, -1), p["lru_conv_w"][l], p["lru_conv_b"][l],
                      (0.5 * p["lru_wa"][l]).astype(BF16), (0.5 * p["lru_wx"][l]).astype(BF16),
                      0.5 * p["lru_ba"][l], 0.5 * p["lru_bx"][l], p["lru_lam"][l])
        x = _merge_call(x, cb, pc, ya.reshape(t, -1), h.reshape(t, -1), gy, g, p["conv_w"][l],
                        p["w_mix_out"][l].astype(BF16), p["ln1_g"][l], p["ln1_b"][l], seq)
        mk, mv = _kv_call(mem, p["xkv_w"][l].astype(BF16))
        rw = jnp.pad(p["router_w"][l], ((0, 0), (0, LANES - N_EXPERTS))).astype(BF16)
        rb = jnp.pad(p["router_b"][l], (0, LANES - N_EXPERTS), constant_values=NEG_BIG)
        x2, idx, gate, cnt = _cross_call(x, mk, mv, p["xq_w"][l].astype(BF16),
                                         p["xo_w"][l].astype(BF16), p["ln2_g"][l], p["ln2_b"][l],
                                         rw, rb.reshape(1, -1), seq)
        x = _moe(x2, idx[:, :TOP_K], gate[:, :TOP_K], cnt[:, 0, :N_EXPERTS],
                 p["w1"][l].astype(BF16), p["b1"][l][:, None, :], p["w2"][l].astype(BF16),
                 p["b2"][l][:, None, :], p["ln3_g"][l], p["ln3_b"][l],
                 split_rows=ta if l == DEPTH - 1 else None)
    ya, yb = x
    return ya.reshape(xa.shape), yb.reshape(xb.shape)


def kernel(x_prompt, x_sample, mem_prompt, mem_sample, ln_in_g, ln_in_b, w_in, b_gate, q_norm_g, k_norm_g, conv_w, lru_conv_w, lru_conv_b, lru_wa, lru_ba, lru_wx, lru_bx, lru_lam, w_mix_out, ln1_g, ln1_b, xq_w, xkv_w, xo_w, ln2_g, ln2_b, router_w, router_b, w1, b1, w2, b2, ln3_g, ln3_b):
    params = dict(ln_in_g=ln_in_g, ln_in_b=ln_in_b, w_in=w_in, b_gate=b_gate, q_norm_g=q_norm_g,
                  k_norm_g=k_norm_g, conv_w=conv_w, lru_conv_w=lru_conv_w, lru_conv_b=lru_conv_b,
                  lru_wa=lru_wa, lru_ba=lru_ba, lru_wx=lru_wx, lru_bx=lru_bx, lru_lam=lru_lam,
                  w_mix_out=w_mix_out, ln1_g=ln1_g, ln1_b=ln1_b, xq_w=xq_w, xkv_w=xkv_w, xo_w=xo_w,
                  ln2_g=ln2_g, ln2_b=ln2_b, router_w=router_w, router_b=router_b, w1=w1, b1=b1,
                  w2=w2, b2=b2, ln3_g=ln3_g, ln3_b=ln3_b)
    assert x_prompt.shape[1:] == x_sample.shape[1:]
    return _encode(x_prompt, x_sample, jnp.concatenate([mem_prompt, mem_sample], axis=0), params)
```

```python
import functools

import jax
import jax.numpy as jnp
from jax import lax
from jax.experimental import pallas as pl
from jax.experimental.pallas import tpu as pltpu

F32 = jnp.float32
BF16 = jnp.bfloat16
I32 = jnp.int32
PACKED = jnp.uint32

D_MODEL = 1024
DEPTH = 2
N_HEADS = 8
N_KV_HEADS = 2
HEAD_DIM = 128
Q_GROUPS = N_HEADS // N_KV_HEADS
D_ATTN_Q = N_HEADS * HEAD_DIM
D_ATTN_KV = N_KV_HEADS * HEAD_DIM
GRID_W = 64
ROPE_AXIS_HALF = HEAD_DIM // 4
ROPE_THETA = 10000.0
QK_EPS = 1e-6
LRU_BLOCKS = 8
LRU_BLOCK_W = D_MODEL // LRU_BLOCKS
LRU_C = 8.0
MEM_HEADS = 4
MEM_HEAD_DIM = D_MODEL // MEM_HEADS
N_EXPERTS = 32
TOP_K = 4
D_FF = D_MODEL // 2
SWIGLU_LIMIT = 7.0
SWIGLU_ALPHA = 1.702
LN_EPS = 1e-5
DEEPNORM_ALPHA = (2 * DEPTH) ** 0.25

LANES = 128
SUBLANES = 8
BF16_SUBLANES = 16
VMEM_LIMIT = 56 * 1024 * 1024

PROJ_TM = 512
ATTN_TQ = 256
ATTN_TK = 512
ATTN_PAIRS_PER_BODY = 8
TOK_TILE = 256
CROSS_TM = 512
LRU_CHUNK = 512
LRU_SUBSEQ = 16
LRU_UNROLL = 8
MOE_GROUP = SUBLANES
MOE_BLOCK = 512
MOE_HALF = 256
MOE_TILE_ROWS = TOK_TILE * TOP_K + N_EXPERTS * MOE_GROUP
HEAD_AUG = 2 * HEAD_DIM
ATTN_SHIFT_LIMIT = 60.0
ATTN_BOUND_MARGIN = 1.02
NEG_BIG = -1e30
LOG2E = 1.4426950408889634


def _cparams(sem, vmem=VMEM_LIMIT):
    return pltpu.CompilerParams(dimension_semantics=sem, vmem_limit_bytes=vmem)


def _layer_norm(x, g, b):
    mu = jnp.mean(x, axis=-1, keepdims=True)
    xc = x - mu
    var = jnp.mean(xc * xc, axis=-1, keepdims=True)
    return xc * lax.rsqrt(var + LN_EPS) * g + b


def _sigmoid(x):
    return 1.0 / (1.0 + jnp.exp(-x))


def _sigmoid_tanh(x):
    return 0.5 * jnp.tanh(0.5 * x) + 0.5


def _dot(a, b):
    return jnp.dot(a, b, preferred_element_type=F32)


def _dot_nt(a, b):
    return lax.dot_general(a, b, (((1,), (1,)), ((), ())), preferred_element_type=F32)


def _ln_kernel(xa_ref, xb_ref, g_ref, b_ref, o_ref, *, tiles_a):
    @pl.when(pl.program_id(0) < tiles_a)
    def _():
        o_ref[...] = _layer_norm(xa_ref[...], g_ref[...], b_ref[...])

    @pl.when(pl.program_id(0) >= tiles_a)
    def _():
        o_ref[...] = _layer_norm(xb_ref[...], g_ref[...], b_ref[...])


def _ln_call(xa, xb, g, b):
    tm = PROJ_TM
    ta, tb = xa.shape[0] // tm, xb.shape[0] // tm
    return pl.pallas_call(
        functools.partial(_ln_kernel, tiles_a=ta),
        grid=(ta + tb,),
        in_specs=[pl.BlockSpec((tm, D_MODEL), lambda i: (jnp.minimum(i, ta - 1), 0)),
                  pl.BlockSpec((tm, D_MODEL), lambda i: (jnp.maximum(i - ta, 0), 0)),
                  pl.BlockSpec((1, D_MODEL), lambda i: (0, 0)),
                  pl.BlockSpec((1, D_MODEL), lambda i: (0, 0))],
        out_specs=pl.BlockSpec((tm, D_MODEL), lambda i: (i, 0)),
        out_shape=jax.ShapeDtypeStruct(((ta + tb) * tm, D_MODEL), F32),
        compiler_params=_cparams(("arbitrary",)),
        name="ln_in",
    )(xa, xb, g.reshape(1, -1), b.reshape(1, -1))


def _rope_partner_index():
    lane = jnp.arange(HEAD_DIM)
    return jnp.where((lane % (2 * ROPE_AXIS_HALF)) < ROPE_AXIS_HALF,
                     lane + ROPE_AXIS_HALF, lane - ROPE_AXIS_HALF)


def _qkv_kernel(x_ref, w_ref, cos_ref, sin_ref, qg_ref, kg_ref, shift_ref, q_ref, k_ref, vt_ref):
    x = x_ref[...].astype(BF16)
    c = cos_ref[...]
    s = sin_ref[...]
    lane = lax.broadcasted_iota(I32, c.shape, 1)
    n_qk = D_ATTN_Q + D_ATTN_KV
    z = _dot(x, w_ref[...])

    def prep(col, g_ref, scale):
        zh = z[:, col:col + HEAD_DIM]
        zp = z[:, n_qk + D_ATTN_KV + col:n_qk + D_ATTN_KV + col + HEAD_DIM]
        r = lax.rsqrt(jnp.mean(zh * zh, axis=-1, keepdims=True) + QK_EPS)
        return ((zh * (r * g_ref[0:1, :])) * c + (zp * (r * g_ref[1:2, :])) * s) * scale

    q_extra = jnp.broadcast_to(shift_ref[...], c.shape).astype(q_ref.dtype)
    k_extra = jnp.where(lane == 0, 1.0, 0.0).astype(k_ref.dtype)
    for h in range(N_HEADS):
        q_ref[:, h * HEAD_AUG:h * HEAD_AUG + HEAD_DIM] = prep(
            h * HEAD_DIM, qg_ref, LOG2E * HEAD_DIM ** -0.5).astype(q_ref.dtype)
        q_ref[:, h * HEAD_AUG + HEAD_DIM:(h + 1) * HEAD_AUG] = q_extra
    for h in range(N_KV_HEADS):
        k_ref[:, h * HEAD_AUG:h * HEAD_AUG + HEAD_DIM] = prep(
            D_ATTN_Q + h * HEAD_DIM, kg_ref, 1.0).astype(k_ref.dtype)
        k_ref[:, h * HEAD_AUG + HEAD_DIM:(h + 1) * HEAD_AUG] = k_extra
    vt_ref[0] = z[:, n_qk:n_qk + D_ATTN_KV].T.astype(vt_ref.dtype)


def _qkv_call(x, w, cos_t, sin_t, qg, kg, shift, seq):
    t = x.shape[0]
    tm = PROJ_TM
    nseq = seq // tm
    shift_row = jnp.zeros((1, HEAD_DIM), F32).at[0, 0].set(-shift)
    partner = _rope_partner_index()
    n_qk = D_ATTN_Q + D_ATTN_KV
    cols = (jnp.arange(n_qk) // HEAD_DIM) * HEAD_DIM + partner[jnp.arange(n_qk) % HEAD_DIM]
    w = jnp.concatenate([w, w[:, cols]], axis=1)
    n = w.shape[1]
    qg = jnp.stack([qg, qg[partner]])
    kg = jnp.stack([kg, kg[partner]])
    vec_spec = pl.BlockSpec((1, HEAD_DIM), lambda i: (0, 0))
    gain_spec = pl.BlockSpec((2, HEAD_DIM), lambda i: (0, 0))
    return pl.pallas_call(
        _qkv_kernel,
        grid=(t // tm,),
        in_specs=[pl.BlockSpec((tm, D_MODEL), lambda i: (i, 0)),
                  pl.BlockSpec((D_MODEL, n), lambda i: (0, 0)),
                  pl.BlockSpec((tm, HEAD_DIM), lambda i: (i % nseq, 0)),
                  pl.BlockSpec((tm, HEAD_DIM), lambda i: (i % nseq, 0)),
                  gain_spec, gain_spec, vec_spec],
        out_specs=[pl.BlockSpec((tm, N_HEADS * HEAD_AUG), lambda i: (i, 0)),
                   pl.BlockSpec((tm, N_KV_HEADS * HEAD_AUG), lambda i: (i, 0)),
                   pl.BlockSpec((1, D_ATTN_KV, tm), lambda i: (i // nseq, 0, i % nseq))],
        out_shape=[jax.ShapeDtypeStruct((t, N_HEADS * HEAD_AUG), BF16),
                   jax.ShapeDtypeStruct((t, N_KV_HEADS * HEAD_AUG), BF16),
                   jax.ShapeDtypeStruct((t // seq, D_ATTN_KV, seq), BF16)],
        compiler_params=_cparams(("parallel",)),
        name="qkv_proj",
    )(x, w, cos_t, sin_t, qg, kg, shift_row)


def _convproj_kernel(x_ref, w_ref, cb_ref, p_ref):
    x = x_ref[...].astype(BF16)
    z = _dot(x, w_ref[...])
    cb_ref[...] = z[:, :D_MODEL].astype(cb_ref.dtype)
    p_ref[...] = (z[:, D_MODEL:2 * D_MODEL] * z[:, 2 * D_MODEL:]).astype(p_ref.dtype)


def _convproj_call(x, w):
    t = x.shape[0]
    tm = PROJ_TM
    return pl.pallas_call(
        _convproj_kernel,
        grid=(t // tm,),
        in_specs=[pl.BlockSpec((tm, D_MODEL), lambda i: (i, 0)),
                  pl.BlockSpec((D_MODEL, 3 * D_MODEL), lambda i: (0, 0))],
        out_specs=[pl.BlockSpec((tm, D_MODEL), lambda i: (i, 0)),
                   pl.BlockSpec((tm, D_MODEL), lambda i: (i, 0))],
        out_shape=[jax.ShapeDtypeStruct((t, D_MODEL), BF16),
                   jax.ShapeDtypeStruct((t, D_MODEL), BF16)],
        compiler_params=_cparams(("parallel",)),
        name="conv_proj",
    )(x, w)


def _rnnproj_kernel(x_ref, w_ref, gy_ref, rx_ref):
    x = x_ref[...].astype(BF16)
    z = _dot(x, w_ref[...])
    gy_ref[...] = jax.nn.gelu(z[:, :D_MODEL]).astype(gy_ref.dtype)
    rx_ref[...] = z[:, D_MODEL:]


def _rnnproj_call(x, w):
    t = x.shape[0]
    tm = PROJ_TM
    return pl.pallas_call(
        _rnnproj_kernel,
        grid=(t // tm,),
        in_specs=[pl.BlockSpec((tm, D_MODEL), lambda i: (i, 0)),
                  pl.BlockSpec((D_MODEL, 2 * D_MODEL), lambda i: (0, 0))],
        out_specs=[pl.BlockSpec((tm, D_MODEL), lambda i: (i, 0)),
                   pl.BlockSpec((tm, D_MODEL), lambda i: (i, 0))],
        out_shape=[jax.ShapeDtypeStruct((t, D_MODEL), BF16),
                   jax.ShapeDtypeStruct((t, D_MODEL), F32)],
        compiler_params=_cparams(("parallel",)),
        name="rnn_proj",
    )(x, w)


def _gateproj_kernel(x_ref, w_ref, b_ref, g_ref):
    x = x_ref[...].astype(BF16)
    z = _dot(x, w_ref[...]) + b_ref[...]
    g_ref[...] = _sigmoid(z).astype(g_ref.dtype)


def _gateproj_call(x, w, b):
    t = x.shape[0]
    tm = PROJ_TM
    return pl.pallas_call(
        _gateproj_kernel,
        grid=(t // tm,),
        in_specs=[pl.BlockSpec((tm, D_MODEL), lambda i: (i, 0)),
                  pl.BlockSpec((D_MODEL, 3 * D_MODEL), lambda i: (0, 0)),
                  pl.BlockSpec((1, 3 * D_MODEL), lambda i: (0, 0))],
        out_specs=pl.BlockSpec((tm, 3 * D_MODEL), lambda i: (i, 0)),
        out_shape=jax.ShapeDtypeStruct((t, 3 * D_MODEL), BF16),
        compiler_params=_cparams(("parallel",)),
        name="gate_proj",
    )(x, w, b.reshape(1, -1))


def _attn_kernel(q_ref, k_ref, vt_ref, o_ref, acc_s, s0_s, s1_s, p0_s, p1_s, *, tk):
    tq = q_ref.shape[1]
    n = Q_GROUPS * tq
    nk = k_ref.shape[1] // tk
    qs = _stack_heads(q_ref)
    acc_s[...] = jnp.zeros(acc_s.shape, F32)
    s_bufs = (s0_s, s1_s)
    p_bufs = (p0_s, p1_s)

    def scores(j, slot):
        start = pl.multiple_of(j * tk, tk)
        s_bufs[slot][...] = _dot_nt(k_ref[0, pl.ds(start, tk), :], qs)

    def softmax(slot, m, l):
        st = s_bufs[slot][...]
        m_new = jnp.maximum(m, jnp.max(st, axis=0, keepdims=True))
        alpha = jnp.exp2(m - m_new)
        p = jnp.exp2(st - m_new)
        p_bufs[slot][...] = p.astype(BF16)
        return m_new, alpha * l + jnp.sum(p, axis=0, keepdims=True), alpha

    def accumulate(j, slot, alpha):
        start = pl.multiple_of(j * tk, tk)
        acc_s[...] = alpha * acc_s[...] + _dot(vt_ref[0, :, pl.ds(start, tk)], p_bufs[slot][...])

    def stage(j, m, l, alphas, first, last):
        new_alphas = []
        for slot in range(2):
            if not first:
                accumulate(j - 2 + slot, slot, alphas[slot])
            m, l, a = softmax(slot, m, l)
            new_alphas.append(a)
            if not last:
                scores(j + 2 + slot, slot)
        return m, l, tuple(new_alphas)

    m = jnp.full((1, n), NEG_BIG, F32)
    l = jnp.zeros((1, n), F32)
    scores(0, 0)
    scores(1, 1)
    if nk == 2:
        m, l, alphas = stage(0, m, l, None, True, True)
    else:
        m, l, alphas = stage(0, m, l, None, True, False)
        m, l, alphas = lax.fori_loop(
            1, nk // 2 - 1, lambda i, c: stage(2 * i, c[0], c[1], c[2], False, False), (m, l, alphas))
        m, l, alphas = stage(nk - 2, m, l, alphas, False, True)
    accumulate(nk - 2, 0, alphas[0])
    accumulate(nk - 1, 1, alphas[1])
    _write_heads(o_ref, acc_s[...] / l)


def _stack_heads(q_ref):
    return jnp.concatenate([q_ref[0, :, g * HEAD_AUG:(g + 1) * HEAD_AUG] for g in range(Q_GROUPS)],
                           axis=0)


def _write_heads(o_ref, ot):
    tq = o_ref.shape[1]
    for g in range(Q_GROUPS):
        o_ref[0, :, g * HEAD_DIM:(g + 1) * HEAD_DIM] = ot[:, g * tq:(g + 1) * tq].T.astype(o_ref.dtype)


def _attn_shifted_kernel(q_ref, k_ref, vt_ref, o_ref, acc_s, p0_s, p1_s, *, tk):
    n = Q_GROUPS * q_ref.shape[1]
    nk = k_ref.shape[1] // tk
    qs = _stack_heads(q_ref)
    acc_s[...] = jnp.zeros(acc_s.shape, F32)
    p_bufs = (p0_s, p1_s)

    def probs(j, slot, l):
        start = pl.multiple_of(j * tk, tk)
        p = jnp.exp2(_dot_nt(k_ref[0, pl.ds(start, tk), :], qs))
        p_bufs[slot][...] = p.astype(BF16)
        return l + jnp.sum(p, axis=0, keepdims=True)

    def accumulate(j, slot):
        start = pl.multiple_of(j * tk, tk)
        acc_s[...] += _dot(vt_ref[0, :, pl.ds(start, tk)], p_bufs[slot][...])

    def pair(i, l):
        j = 2 * i
        l = probs(j + 1, 1, l)
        accumulate(j, 0)
        l = probs(j + 2, 0, l)
        accumulate(j + 1, 1)
        return l

    l = probs(0, 0, jnp.zeros((1, n), F32))
    n_pairs = nk // 2 - 1
    trips = n_pairs // ATTN_PAIRS_PER_BODY

    def body(i, l):
        for r in range(ATTN_PAIRS_PER_BODY):
            l = pair(i * ATTN_PAIRS_PER_BODY + r, l)
        return l

    if trips > 1:
        l = lax.fori_loop(0, trips, body, l)
    else:
        trips = 0
    for i in range(trips * ATTN_PAIRS_PER_BODY, n_pairs):
        l = pair(i, l)
    l = probs(nk - 1, 1, l)
    accumulate(nk - 2, 0)
    accumulate(nk - 1, 1)
    _write_heads(o_ref, acc_s[...] / l)


def _attn_call(q, k, vt, shifted):
    b, s, _ = q.shape
    tq = min(ATTN_TQ, s)
    tk = min(ATTN_TK, s // 2)
    assert (s // tk) % 2 == 0
    n = Q_GROUPS * tq
    p_bufs = [pltpu.VMEM((tk, n), BF16)] * 2
    s_bufs = [] if shifted else [pltpu.VMEM((tk, n), F32)] * 2
    return pl.pallas_call(
        functools.partial(_attn_shifted_kernel if shifted else _attn_kernel, tk=tk),
        grid=(b, N_KV_HEADS, s // tq),
        in_specs=[pl.BlockSpec((1, tq, Q_GROUPS * HEAD_AUG), lambda bi, h, i: (bi, i, h)),
                  pl.BlockSpec((1, s, HEAD_AUG), lambda bi, h, i: (bi, 0, h)),
                  pl.BlockSpec((1, HEAD_DIM, s), lambda bi, h, i: (bi, h, 0))],
        out_specs=pl.BlockSpec((1, tq, Q_GROUPS * HEAD_DIM), lambda bi, h, i: (bi, i, h)),
        out_shape=jax.ShapeDtypeStruct((b, s, D_ATTN_Q), BF16),
        scratch_shapes=[pltpu.VMEM((HEAD_DIM, n), F32)] + s_bufs + p_bufs,
        compiler_params=_cparams(("parallel", "parallel", "parallel")),
        name="flash_attn_shifted" if shifted else "flash_attn",
    )(q, k, vt)


def _lru_kernel(x_ref, cw_ref, cb_ref, wa_ref, wx_ref, ba_ref, bx_ref, lam_ref, o_hbm,
                xp_s, a0_s, u0_s, a1_s, u1_s, h_s, sem, *, seq, chunk):
    sub = seq // LRU_SUBSEQ
    cps = sub // chunk
    nchunks = seq // chunk

    zeros8 = jnp.zeros((SUBLANES, LANES), F32)
    xp_s[0:SUBLANES, :] = zeros8
    xp_s[seq + SUBLANES:seq + 2 * SUBLANES, :] = zeros8
    xp_s[SUBLANES:seq + SUBLANES, :] = x_ref[0]
    cw = cw_ref[...]
    cb = cb_ref[...]
    a_bufs = (a0_s, a1_s)
    u_bufs = (u0_s, u1_s)

    neg = -lam_ref[...]
    softplus = jnp.maximum(neg, 0.0) + jnp.log1p(jnp.exp(-jnp.abs(neg)))
    half_decay = -0.5 * LRU_C * softplus

    def gate_chunk(c, _):
        base = pl.multiple_of(c * chunk, chunk)
        xc = cb
        for k in range(4):
            xc = xc + cw[k:k + 1, :] * xp_s[pl.ds(base + SUBLANES - 1 + k, chunk), :]
        xb = xc.astype(BF16)
        xh = 0.5 * xc
        dst = pl.ds((c % cps) * (chunk * LRU_SUBSEQ) + c // cps, chunk, stride=LRU_SUBSEQ)
        for d in range(2):
            tr = jnp.tanh(_dot(xb, wa_ref[d, 0]) + ba_ref[d:d + 1, :])
            ti = jnp.tanh(_dot(xb, wx_ref[d, 0]) + bx_ref[d:d + 1, :])
            a = jnp.exp(half_decay[d:d + 1, :] * tr + half_decay[d:d + 1, :])
            u_bufs[d][dst, :] = jnp.sqrt(1.0 - a * a) * (xh * (ti + 1.0))
            a_bufs[d][dst, :] = a
        return 0

    lax.fori_loop(0, nchunks, gate_chunk, 0)

    def step(tt, carry):
        out = []
        for d in range(2):
            h, p = carry[d]
            t = tt if d == 0 else sub - 1 - tt
            idx = pl.ds(pl.multiple_of(t * LRU_SUBSEQ, LRU_SUBSEQ), LRU_SUBSEQ)
            av = a_bufs[d][idx, :]
            h = av * h + u_bufs[d][idx, :]
            p = av * p
            u_bufs[d][idx, :] = h
            a_bufs[d][idx, :] = p
            out.append((h, p))
        return tuple(out)

    h0 = jnp.zeros((LRU_SUBSEQ, LANES), F32)
    p0 = jnp.ones((LRU_SUBSEQ, LANES), F32)
    ends = lax.fori_loop(0, sub, step, ((h0, p0), (h0, p0)), unroll=LRU_UNROLL)

    carry_in = []
    for d in range(2):
        h_end, p_end = ends[d]
        order = range(LRU_SUBSEQ) if d == 0 else range(LRU_SUBSEQ - 1, -1, -1)
        states = [None] * LRU_SUBSEQ
        state = jnp.zeros((1, LANES), F32)
        for j in order:
            states[j] = state
            state = p_end[j:j + 1, :] * state + h_end[j:j + 1, :]
        carry_in.append(jnp.concatenate(states, axis=0))

    step_id = pl.program_id(0) * pl.num_programs(1) + pl.program_id(1)
    last_id = pl.num_programs(0) * pl.num_programs(1) - 1

    def out_copies(bi, ci):
        return [pltpu.make_async_copy(
            h_s.at[:, j, :],
            o_hbm.at[bi, pl.ds(j * sub, sub), pl.ds(pl.multiple_of(ci * LANES, LANES), LANES)],
            sem) for j in range(LRU_SUBSEQ)]

    @pl.when(step_id > 0)
    def _():
        for cp in out_copies(0, 0):
            cp.wait()

    steps = min(sub, chunk)
    for c in range(sub // steps):
        rows = slice(c * steps * LRU_SUBSEQ, (c + 1) * steps * LRU_SUBSEQ)
        c0 = jnp.tile(carry_in[0], (steps, 1))
        c1 = jnp.tile(carry_in[1], (steps, 1))
        val = u0_s[rows, :] + a0_s[rows, :] * c0 + u1_s[rows, :] + a1_s[rows, :] * c1
        h_s[c * steps:(c + 1) * steps] = val.reshape(steps, LRU_SUBSEQ, LANES)

    for cp in out_copies(pl.program_id(0), pl.program_id(1)):
        cp.start()

    @pl.when(step_id == last_id)
    def _():
        for cp in out_copies(0, 0):
            cp.wait()


def _lru_call(rx, cw, cb, wa, wx, ba, bx, lam):
    b, s, _ = rx.shape
    chunk = min(LRU_CHUNK, s // LRU_SUBSEQ)
    assert LRU_SUBSEQ % SUBLANES == 0
    w = LRU_BLOCK_W
    return pl.pallas_call(
        functools.partial(_lru_kernel, seq=s, chunk=chunk),
        grid=(b, LRU_BLOCKS),
        in_specs=[pl.BlockSpec((1, s, w), lambda bi, c: (bi, 0, c)),
                  pl.BlockSpec((4, w), lambda bi, c: (0, c)),
                  pl.BlockSpec((1, w), lambda bi, c: (0, c)),
                  pl.BlockSpec((2, 1, w, w), lambda bi, c: (0, c, 0, 0)),
                  pl.BlockSpec((2, 1, w, w), lambda bi, c: (0, c, 0, 0)),
                  pl.BlockSpec((2, w), lambda bi, c: (0, c)),
                  pl.BlockSpec((2, w), lambda bi, c: (0, c)),
                  pl.BlockSpec((2, w), lambda bi, c: (0, c))],
        out_specs=pl.BlockSpec(memory_space=pl.ANY),
        out_shape=jax.ShapeDtypeStruct((b, s, D_MODEL), F32),
        scratch_shapes=[pltpu.VMEM((s + 2 * SUBLANES, w), F32)]
        + [pltpu.VMEM((s, w), F32)] * 4
        + [pltpu.VMEM((s // LRU_SUBSEQ, LRU_SUBSEQ, w), F32), pltpu.SemaphoreType.DMA(())],
        compiler_params=_cparams(("arbitrary", "arbitrary")),
        name="rg_lru",
    )(rx, cw, cb.reshape(1, -1), wa, wx, ba, bx, lam)


def _kv_kernel(m_ref, w_ref, k_ref, v_ref):
    z = _dot(m_ref[0].astype(BF16), w_ref[...])
    k_ref[0] = z[:, :D_MODEL].astype(k_ref.dtype)
    v_ref[0] = z[:, D_MODEL:].astype(v_ref.dtype)


def _kv_call(mem, w):
    b, m, _ = mem.shape
    return pl.pallas_call(
        _kv_kernel,
        grid=(b,),
        in_specs=[pl.BlockSpec((1, m, D_MODEL), lambda i: (i, 0, 0)),
                  pl.BlockSpec((D_MODEL, 2 * D_MODEL), lambda i: (0, 0))],
        out_specs=[pl.BlockSpec((1, m, D_MODEL), lambda i: (i, 0, 0)),
                   pl.BlockSpec((1, m, D_MODEL), lambda i: (i, 0, 0))],
        out_shape=[jax.ShapeDtypeStruct((b, m, D_MODEL), BF16),
                   jax.ShapeDtypeStruct((b, m, D_MODEL), BF16)],
        compiler_params=_cparams(("parallel",)),
        name="mem_kv_proj",
    )(mem, w)


def _mix_cross_kernel(x_ref, cb_ref, p_ref, pprev_ref, pnext_ref, ya_ref, h_ref, gy_ref, g_ref,
                      cw_ref, wm_ref, l1g_ref, l1b_ref, k_ref, v_ref, wq_ref, wo_ref, l2g_ref,
                      l2b_ref, rw_ref, rb_ref, x2_ref, idx_ref, gate_ref, cnt_ref, *, seq):
    tm = x_ref.shape[0]
    t0 = pl.program_id(0) * tm
    has_prev = ((t0 % seq) != 0).astype(F32)
    has_next = (((t0 + tm) % seq) != 0).astype(F32)
    cw = cw_ref[...]
    row = lax.broadcasted_iota(I32, (TOK_TILE, D_MODEL), 0)
    halves = tm // TOK_TILE
    for r in range(halves):
        rows = slice(r * TOK_TILE, (r + 1) * TOK_TILE)
        lo, hi = r * TOK_TILE, (r + 1) * TOK_TILE
        prev_row = (pprev_ref[BF16_SUBLANES - 1:BF16_SUBLANES, :].astype(F32) * has_prev if r == 0
                    else p_ref[lo - 1:lo, :].astype(F32))
        next_row = (pnext_ref[0:1, :].astype(F32) * has_next if r == halves - 1
                    else p_ref[hi:hi + 1, :].astype(F32))
        p = p_ref[rows, :].astype(F32)
        p_dn = jnp.where(row == 0, prev_row, pltpu.roll(p, 1, 0))
        p_up = jnp.where(row == TOK_TILE - 1, next_row, pltpu.roll(p, TOK_TILE - 1, 0))
        conv = cw[0:1, :] * p_dn + cw[1:2, :] * p + cw[2:3, :] * p_up
        y_conv = cb_ref[rows, :].astype(F32) * conv
        y_rnn = h_ref[rows, :] * gy_ref[rows, :].astype(F32)
        merged = (g_ref[rows, :D_MODEL].astype(F32) * y_conv
                  + g_ref[rows, D_MODEL:2 * D_MODEL].astype(F32) * ya_ref[rows, :].astype(F32)
                  + g_ref[rows, 2 * D_MODEL:].astype(F32) * y_rnn)
        mix = _dot(merged.astype(BF16), wm_ref[...])
        x1 = _layer_norm(DEEPNORM_ALPHA * x_ref[rows, :] + mix, l1g_ref[...], l1b_ref[...])
        _cross_rows(x1, k_ref, v_ref, wq_ref, wo_ref, l2g_ref, l2b_ref, rw_ref, rb_ref,
                    x2_ref.at[rows, :], idx_ref.at[rows, :], gate_ref.at[rows, :], cnt_ref.at[r])


def _cross_rows(x, k_ref, v_ref, wq_ref, wo_ref, lg_ref, lb_ref, rw_ref, rb_ref,
                x2_ref, idx_ref, gate_ref, cnt_ref):
    q = _dot(x.astype(BF16), wq_ref[...]) * (MEM_HEAD_DIM ** -0.5)
    outs = []
    for h in range(MEM_HEADS):
        sl = slice(h * MEM_HEAD_DIM, (h + 1) * MEM_HEAD_DIM)
        s = _dot_nt(q[:, sl].astype(BF16), k_ref[0, :, sl])
        m = jnp.max(s, axis=-1, keepdims=True)
        p = jnp.exp(s - m)
        p = p / jnp.sum(p, axis=-1, keepdims=True)
        outs.append(_dot(p.astype(BF16), v_ref[0, :, sl]))
    o = jnp.concatenate(outs, axis=1)
    cross = _dot(o.astype(BF16), wo_ref[...])
    x2 = _layer_norm(DEEPNORM_ALPHA * x + cross, lg_ref[...], lb_ref[...])
    x2_ref[...] = x2

    logits = _dot(x2.astype(BF16), rw_ref[...]) + rb_ref[...]
    lane = lax.broadcasted_iota(I32, logits.shape, 1)
    idx_acc = jnp.zeros(logits.shape, I32)
    val_acc = jnp.zeros(logits.shape, F32)
    member = jnp.zeros(logits.shape, F32)
    top0 = None
    for k in range(TOP_K):
        m = jnp.max(logits, axis=-1, keepdims=True)
        pick = jnp.min(jnp.where(logits == m, lane, LANES), axis=-1, keepdims=True)
        hit = lane == pick
        if k == 0:
            top0 = m
        idx_acc = jnp.where(lane == k, pick, idx_acc)
        val_acc = jnp.where(lane == k, jnp.exp(m - top0), val_acc)
        member = member + hit.astype(F32)
        logits = jnp.where(hit, NEG_BIG, logits)
    idx_ref[...] = idx_acc
    gate_ref[...] = val_acc / jnp.sum(val_acc, axis=-1, keepdims=True)
    cnt_ref[...] = jnp.sum(member, axis=0, keepdims=True).astype(I32)


def _mix_cross_call(x, cb, p, ya, h, gy, g, cw, wm, l1g, l1b, k, v, wq, wo, l2g, l2b, rw, rb, seq):
    t = x.shape[0]
    tm = CROSS_TM
    halves = tm // TOK_TILE
    per_seq = seq // tm
    m = k.shape[1]
    hb = BF16_SUBLANES
    per = tm // hb
    last = t // hb - 1
    row_spec = pl.BlockSpec((tm, D_MODEL), lambda i: (i, 0))
    vec_spec = pl.BlockSpec((1, D_MODEL), lambda i: (0, 0))
    mat_spec = pl.BlockSpec((D_MODEL, D_MODEL), lambda i: (0, 0))
    mem_spec = pl.BlockSpec((1, m, D_MODEL), lambda i: (i // per_seq, 0, 0))
    lane_spec = pl.BlockSpec((tm, LANES), lambda i: (i, 0))
    return pl.pallas_call(
        functools.partial(_mix_cross_kernel, seq=seq),
        grid=(t // tm,),
        in_specs=[row_spec, row_spec, row_spec,
                  pl.BlockSpec((hb, D_MODEL), lambda i: (jnp.maximum(i * per - 1, 0), 0)),
                  pl.BlockSpec((hb, D_MODEL), lambda i: (jnp.minimum((i + 1) * per, last), 0)),
                  row_spec, row_spec, row_spec,
                  pl.BlockSpec((tm, 3 * D_MODEL), lambda i: (i, 0)),
                  pl.BlockSpec((3, D_MODEL), lambda i: (0, 0)),
                  mat_spec, vec_spec, vec_spec,
                  mem_spec, mem_spec, mat_spec, mat_spec, vec_spec, vec_spec,
                  pl.BlockSpec((D_MODEL, LANES), lambda i: (0, 0)),
                  pl.BlockSpec((1, LANES), lambda i: (0, 0))],
        out_specs=[row_spec, lane_spec, lane_spec,
                   pl.BlockSpec((halves, 1, LANES), lambda i: (i, 0, 0))],
        out_shape=[jax.ShapeDtypeStruct((t, D_MODEL), F32),
                   jax.ShapeDtypeStruct((t, LANES), I32),
                   jax.ShapeDtypeStruct((t, LANES), F32),
                   jax.ShapeDtypeStruct((t // TOK_TILE, 1, LANES), I32)],
        compiler_params=_cparams(("parallel",)),
        name="mix_cross_router",
    )(x, cb, p, p, p, ya, h, gy, g, cw, wm, l1g.reshape(1, -1), l1b.reshape(1, -1), k, v, wq, wo,
      l2g.reshape(1, -1), l2b.reshape(1, -1), rw, rb)


def _pack_rows(v):
    bits = pltpu.bitcast(v, PACKED)
    half = v.shape[1] // 2
    return (bits[:, :half] & jnp.uint32(0xFFFF0000)) | (bits[:, half:] >> 16)


def _unpack_rows(u):
    hi = pltpu.bitcast(u & jnp.uint32(0xFFFF0000), F32).astype(BF16)
    lo = pltpu.bitcast(u << 16, F32).astype(BF16)
    return hi, lo


def _groups_copy(src, dst, sem, s_grp, d_grp, n_grp):
    rows = n_grp * MOE_GROUP
    return pltpu.make_async_copy(
        src.at[pl.ds(pl.multiple_of(s_grp * MOE_GROUP, MOE_GROUP), rows), :],
        dst.at[pl.ds(pl.multiple_of(d_grp * MOE_GROUP, MOE_GROUP), rows), :], sem)


def _moe_sort_kernel(off8_ref, len8_ref, dst8_ref, x_ref, idx_ref, offv_ref, dest_ref, xs_hbm,
                     xs_s, sem, pending_s):
    s = pl.program_id(0)
    tm = x_ref.shape[0]
    idx = idx_ref[...]
    e_iota = lax.broadcasted_iota(I32, (N_EXPERTS, tm), 0)
    member = jnp.zeros((N_EXPERTS, tm), F32)
    for k in range(TOP_K):
        member = member + (e_iota == idx[k:k + 1, :]).astype(F32)
    earlier = (lax.broadcasted_iota(I32, (tm, tm), 0)
               < lax.broadcasted_iota(I32, (tm, tm), 1)).astype(BF16)
    rank = _dot(member.astype(BF16), earlier)
    pos = rank + offv_ref[0][:, 0:1]
    r_iota = lax.broadcasted_iota(I32, (MOE_TILE_ROWS, tm), 0)
    onehot = jnp.zeros((MOE_TILE_ROWS, tm), F32)
    dests = []
    for k in range(TOP_K):
        dk = jnp.sum(jnp.where(e_iota == idx[k:k + 1, :], pos, 0.0), axis=0, keepdims=True)
        dk = dk.astype(I32)
        dests.append(dk)
        onehot = jnp.where(r_iota == dk, 1.0, onehot)
    dest_ref[...] = jnp.concatenate(dests, axis=0)
    slot = s % 2
    xs_s[slot] = _pack_rows(_dot(onehot.astype(BF16), x_ref[...].astype(BF16)))

    def per_expert(e, total):
        n = len8_ref[s * N_EXPERTS + e]

        @pl.when(n > 0)
        def _():
            _groups_copy(xs_s.at[slot], xs_hbm, sem.at[slot], off8_ref[s * N_EXPERTS + e],
                         dst8_ref[s * N_EXPERTS + e], n).start()

        return total + n

    total = lax.fori_loop(0, N_EXPERTS, per_expert, 0)

    def drain(which, count):
        @pl.when(count > 0)
        def _():
            _groups_copy(xs_s.at[which], xs_hbm, sem.at[which], 0, 0, count).wait()

    @pl.when(s > 0)
    def _():
        drain(1 - slot, pending_s[0])

    pending_s[0] = total

    @pl.when(s == pl.num_programs(0) - 1)
    def _():
        drain(slot, total)


def _moe_sort_call(x2, idx_t, offv, off8, len8, dst8, n_rows):
    t = x2.shape[0]
    tm = TOK_TILE
    grid_spec = pltpu.PrefetchScalarGridSpec(
        num_scalar_prefetch=3,
        grid=(t // tm,),
        in_specs=[pl.BlockSpec((tm, D_MODEL), lambda i, *_: (i, 0)),
                  pl.BlockSpec((TOP_K, tm), lambda i, *_: (0, i)),
                  pl.BlockSpec((1, N_EXPERTS, LANES), lambda i, *_: (i, 0, 0))],
        out_specs=[pl.BlockSpec((TOP_K, tm), lambda i, *_: (0, i)),
                   pl.BlockSpec(memory_space=pl.ANY)],
        scratch_shapes=[pltpu.VMEM((2, MOE_TILE_ROWS, D_MODEL // 2), PACKED),
                        pltpu.SemaphoreType.DMA((2,)),
                        pltpu.SMEM((1,), I32)],
    )
    return pl.pallas_call(
        _moe_sort_kernel,
        grid_spec=grid_spec,
        out_shape=[jax.ShapeDtypeStruct((TOP_K, t), I32),
                   jax.ShapeDtypeStruct((n_rows, D_MODEL // 2), PACKED)],
        compiler_params=_cparams(("arbitrary",)),
        name="moe_sort",
    )(off8, len8, dst8, x2, idx_t, offv)


def _moe_ffn_kernel(be_ref, nb_ref, xs_ref, w1_ref, b1_ref, w2_ref, b2_ref, o_ref, w1_s, w2_s):
    i = pl.program_id(0)

    @pl.when((i == 0) | (be_ref[i] != be_ref[jnp.maximum(i - 1, 0)]))
    def _():
        w1_s[...] = w1_ref[0].astype(BF16)
        w2_s[...] = w2_ref[0].astype(BF16)

    @pl.when(i < nb_ref[0])
    def _():
        half = D_MODEL // 2
        for r in range(MOE_BLOCK // MOE_HALF):
            rows = slice(r * MOE_HALF, (r + 1) * MOE_HALF)
            hi, lo = _unpack_rows(xs_ref[rows, :])
            hcat = _dot(hi, w1_s[:half, :]) + _dot(lo, w1_s[half:, :]) + b1_ref[0]
            glu = jnp.minimum(hcat[:, :D_FF], SWIGLU_LIMIT)
            lin = jnp.clip(hcat[:, D_FF:], -SWIGLU_LIMIT, SWIGLU_LIMIT)
            act = glu * _sigmoid_tanh(SWIGLU_ALPHA * glu) * (lin + 1.0)
            out = _dot(act.astype(BF16), w2_s[...]) + b2_ref[0]
            o_ref[rows, :] = _pack_rows(out.astype(BF16).astype(F32))


def _moe_ffn_call(xs, block_e, nb_used, w1, b1, w2, b2):
    n_rows = xs.shape[0]
    nb = n_rows // MOE_BLOCK

    def blk(i, be, nbu):
        return (jnp.minimum(i, nbu[0] - 1), 0)

    grid_spec = pltpu.PrefetchScalarGridSpec(
        num_scalar_prefetch=2,
        grid=(nb,),
        in_specs=[pl.BlockSpec((MOE_BLOCK, D_MODEL // 2), blk),
                  pl.BlockSpec((1, D_MODEL, 2 * D_FF), lambda i, be, nbu: (be[i], 0, 0)),
                  pl.BlockSpec((1, 1, 2 * D_FF), lambda i, be, nbu: (be[i], 0, 0)),
                  pl.BlockSpec((1, D_FF, D_MODEL), lambda i, be, nbu: (be[i], 0, 0)),
                  pl.BlockSpec((1, 1, D_MODEL), lambda i, be, nbu: (be[i], 0, 0))],
        out_specs=pl.BlockSpec((MOE_BLOCK, D_MODEL // 2), blk),
        scratch_shapes=[pltpu.VMEM((D_MODEL, 2 * D_FF), BF16), pltpu.VMEM((D_FF, D_MODEL), BF16)],
    )
    return pl.pallas_call(
        _moe_ffn_kernel,
        grid_spec=grid_spec,
        out_shape=jax.ShapeDtypeStruct((n_rows, D_MODEL // 2), PACKED),
        compiler_params=_cparams(("arbitrary",)),
        name="moe_ffn",
    )(block_e, nb_used, xs, w1, b1, w2, b2)


def _moe_combine_kernel(off8_ref, len8_ref, dst8_ref, tot8_ref, x_ref, dest_ref, gate_ref,
                        lg_ref, lb_ref, out_hbm, *refs, tiles_a):
    o_refs, (buf, sem) = refs[:-2], refs[-2:]
    s = pl.program_id(0)
    tm = x_ref.shape[0]
    slot = s % 2

    def gather(tile):
        which = tile % 2

        def per_expert(e, _):
            n = len8_ref[tile * N_EXPERTS + e]

            @pl.when(n > 0)
            def _():
                _groups_copy(out_hbm, buf.at[which], sem.at[which], dst8_ref[tile * N_EXPERTS + e],
                             off8_ref[tile * N_EXPERTS + e], n).start()

            return 0

        lax.fori_loop(0, N_EXPERTS, per_expert, 0)

    @pl.when(s == 0)
    def _():
        gather(s)

    @pl.when(s + 1 < pl.num_programs(0))
    def _():
        gather(s + 1)

    def clear(g, _):
        buf[slot, pl.ds(pl.multiple_of(g * MOE_GROUP, MOE_GROUP), MOE_GROUP), :] = jnp.zeros(
            (MOE_GROUP, D_MODEL // 2), PACKED)
        return 0

    lax.fori_loop(tot8_ref[s], MOE_TILE_ROWS // MOE_GROUP, clear, 0)

    dest = dest_ref[...]
    gate = gate_ref[...]
    r_iota = lax.broadcasted_iota(I32, (tm, MOE_TILE_ROWS), 1)
    weights = jnp.zeros((tm, MOE_TILE_ROWS), F32)
    for k in range(TOP_K):
        weights = jnp.where(r_iota == dest[:, k:k + 1], gate[:, k:k + 1], weights)

    _groups_copy(out_hbm, buf.at[slot], sem.at[slot], 0, 0, tot8_ref[s]).wait()
    hi, lo = _unpack_rows(buf[slot])
    wb = weights.astype(BF16)
    ff = jnp.concatenate([_dot(wb, hi), _dot(wb, lo)], axis=1)
    y = _layer_norm(DEEPNORM_ALPHA * x_ref[...] + ff, lg_ref[...], lb_ref[...])
    if tiles_a is None:
        o_refs[0][...] = y
    else:
        @pl.when(s < tiles_a)
        def _():
            o_refs[0][...] = y

        @pl.when(s >= tiles_a)
        def _():
            o_refs[1][...] = y


def _moe_combine_call(x2, dest, gate, out_rows, lg, lb, off8, len8, dst8, tot8, split_rows):
    t = x2.shape[0]
    tm = TOK_TILE
    if split_rows is None:
        tiles_a = None
        out_specs = pl.BlockSpec((tm, D_MODEL), lambda i, *_: (i, 0))
        out_shape = jax.ShapeDtypeStruct((t, D_MODEL), F32)
    else:
        tiles_a = split_rows // tm
        out_specs = [pl.BlockSpec((tm, D_MODEL), lambda i, *_: (jnp.minimum(i, tiles_a - 1), 0)),
                     pl.BlockSpec((tm, D_MODEL), lambda i, *_: (jnp.maximum(i - tiles_a, 0), 0))]
        out_shape = [jax.ShapeDtypeStruct((split_rows, D_MODEL), F32),
                     jax.ShapeDtypeStruct((t - split_rows, D_MODEL), F32)]
    grid_spec = pltpu.PrefetchScalarGridSpec(
        num_scalar_prefetch=4,
        grid=(t // tm,),
        in_specs=[pl.BlockSpec((tm, D_MODEL), lambda i, *_: (i, 0)),
                  pl.BlockSpec((tm, TOP_K), lambda i, *_: (i, 0)),
                  pl.BlockSpec((tm, TOP_K), lambda i, *_: (i, 0)),
                  pl.BlockSpec((1, D_MODEL), lambda i, *_: (0, 0)),
                  pl.BlockSpec((1, D_MODEL), lambda i, *_: (0, 0)),
                  pl.BlockSpec(memory_space=pl.ANY)],
        out_specs=out_specs,
        scratch_shapes=[pltpu.VMEM((2, MOE_TILE_ROWS, D_MODEL // 2), PACKED),
                        pltpu.SemaphoreType.DMA((2,))],
    )
    return pl.pallas_call(
        functools.partial(_moe_combine_kernel, tiles_a=tiles_a),
        grid_spec=grid_spec,
        out_shape=out_shape,
        compiler_params=_cparams(("arbitrary",)),
        name="moe_combine_ln3",
    )(off8, len8, dst8, tot8, x2, dest, gate, lg.reshape(1, -1), lb.reshape(1, -1), out_rows)


def _moe_tables(cnt, n_blocks):
    len8 = (cnt + MOE_GROUP - 1) // MOE_GROUP
    off8 = jnp.cumsum(len8, axis=1) - len8
    tot8 = jnp.sum(len8, axis=1)
    per_block = MOE_BLOCK // MOE_GROUP
    blocks_e = (jnp.sum(len8, axis=0) + per_block - 1) // per_block
    ends_e = jnp.cumsum(blocks_e)
    start8_e = (ends_e - blocks_e) * per_block
    dst8 = start8_e[None, :] + jnp.cumsum(len8, axis=0) - len8
    nb_used = ends_e[-1:]
    block_e = jnp.minimum(jnp.sum(ends_e[None, :] <= jnp.arange(n_blocks, dtype=I32)[:, None], axis=1),
                          N_EXPERTS - 1).astype(I32)
    return (off8.reshape(-1).astype(I32), len8.reshape(-1).astype(I32),
            dst8.reshape(-1).astype(I32), tot8.astype(I32), block_e, nb_used.astype(I32))


def _moe(x2, idx, gate, cnt, w1, b1, w2, b2, lg, lb, split_rows=None):
    t = x2.shape[0]
    tiles = t // TOK_TILE
    max_rows = t * TOP_K + tiles * N_EXPERTS * (MOE_GROUP - 1)
    n_blocks = -(-max_rows // MOE_BLOCK) + N_EXPERTS
    off8, len8, dst8, tot8, block_e, nb_used = _moe_tables(cnt, n_blocks)
    offv = jnp.broadcast_to((off8.reshape(tiles, N_EXPERTS, 1) * MOE_GROUP).astype(F32),
                            (tiles, N_EXPERTS, LANES))
    dest_t, xs = _moe_sort_call(x2, idx.T, offv, off8, len8, dst8, n_blocks * MOE_BLOCK)
    out_rows = _moe_ffn_call(xs, block_e, nb_used, w1, b1, w2, b2)
    return _moe_combine_call(x2, dest_t.T, gate, out_rows, lg, lb, off8, len8, dst8, tot8,
                             split_rows)


def _rope_tables(seq):
    rows = seq // GRID_W
    row = jnp.repeat(jnp.arange(rows, dtype=F32), GRID_W)
    col = jnp.tile(jnp.arange(GRID_W, dtype=F32), rows)
    inv_freq = ROPE_THETA ** (-jnp.arange(ROPE_AXIS_HALF, dtype=F32) / ROPE_AXIS_HALF)
    ar = row[:, None] * inv_freq
    ac = col[:, None] * inv_freq
    cos_t = jnp.concatenate([jnp.cos(ar), jnp.cos(ar), jnp.cos(ac), jnp.cos(ac)], axis=1)
    sin_t = jnp.concatenate([-jnp.sin(ar), jnp.sin(ar), -jnp.sin(ac), jnp.sin(ac)], axis=1)
    return cos_t, sin_t


def _encode(xa, xb, mem, p):
    seq = xa.shape[1]
    b = xa.shape[0] + xb.shape[0]
    t = b * seq
    ta = xa.shape[0] * seq
    cos_t, sin_t = _rope_tables(seq)
    x = _ln_call(xa.reshape(ta, D_MODEL), xb.reshape(t - ta, D_MODEL), p["ln_in_g"], p["ln_in_b"])
    c0 = D_ATTN_Q + 2 * D_ATTN_KV
    c1 = c0 + 3 * D_MODEL
    c2 = c1 + 2 * D_MODEL
    for l in range(DEPTH):
        w_in = p["w_in"][l].astype(BF16)
        qg, kg = p["q_norm_g"][l], p["k_norm_g"][l]
        bound = (ATTN_BOUND_MARGIN * LOG2E * HEAD_DIM ** 0.5
                 * jnp.max(jnp.abs(qg)) * jnp.max(jnp.abs(kg)))
        use_shift = bound <= ATTN_SHIFT_LIMIT
        q, k, vt = _qkv_call(x, w_in[:, :c0], cos_t, sin_t, qg, kg,
                             jnp.where(use_shift, bound, 0.0), seq)
        cb, pc = _convproj_call(x, w_in[:, c0:c1])
        gy, rx = _rnnproj_call(x, w_in[:, c1:c2])
        g = _gateproj_call(x, w_in[:, c2:], p["b_gate"][l].reshape(-1))
        ya = lax.cond(use_shift,
                      functools.partial(_attn_call, shifted=True),
                      functools.partial(_attn_call, shifted=False),
                      q.reshape(b, seq, -1), k.reshape(b, seq, -1), vt)
        h = _lru_call(rx.reshape(b, seq, -1), p["lru_conv_w"][l], p["lru_conv_b"][l],
                      (0.5 * p["lru_wa"][l]).astype(BF16), (0.5 * p["lru_wx"][l]).astype(BF16),
                      0.5 * p["lru_ba"][l], 0.5 * p["lru_bx"][l], p["lru_lam"][l])
        mk, mv = _kv_call(mem, p["xkv_w"][l].astype(BF16))
        rw = jnp.pad(p["router_w"][l], ((0, 0), (0, LANES - N_EXPERTS))).astype(BF16)
        rb = jnp.pad(p["router_b"][l], (0, LANES - N_EXPERTS), constant_values=NEG_BIG)
        x2, idx, gate, cnt = _mix_cross_call(
            x, cb, pc, ya.reshape(t, -1), h.reshape(t, -1), gy, g, p["conv_w"][l],
            p["w_mix_out"][l].astype(BF16), p["ln1_g"][l], p["ln1_b"][l], mk, mv,
            p["xq_w"][l].astype(BF16), p["xo_w"][l].astype(BF16), p["ln2_g"][l], p["ln2_b"][l],
            rw, rb.reshape(1, -1), seq)
        x = _moe(x2, idx[:, :TOP_K], gate[:, :TOP_K], cnt[:, 0, :N_EXPERTS],
                 p["w1"][l], p["b1"][l][:, None, :], p["w2"][l],
                 p["b2"][l][:, None, :], p["ln3_g"][l], p["ln3_b"][l],
                 split_rows=ta if l == DEPTH - 1 else None)
    ya, yb = x
    return ya.reshape(xa.shape), yb.reshape(xb.shape)


def kernel(x_prompt, x_sample, mem_prompt, mem_sample, ln_in_g, ln_in_b, w_in, b_gate, q_norm_g, k_norm_g, conv_w, lru_conv_w, lru_conv_b, lru_wa, lru_ba, lru_wx, lru_bx, lru_lam, w_mix_out, ln1_g, ln1_b, xq_w, xkv_w, xo_w, ln2_g, ln2_b, router_w, router_b, w1, b1, w2, b2, ln3_g, ln3_b):
    params = dict(ln_in_g=ln_in_g, ln_in_b=ln_in_b, w_in=w_in, b_gate=b_gate, q_norm_g=q_norm_g,
                  k_norm_g=k_norm_g, conv_w=conv_w, lru_conv_w=lru_conv_w, lru_conv_b=lru_conv_b,
                  lru_wa=lru_wa, lru_ba=lru_ba, lru_wx=lru_wx, lru_bx=lru_bx, lru_lam=lru_lam,
                  w_mix_out=w_mix_out, ln1_g=ln1_g, ln1_b=ln1_b, xq_w=xq_w, xkv_w=xkv_w, xo_w=xo_w,
                  ln2_g=ln2_g, ln2_b=ln2_b, router_w=router_w, router_b=router_b, w1=w1, b1=b1,
                  w2=w2, b2=b2, ln3_g=ln3_g, ln3_b=ln3_b)
    assert x_prompt.shape[1:] == x_sample.shape[1:]
    return _encode(x_prompt, x_sample, jnp.concatenate([mem_prompt, mem_sample], axis=0), params)
```

```python
import functools

import jax
import jax.numpy as jnp
from jax import lax
from jax.experimental import pallas as pl
from jax.experimental.pallas import tpu as pltpu

F32 = jnp.float32
BF16 = jnp.bfloat16
I32 = jnp.int32
PACKED = jnp.uint32

D_MODEL = 1024
DEPTH = 2
N_HEADS = 8
N_KV_HEADS = 2
HEAD_DIM = 128
Q_GROUPS = N_HEADS // N_KV_HEADS
D_ATTN_Q = N_HEADS * HEAD_DIM
D_ATTN_KV = N_KV_HEADS * HEAD_DIM
GRID_W = 64
ROPE_AXIS_HALF = HEAD_DIM // 4
ROPE_THETA = 10000.0
QK_EPS = 1e-6
LRU_BLOCKS = 8
LRU_BLOCK_W = D_MODEL // LRU_BLOCKS
LRU_C = 8.0
MEM_HEADS = 4
MEM_HEAD_DIM = D_MODEL // MEM_HEADS
N_EXPERTS = 32
TOP_K = 4
D_FF = D_MODEL // 2
SWIGLU_LIMIT = 7.0
SWIGLU_ALPHA = 1.702
LN_EPS = 1e-5
DEEPNORM_ALPHA = (2 * DEPTH) ** 0.25

LANES = 128
SUBLANES = 8
BF16_SUBLANES = 16
VMEM_LIMIT = 56 * 1024 * 1024

PROJ_TM = 512
ATTN_TQ = 256
ATTN_TK = 512
ATTN_PAIRS_PER_BODY = 8
TOK_TILE = 256
CROSS_TM = 512
LRU_CHUNK = 512
LRU_SUBSEQ = 16
LRU_UNROLL = 8
MOE_GROUP = SUBLANES
MOE_BLOCK = 512
MOE_HALF = 256
MOE_TILE_ROWS = TOK_TILE * TOP_K + N_EXPERTS * MOE_GROUP
HEAD_AUG = 2 * HEAD_DIM
ATTN_SHIFT_LIMIT = 60.0
ATTN_BOUND_MARGIN = 1.02
NEG_BIG = -1e30
LOG2E = 1.4426950408889634


def _cparams(sem, vmem=VMEM_LIMIT):
    return pltpu.CompilerParams(dimension_semantics=sem, vmem_limit_bytes=vmem)


def _layer_norm(x, g, b):
    mu = jnp.mean(x, axis=-1, keepdims=True)
    xc = x - mu
    var = jnp.mean(xc * xc, axis=-1, keepdims=True)
    return xc * lax.rsqrt(var + LN_EPS) * g + b


def _sigmoid(x):
    return 1.0 / (1.0 + jnp.exp(-x))


def _sigmoid_tanh(x):
    return 0.5 * jnp.tanh(0.5 * x) + 0.5


def _dot(a, b):
    return jnp.dot(a, b, preferred_element_type=F32)


def _dot_nt(a, b):
    return lax.dot_general(a, b, (((1,), (1,)), ((), ())), preferred_element_type=F32)


def _ln_kernel(xa_ref, xb_ref, g_ref, b_ref, o_ref, *, tiles_a):
    @pl.when(pl.program_id(0) < tiles_a)
    def _():
        o_ref[...] = _layer_norm(xa_ref[...], g_ref[...], b_ref[...])

    @pl.when(pl.program_id(0) >= tiles_a)
    def _():
        o_ref[...] = _layer_norm(xb_ref[...], g_ref[...], b_ref[...])


def _ln_call(xa, xb, g, b):
    tm = PROJ_TM
    ta, tb = xa.shape[0] // tm, xb.shape[0] // tm
    return pl.pallas_call(
        functools.partial(_ln_kernel, tiles_a=ta),
        grid=(ta + tb,),
        in_specs=[pl.BlockSpec((tm, D_MODEL), lambda i: (jnp.minimum(i, ta - 1), 0)),
                  pl.BlockSpec((tm, D_MODEL), lambda i: (jnp.maximum(i - ta, 0), 0)),
                  pl.BlockSpec((1, D_MODEL), lambda i: (0, 0)),
                  pl.BlockSpec((1, D_MODEL), lambda i: (0, 0))],
        out_specs=pl.BlockSpec((tm, D_MODEL), lambda i: (i, 0)),
        out_shape=jax.ShapeDtypeStruct(((ta + tb) * tm, D_MODEL), F32),
        compiler_params=_cparams(("arbitrary",)),
        name="ln_in",
    )(xa, xb, g.reshape(1, -1), b.reshape(1, -1))


def _rope_partner_index():
    lane = jnp.arange(HEAD_DIM)
    return jnp.where((lane % (2 * ROPE_AXIS_HALF)) < ROPE_AXIS_HALF,
                     lane + ROPE_AXIS_HALF, lane - ROPE_AXIS_HALF)


def _qkv_kernel(x_ref, w_ref, cos_ref, sin_ref, qg_ref, kg_ref, shift_ref, q_ref, k_ref, vt_ref):
    x = x_ref[...].astype(BF16)
    c = cos_ref[...]
    s = sin_ref[...]
    lane = lax.broadcasted_iota(I32, c.shape, 1)
    n_qk = D_ATTN_Q + D_ATTN_KV
    z = _dot(x, w_ref[...])

    def prep(col, g_ref, scale):
        zh = z[:, col:col + HEAD_DIM]
        zp = z[:, n_qk + D_ATTN_KV + col:n_qk + D_ATTN_KV + col + HEAD_DIM]
        r = lax.rsqrt(jnp.mean(zh * zh, axis=-1, keepdims=True) + QK_EPS)
        return ((zh * (r * g_ref[0:1, :])) * c + (zp * (r * g_ref[1:2, :])) * s) * scale

    q_extra = jnp.broadcast_to(shift_ref[...], c.shape).astype(q_ref.dtype)
    k_extra = jnp.where(lane == 0, 1.0, 0.0).astype(k_ref.dtype)
    for h in range(N_HEADS):
        q_ref[:, h * HEAD_AUG:h * HEAD_AUG + HEAD_DIM] = prep(
            h * HEAD_DIM, qg_ref, LOG2E * HEAD_DIM ** -0.5).astype(q_ref.dtype)
        q_ref[:, h * HEAD_AUG + HEAD_DIM:(h + 1) * HEAD_AUG] = q_extra
    for h in range(N_KV_HEADS):
        k_ref[:, h * HEAD_AUG:h * HEAD_AUG + HEAD_DIM] = prep(
            D_ATTN_Q + h * HEAD_DIM, kg_ref, 1.0).astype(k_ref.dtype)
        k_ref[:, h * HEAD_AUG + HEAD_DIM:(h + 1) * HEAD_AUG] = k_extra
    vt_ref[0] = z[:, n_qk:n_qk + D_ATTN_KV].T.astype(vt_ref.dtype)


def _qkv_call(x, w, cos_t, sin_t, qg, kg, shift, seq):
    t = x.shape[0]
    tm = PROJ_TM
    nseq = seq // tm
    shift_row = jnp.zeros((1, HEAD_DIM), F32).at[0, 0].set(-shift)
    partner = _rope_partner_index()
    n_qk = D_ATTN_Q + D_ATTN_KV
    w_partner = w[:, :n_qk].reshape(D_MODEL, -1, 2, ROPE_AXIS_HALF)[:, :, ::-1, :].reshape(D_MODEL, n_qk)
    w = jnp.concatenate([w, w_partner], axis=1)
    n = w.shape[1]
    qg = jnp.stack([qg, qg[partner]])
    kg = jnp.stack([kg, kg[partner]])
    vec_spec = pl.BlockSpec((1, HEAD_DIM), lambda i: (0, 0))
    gain_spec = pl.BlockSpec((2, HEAD_DIM), lambda i: (0, 0))
    return pl.pallas_call(
        _qkv_kernel,
        grid=(t // tm,),
        in_specs=[pl.BlockSpec((tm, D_MODEL), lambda i: (i, 0)),
                  pl.BlockSpec((D_MODEL, n), lambda i: (0, 0)),
                  pl.BlockSpec((tm, HEAD_DIM), lambda i: (i % nseq, 0)),
                  pl.BlockSpec((tm, HEAD_DIM), lambda i: (i % nseq, 0)),
                  gain_spec, gain_spec, vec_spec],
        out_specs=[pl.BlockSpec((tm, N_HEADS * HEAD_AUG), lambda i: (i, 0)),
                   pl.BlockSpec((tm, N_KV_HEADS * HEAD_AUG), lambda i: (i, 0)),
                   pl.BlockSpec((1, D_ATTN_KV, tm), lambda i: (i // nseq, 0, i % nseq))],
        out_shape=[jax.ShapeDtypeStruct((t, N_HEADS * HEAD_AUG), BF16),
                   jax.ShapeDtypeStruct((t, N_KV_HEADS * HEAD_AUG), BF16),
                   jax.ShapeDtypeStruct((t // seq, D_ATTN_KV, seq), BF16)],
        compiler_params=_cparams(("parallel",)),
        name="qkv_proj",
    )(x, w, cos_t, sin_t, qg, kg, shift_row)


def _convproj_kernel(x_ref, w_ref, cb_ref, p_ref):
    x = x_ref[...].astype(BF16)
    z = _dot(x, w_ref[...])
    cb_ref[...] = z[:, :D_MODEL].astype(cb_ref.dtype)
    p_ref[...] = (z[:, D_MODEL:2 * D_MODEL] * z[:, 2 * D_MODEL:]).astype(p_ref.dtype)


def _convproj_call(x, w):
    t = x.shape[0]
    tm = PROJ_TM
    return pl.pallas_call(
        _convproj_kernel,
        grid=(t // tm,),
        in_specs=[pl.BlockSpec((tm, D_MODEL), lambda i: (i, 0)),
                  pl.BlockSpec((D_MODEL, 3 * D_MODEL), lambda i: (0, 0))],
        out_specs=[pl.BlockSpec((tm, D_MODEL), lambda i: (i, 0)),
                   pl.BlockSpec((tm, D_MODEL), lambda i: (i, 0))],
        out_shape=[jax.ShapeDtypeStruct((t, D_MODEL), BF16),
                   jax.ShapeDtypeStruct((t, D_MODEL), BF16)],
        compiler_params=_cparams(("parallel",)),
        name="conv_proj",
    )(x, w)


def _rnnproj_kernel(x_ref, w_ref, gy_ref, rx_ref):
    x = x_ref[...].astype(BF16)
    z = _dot(x, w_ref[...])
    gy_ref[...] = jax.nn.gelu(z[:, :D_MODEL]).astype(gy_ref.dtype)
    rx_ref[...] = z[:, D_MODEL:]


def _rnnproj_call(x, w):
    t = x.shape[0]
    tm = PROJ_TM
    return pl.pallas_call(
        _rnnproj_kernel,
        grid=(t // tm,),
        in_specs=[pl.BlockSpec((tm, D_MODEL), lambda i: (i, 0)),
                  pl.BlockSpec((D_MODEL, 2 * D_MODEL), lambda i: (0, 0))],
        out_specs=[pl.BlockSpec((tm, D_MODEL), lambda i: (i, 0)),
                   pl.BlockSpec((tm, D_MODEL), lambda i: (i, 0))],
        out_shape=[jax.ShapeDtypeStruct((t, D_MODEL), BF16),
                   jax.ShapeDtypeStruct((t, D_MODEL), F32)],
        compiler_params=_cparams(("parallel",)),
        name="rnn_proj",
    )(x, w)


def _gateproj_kernel(x_ref, w_ref, b_ref, g_ref):
    x = x_ref[...].astype(BF16)
    z = _dot(x, w_ref[...]) + b_ref[...]
    g_ref[...] = _sigmoid(z).astype(g_ref.dtype)


def _gateproj_call(x, w, b):
    t = x.shape[0]
    tm = PROJ_TM
    return pl.pallas_call(
        _gateproj_kernel,
        grid=(t // tm,),
        in_specs=[pl.BlockSpec((tm, D_MODEL), lambda i: (i, 0)),
                  pl.BlockSpec((D_MODEL, 3 * D_MODEL), lambda i: (0, 0)),
                  pl.BlockSpec((1, 3 * D_MODEL), lambda i: (0, 0))],
        out_specs=pl.BlockSpec((tm, 3 * D_MODEL), lambda i: (i, 0)),
        out_shape=jax.ShapeDtypeStruct((t, 3 * D_MODEL), BF16),
        compiler_params=_cparams(("parallel",)),
        name="gate_proj",
    )(x, w, b.reshape(1, -1))


def _attn_kernel(q_ref, k_ref, vt_ref, o_ref, acc_s, s0_s, s1_s, p0_s, p1_s, *, tk):
    tq = q_ref.shape[1]
    n = Q_GROUPS * tq
    nk = k_ref.shape[1] // tk
    qs = _stack_heads(q_ref)
    acc_s[...] = jnp.zeros(acc_s.shape, F32)
    s_bufs = (s0_s, s1_s)
    p_bufs = (p0_s, p1_s)

    def scores(j, slot):
        start = pl.multiple_of(j * tk, tk)
        s_bufs[slot][...] = _dot_nt(k_ref[0, pl.ds(start, tk), :], qs)

    def softmax(slot, m, l):
        st = s_bufs[slot][...]
        m_new = jnp.maximum(m, jnp.max(st, axis=0, keepdims=True))
        alpha = jnp.exp2(m - m_new)
        p = jnp.exp2(st - m_new)
        p_bufs[slot][...] = p.astype(BF16)
        return m_new, alpha * l + jnp.sum(p, axis=0, keepdims=True), alpha

    def accumulate(j, slot, alpha):
        start = pl.multiple_of(j * tk, tk)
        acc_s[...] = alpha * acc_s[...] + _dot(vt_ref[0, :, pl.ds(start, tk)], p_bufs[slot][...])

    def stage(j, m, l, alphas, first, last):
        new_alphas = []
        for slot in range(2):
            if not first:
                accumulate(j - 2 + slot, slot, alphas[slot])
            m, l, a = softmax(slot, m, l)
            new_alphas.append(a)
            if not last:
                scores(j + 2 + slot, slot)
        return m, l, tuple(new_alphas)

    m = jnp.full((1, n), NEG_BIG, F32)
    l = jnp.zeros((1, n), F32)
    scores(0, 0)
    scores(1, 1)
    if nk == 2:
        m, l, alphas = stage(0, m, l, None, True, True)
    else:
        m, l, alphas = stage(0, m, l, None, True, False)
        m, l, alphas = lax.fori_loop(
            1, nk // 2 - 1, lambda i, c: stage(2 * i, c[0], c[1], c[2], False, False), (m, l, alphas))
        m, l, alphas = stage(nk - 2, m, l, alphas, False, True)
    accumulate(nk - 2, 0, alphas[0])
    accumulate(nk - 1, 1, alphas[1])
    _write_heads(o_ref, acc_s[...] / l)


def _stack_heads(q_ref):
    return jnp.concatenate([q_ref[0, :, g * HEAD_AUG:(g + 1) * HEAD_AUG] for g in range(Q_GROUPS)],
                           axis=0)


def _write_heads(o_ref, ot):
    tq = o_ref.shape[1]
    for g in range(Q_GROUPS):
        o_ref[0, :, g * HEAD_DIM:(g + 1) * HEAD_DIM] = ot[:, g * tq:(g + 1) * tq].T.astype(o_ref.dtype)


def _attn_shifted_kernel(q_ref, k_ref, vt_ref, o_ref, acc_s, p0_s, p1_s, *, tk):
    n = Q_GROUPS * q_ref.shape[1]
    nk = k_ref.shape[1] // tk
    qs = _stack_heads(q_ref)
    acc_s[...] = jnp.zeros(acc_s.shape, F32)
    p_bufs = (p0_s, p1_s)

    def probs(j, slot, l):
        start = pl.multiple_of(j * tk, tk)
        p = jnp.exp2(_dot_nt(k_ref[0, pl.ds(start, tk), :], qs))
        p_bufs[slot][...] = p.astype(BF16)
        return l + jnp.sum(p, axis=0, keepdims=True)

    def accumulate(j, slot):
        start = pl.multiple_of(j * tk, tk)
        acc_s[...] += _dot(vt_ref[0, :, pl.ds(start, tk)], p_bufs[slot][...])

    def pair(i, l):
        j = 2 * i
        l = probs(j + 1, 1, l)
        accumulate(j, 0)
        l = probs(j + 2, 0, l)
        accumulate(j + 1, 1)
        return l

    l = probs(0, 0, jnp.zeros((1, n), F32))
    n_pairs = nk // 2 - 1
    trips = n_pairs // ATTN_PAIRS_PER_BODY

    def body(i, l):
        for r in range(ATTN_PAIRS_PER_BODY):
            l = pair(i * ATTN_PAIRS_PER_BODY + r, l)
        return l

    if trips > 1:
        l = lax.fori_loop(0, trips, body, l)
    else:
        trips = 0
    for i in range(trips * ATTN_PAIRS_PER_BODY, n_pairs):
        l = pair(i, l)
    l = probs(nk - 1, 1, l)
    accumulate(nk - 2, 0)
    accumulate(nk - 1, 1)
    _write_heads(o_ref, acc_s[...] / l)


def _attn_call(q, k, vt, shifted):
    b, s, _ = q.shape
    tq = min(ATTN_TQ, s)
    tk = min(ATTN_TK, s // 2)
    assert (s // tk) % 2 == 0
    n = Q_GROUPS * tq
    p_bufs = [pltpu.VMEM((tk, n), BF16)] * 2
    s_bufs = [] if shifted else [pltpu.VMEM((tk, n), F32)] * 2
    return pl.pallas_call(
        functools.partial(_attn_shifted_kernel if shifted else _attn_kernel, tk=tk),
        grid=(b, N_KV_HEADS, s // tq),
        in_specs=[pl.BlockSpec((1, tq, Q_GROUPS * HEAD_AUG), lambda bi, h, i: (bi, i, h)),
                  pl.BlockSpec((1, s, HEAD_AUG), lambda bi, h, i: (bi, 0, h)),
                  pl.BlockSpec((1, HEAD_DIM, s), lambda bi, h, i: (bi, h, 0))],
        out_specs=pl.BlockSpec((1, tq, Q_GROUPS * HEAD_DIM), lambda bi, h, i: (bi, i, h)),
        out_shape=jax.ShapeDtypeStruct((b, s, D_ATTN_Q), BF16),
        scratch_shapes=[pltpu.VMEM((HEAD_DIM, n), F32)] + s_bufs + p_bufs,
        compiler_params=_cparams(("parallel", "parallel", "parallel")),
        name="flash_attn_shifted" if shifted else "flash_attn",
    )(q, k, vt)


def _lru_kernel(x_ref, cw_ref, cb_ref, wa_ref, wx_ref, ba_ref, bx_ref, lam_ref, o_hbm,
                xp_s, a0_s, u0_s, a1_s, u1_s, h_s, sem, *, seq, chunk):
    sub = seq // LRU_SUBSEQ
    cps = sub // chunk
    nchunks = seq // chunk

    zeros8 = jnp.zeros((SUBLANES, LANES), F32)
    xp_s[0:SUBLANES, :] = zeros8
    xp_s[seq + SUBLANES:seq + 2 * SUBLANES, :] = zeros8
    xp_s[SUBLANES:seq + SUBLANES, :] = x_ref[0]
    cw = cw_ref[...]
    cb = cb_ref[...]
    a_bufs = (a0_s, a1_s)
    u_bufs = (u0_s, u1_s)

    neg = -lam_ref[...]
    softplus = jnp.maximum(neg, 0.0) + jnp.log1p(jnp.exp(-jnp.abs(neg)))
    half_decay = -0.5 * LRU_C * softplus

    def gate_chunk(c, _):
        base = pl.multiple_of(c * chunk, chunk)
        xc = cb
        for k in range(4):
            xc = xc + cw[k:k + 1, :] * xp_s[pl.ds(base + SUBLANES - 1 + k, chunk), :]
        xb = xc.astype(BF16)
        xh = 0.5 * xc
        dst = pl.ds((c % cps) * (chunk * LRU_SUBSEQ) + c // cps, chunk, stride=LRU_SUBSEQ)
        for d in range(2):
            tr = jnp.tanh(_dot(xb, wa_ref[d, 0]) + ba_ref[d:d + 1, :])
            ti = jnp.tanh(_dot(xb, wx_ref[d, 0]) + bx_ref[d:d + 1, :])
            a = jnp.exp(half_decay[d:d + 1, :] * tr + half_decay[d:d + 1, :])
            u_bufs[d][dst, :] = jnp.sqrt(1.0 - a * a) * (xh * (ti + 1.0))
            a_bufs[d][dst, :] = a
        return 0

    lax.fori_loop(0, nchunks, gate_chunk, 0)

    def step(tt, carry):
        out = []
        for d in range(2):
            h, p = carry[d]
            t = tt if d == 0 else sub - 1 - tt
            idx = pl.ds(pl.multiple_of(t * LRU_SUBSEQ, LRU_SUBSEQ), LRU_SUBSEQ)
            av = a_bufs[d][idx, :]
            h = av * h + u_bufs[d][idx, :]
            p = av * p
            u_bufs[d][idx, :] = h
            a_bufs[d][idx, :] = p
            out.append((h, p))
        return tuple(out)

    h0 = jnp.zeros((LRU_SUBSEQ, LANES), F32)
    p0 = jnp.ones((LRU_SUBSEQ, LANES), F32)
    ends = lax.fori_loop(0, sub, step, ((h0, p0), (h0, p0)), unroll=LRU_UNROLL)

    carry_in = []
    for d in range(2):
        h_end, p_end = ends[d]
        order = range(LRU_SUBSEQ) if d == 0 else range(LRU_SUBSEQ - 1, -1, -1)
        states = [None] * LRU_SUBSEQ
        state = jnp.zeros((1, LANES), F32)
        for j in order:
            states[j] = state
            state = p_end[j:j + 1, :] * state + h_end[j:j + 1, :]
        carry_in.append(jnp.concatenate(states, axis=0))

    step_id = pl.program_id(0) * pl.num_programs(1) + pl.program_id(1)
    last_id = pl.num_programs(0) * pl.num_programs(1) - 1

    def out_copies(bi, ci):
        return [pltpu.make_async_copy(
            h_s.at[:, j, :],
            o_hbm.at[bi, pl.ds(j * sub, sub), pl.ds(pl.multiple_of(ci * LANES, LANES), LANES)],
            sem) for j in range(LRU_SUBSEQ)]

    @pl.when(step_id > 0)
    def _():
        for cp in out_copies(0, 0):
            cp.wait()

    steps = min(sub, chunk)
    for c in range(sub // steps):
        rows = slice(c * steps * LRU_SUBSEQ, (c + 1) * steps * LRU_SUBSEQ)
        c0 = jnp.tile(carry_in[0], (steps, 1))
        c1 = jnp.tile(carry_in[1], (steps, 1))
        val = u0_s[rows, :] + a0_s[rows, :] * c0 + u1_s[rows, :] + a1_s[rows, :] * c1
        h_s[c * steps:(c + 1) * steps] = val.reshape(steps, LRU_SUBSEQ, LANES)

    for cp in out_copies(pl.program_id(0), pl.program_id(1)):
        cp.start()

    @pl.when(step_id == last_id)
    def _():
        for cp in out_copies(0, 0):
            cp.wait()


def _lru_call(rx, cw, cb, wa, wx, ba, bx, lam):
    b, s, _ = rx.shape
    chunk = min(LRU_CHUNK, s // LRU_SUBSEQ)
    assert LRU_SUBSEQ % SUBLANES == 0
    w = LRU_BLOCK_W
    return pl.pallas_call(
        functools.partial(_lru_kernel, seq=s, chunk=chunk),
        grid=(b, LRU_BLOCKS),
        in_specs=[pl.BlockSpec((1, s, w), lambda bi, c: (bi, 0, c)),
                  pl.BlockSpec((4, w), lambda bi, c: (0, c)),
                  pl.BlockSpec((1, w), lambda bi, c: (0, c)),
                  pl.BlockSpec((2, 1, w, w), lambda bi, c: (0, c, 0, 0)),
                  pl.BlockSpec((2, 1, w, w), lambda bi, c: (0, c, 0, 0)),
                  pl.BlockSpec((2, w), lambda bi, c: (0, c)),
                  pl.BlockSpec((2, w), lambda bi, c: (0, c)),
                  pl.BlockSpec((2, w), lambda bi, c: (0, c))],
        out_specs=pl.BlockSpec(memory_space=pl.ANY),
        out_shape=jax.ShapeDtypeStruct((b, s, D_MODEL), F32),
        scratch_shapes=[pltpu.VMEM((s + 2 * SUBLANES, w), F32)]
        + [pltpu.VMEM((s, w), F32)] * 4
        + [pltpu.VMEM((s // LRU_SUBSEQ, LRU_SUBSEQ, w), F32), pltpu.SemaphoreType.DMA(())],
        compiler_params=_cparams(("arbitrary", "arbitrary")),
        name="rg_lru",
    )(rx, cw, cb.reshape(1, -1), wa, wx, ba, bx, lam)


def _kv_kernel(m_ref, w_ref, k_ref, v_ref):
    z = _dot(m_ref[0].astype(BF16), w_ref[...])
    k_ref[0] = z[:, :D_MODEL].astype(k_ref.dtype)
    v_ref[0] = z[:, D_MODEL:].astype(v_ref.dtype)


def _kv_call(mem, w):
    b, m, _ = mem.shape
    return pl.pallas_call(
        _kv_kernel,
        grid=(b,),
        in_specs=[pl.BlockSpec((1, m, D_MODEL), lambda i: (i, 0, 0)),
                  pl.BlockSpec((D_MODEL, 2 * D_MODEL), lambda i: (0, 0))],
        out_specs=[pl.BlockSpec((1, m, D_MODEL), lambda i: (i, 0, 0)),
                   pl.BlockSpec((1, m, D_MODEL), lambda i: (i, 0, 0))],
        out_shape=[jax.ShapeDtypeStruct((b, m, D_MODEL), BF16),
                   jax.ShapeDtypeStruct((b, m, D_MODEL), BF16)],
        compiler_params=_cparams(("parallel",)),
        name="mem_kv_proj",
    )(mem, w)


def _mix_cross_kernel(x_ref, cb_ref, p_ref, pprev_ref, pnext_ref, ya_ref, h_ref, gy_ref, g_ref,
                      cw_ref, wm_ref, l1g_ref, l1b_ref, k_ref, v_ref, wq_ref, wo_ref, l2g_ref,
                      l2b_ref, rw_ref, rb_ref, x2_ref, idx_ref, gate_ref, cnt_ref, *, seq):
    tm = x_ref.shape[0]
    t0 = pl.program_id(0) * tm
    has_prev = ((t0 % seq) != 0).astype(F32)
    has_next = (((t0 + tm) % seq) != 0).astype(F32)
    cw = cw_ref[...]
    row = lax.broadcasted_iota(I32, (TOK_TILE, D_MODEL), 0)
    halves = tm // TOK_TILE
    for r in range(halves):
        rows = slice(r * TOK_TILE, (r + 1) * TOK_TILE)
        lo, hi = r * TOK_TILE, (r + 1) * TOK_TILE
        prev_row = (pprev_ref[BF16_SUBLANES - 1:BF16_SUBLANES, :].astype(F32) * has_prev if r == 0
                    else p_ref[lo - 1:lo, :].astype(F32))
        next_row = (pnext_ref[0:1, :].astype(F32) * has_next if r == halves - 1
                    else p_ref[hi:hi + 1, :].astype(F32))
        p = p_ref[rows, :].astype(F32)
        p_dn = jnp.where(row == 0, prev_row, pltpu.roll(p, 1, 0))
        p_up = jnp.where(row == TOK_TILE - 1, next_row, pltpu.roll(p, TOK_TILE - 1, 0))
        conv = cw[0:1, :] * p_dn + cw[1:2, :] * p + cw[2:3, :] * p_up
        y_conv = cb_ref[rows, :].astype(F32) * conv
        y_rnn = h_ref[rows, :] * gy_ref[rows, :].astype(F32)
        merged = (g_ref[rows, :D_MODEL].astype(F32) * y_conv
                  + g_ref[rows, D_MODEL:2 * D_MODEL].astype(F32) * ya_ref[rows, :].astype(F32)
                  + g_ref[rows, 2 * D_MODEL:].astype(F32) * y_rnn)
        mix = _dot(merged.astype(BF16), wm_ref[...])
        x1 = _layer_norm(DEEPNORM_ALPHA * x_ref[rows, :] + mix, l1g_ref[...], l1b_ref[...])
        _cross_rows(x1, k_ref, v_ref, wq_ref, wo_ref, l2g_ref, l2b_ref, rw_ref, rb_ref,
                    x2_ref.at[rows, :], idx_ref.at[rows, :], gate_ref.at[rows, :], cnt_ref.at[r])


def _cross_rows(x, k_ref, v_ref, wq_ref, wo_ref, lg_ref, lb_ref, rw_ref, rb_ref,
                x2_ref, idx_ref, gate_ref, cnt_ref):
    q = _dot(x.astype(BF16), wq_ref[...]) * (MEM_HEAD_DIM ** -0.5)
    outs = []
    for h in range(MEM_HEADS):
        sl = slice(h * MEM_HEAD_DIM, (h + 1) * MEM_HEAD_DIM)
        s = _dot_nt(q[:, sl].astype(BF16), k_ref[0, :, sl])
        m = jnp.max(s, axis=-1, keepdims=True)
        p = jnp.exp(s - m)
        p = p / jnp.sum(p, axis=-1, keepdims=True)
        outs.append(_dot(p.astype(BF16), v_ref[0, :, sl]))
    o = jnp.concatenate(outs, axis=1)
    cross = _dot(o.astype(BF16), wo_ref[...])
    x2 = _layer_norm(DEEPNORM_ALPHA * x + cross, lg_ref[...], lb_ref[...])
    x2_ref[...] = x2

    logits = _dot(x2.astype(BF16), rw_ref[...]) + rb_ref[...]
    lane = lax.broadcasted_iota(I32, logits.shape, 1)
    idx_acc = jnp.zeros(logits.shape, I32)
    val_acc = jnp.zeros(logits.shape, F32)
    member = jnp.zeros(logits.shape, F32)
    top0 = None
    for k in range(TOP_K):
        m = jnp.max(logits, axis=-1, keepdims=True)
        pick = jnp.min(jnp.where(logits == m, lane, LANES), axis=-1, keepdims=True)
        hit = lane == pick
        if k == 0:
            top0 = m
        idx_acc = jnp.where(lane == k, pick, idx_acc)
        val_acc = jnp.where(lane == k, jnp.exp(m - top0), val_acc)
        member = member + hit.astype(F32)
        logits = jnp.where(hit, NEG_BIG, logits)
    idx_ref[...] = idx_acc
    gate_ref[...] = val_acc / jnp.sum(val_acc, axis=-1, keepdims=True)
    cnt_ref[...] = jnp.sum(member, axis=0, keepdims=True).astype(I32)


def _mix_cross_call(x, cb, p, ya, h, gy, g, cw, wm, l1g, l1b, k, v, wq, wo, l2g, l2b, rw, rb, seq):
    t = x.shape[0]
    tm = CROSS_TM
    halves = tm // TOK_TILE
    per_seq = seq // tm
    m = k.shape[1]
    hb = BF16_SUBLANES
    per = tm // hb
    last = t // hb - 1
    row_spec = pl.BlockSpec((tm, D_MODEL), lambda i: (i, 0))
    vec_spec = pl.BlockSpec((1, D_MODEL), lambda i: (0, 0))
    mat_spec = pl.BlockSpec((D_MODEL, D_MODEL), lambda i: (0, 0))
    mem_spec = pl.BlockSpec((1, m, D_MODEL), lambda i: (i // per_seq, 0, 0))
    lane_spec = pl.BlockSpec((tm, LANES), lambda i: (i, 0))
    return pl.pallas_call(
        functools.partial(_mix_cross_kernel, seq=seq),
        grid=(t // tm,),
        in_specs=[row_spec, row_spec, row_spec,
                  pl.BlockSpec((hb, D_MODEL), lambda i: (jnp.maximum(i * per - 1, 0), 0)),
                  pl.BlockSpec((hb, D_MODEL), lambda i: (jnp.minimum((i + 1) * per, last), 0)),
                  row_spec, row_spec, row_spec,
                  pl.BlockSpec((tm, 3 * D_MODEL), lambda i: (i, 0)),
                  pl.BlockSpec((3, D_MODEL), lambda i: (0, 0)),
                  mat_spec, vec_spec, vec_spec,
                  mem_spec, mem_spec, mat_spec, mat_spec, vec_spec, vec_spec,
                  pl.BlockSpec((D_MODEL, LANES), lambda i: (0, 0)),
                  pl.BlockSpec((1, LANES), lambda i: (0, 0))],
        out_specs=[row_spec, lane_spec, lane_spec,
                   pl.BlockSpec((halves, 1, LANES), lambda i: (i, 0, 0))],
        out_shape=[jax.ShapeDtypeStruct((t, D_MODEL), F32),
                   jax.ShapeDtypeStruct((t, LANES), I32),
                   jax.ShapeDtypeStruct((t, LANES), F32),
                   jax.ShapeDtypeStruct((t // TOK_TILE, 1, LANES), I32)],
        compiler_params=_cparams(("parallel",)),
        name="mix_cross_router",
    )(x, cb, p, p, p, ya, h, gy, g, cw, wm, l1g.reshape(1, -1), l1b.reshape(1, -1), k, v, wq, wo,
      l2g.reshape(1, -1), l2b.reshape(1, -1), rw, rb)


def _pack_rows(v):
    bits = pltpu.bitcast(v, PACKED)
    half = v.shape[1] // 2
    return (bits[:, :half] & jnp.uint32(0xFFFF0000)) | (bits[:, half:] >> 16)


def _unpack_rows(u):
    hi = pltpu.bitcast(u & jnp.uint32(0xFFFF0000), F32).astype(BF16)
    lo = pltpu.bitcast(u << 16, F32).astype(BF16)
    return hi, lo


def _groups_copy(src, dst, sem, s_grp, d_grp, n_grp):
    rows = n_grp * MOE_GROUP
    return pltpu.make_async_copy(
        src.at[pl.ds(pl.multiple_of(s_grp * MOE_GROUP, MOE_GROUP), rows), :],
        dst.at[pl.ds(pl.multiple_of(d_grp * MOE_GROUP, MOE_GROUP), rows), :], sem)


def _moe_sort_kernel(off8_ref, len8_ref, dst8_ref, x_ref, idx_ref, offv_ref, dest_ref, xs_hbm,
                     xs_s, sem, pending_s):
    s = pl.program_id(0)
    tm = x_ref.shape[0]
    idx = idx_ref[...]
    e_iota = lax.broadcasted_iota(I32, (N_EXPERTS, tm), 0)
    member = jnp.zeros((N_EXPERTS, tm), F32)
    for k in range(TOP_K):
        member = member + (e_iota == idx[k:k + 1, :]).astype(F32)
    earlier = (lax.broadcasted_iota(I32, (tm, tm), 0)
               < lax.broadcasted_iota(I32, (tm, tm), 1)).astype(BF16)
    rank = _dot(member.astype(BF16), earlier)
    pos = rank + offv_ref[0][:, 0:1]
    r_iota = lax.broadcasted_iota(I32, (MOE_TILE_ROWS, tm), 0)
    onehot = jnp.zeros((MOE_TILE_ROWS, tm), F32)
    dests = []
    for k in range(TOP_K):
        dk = jnp.sum(jnp.where(e_iota == idx[k:k + 1, :], pos, 0.0), axis=0, keepdims=True)
        dk = dk.astype(I32)
        dests.append(dk)
        onehot = jnp.where(r_iota == dk, 1.0, onehot)
    dest_ref[...] = jnp.concatenate(dests, axis=0)
    slot = s % 2
    xs_s[slot] = _pack_rows(_dot(onehot.astype(BF16), x_ref[...].astype(BF16)))

    def per_expert(e, total):
        n = len8_ref[s * N_EXPERTS + e]

        @pl.when(n > 0)
        def _():
            _groups_copy(xs_s.at[slot], xs_hbm, sem.at[slot], off8_ref[s * N_EXPERTS + e],
                         dst8_ref[s * N_EXPERTS + e], n).start()

        return total + n

    total = lax.fori_loop(0, N_EXPERTS, per_expert, 0)

    def drain(which, count):
        @pl.when(count > 0)
        def _():
            _groups_copy(xs_s.at[which], xs_hbm, sem.at[which], 0, 0, count).wait()

    @pl.when(s > 0)
    def _():
        drain(1 - slot, pending_s[0])

    pending_s[0] = total

    @pl.when(s == pl.num_programs(0) - 1)
    def _():
        drain(slot, total)


def _moe_sort_call(x2, idx_t, offv, off8, len8, dst8, n_rows):
    t = x2.shape[0]
    tm = TOK_TILE
    grid_spec = pltpu.PrefetchScalarGridSpec(
        num_scalar_prefetch=3,
        grid=(t // tm,),
        in_specs=[pl.BlockSpec((tm, D_MODEL), lambda i, *_: (i, 0)),
                  pl.BlockSpec((TOP_K, tm), lambda i, *_: (0, i)),
                  pl.BlockSpec((1, N_EXPERTS, LANES), lambda i, *_: (i, 0, 0))],
        out_specs=[pl.BlockSpec((TOP_K, tm), lambda i, *_: (0, i)),
                   pl.BlockSpec(memory_space=pl.ANY)],
        scratch_shapes=[pltpu.VMEM((2, MOE_TILE_ROWS, D_MODEL // 2), PACKED),
                        pltpu.SemaphoreType.DMA((2,)),
                        pltpu.SMEM((1,), I32)],
    )
    return pl.pallas_call(
        _moe_sort_kernel,
        grid_spec=grid_spec,
        out_shape=[jax.ShapeDtypeStruct((TOP_K, t), I32),
                   jax.ShapeDtypeStruct((n_rows, D_MODEL // 2), PACKED)],
        compiler_params=_cparams(("arbitrary",)),
        name="moe_sort",
    )(off8, len8, dst8, x2, idx_t, offv)


def _moe_ffn_kernel(be_ref, nb_ref, xs_ref, w1_ref, b1_ref, w2_ref, b2_ref, o_ref, w1_s, w2_s):
    i = pl.program_id(0)

    @pl.when((i == 0) | (be_ref[i] != be_ref[jnp.maximum(i - 1, 0)]))
    def _():
        w1_s[...] = w1_ref[0].astype(BF16)
        w2_s[...] = w2_ref[0].astype(BF16)

    @pl.when(i < nb_ref[0])
    def _():
        half = D_MODEL // 2
        for r in range(MOE_BLOCK // MOE_HALF):
            rows = slice(r * MOE_HALF, (r + 1) * MOE_HALF)
            hi, lo = _unpack_rows(xs_ref[rows, :])
            hcat = _dot(hi, w1_s[:half, :]) + _dot(lo, w1_s[half:, :]) + b1_ref[0]
            glu = jnp.minimum(hcat[:, :D_FF], SWIGLU_LIMIT)
            lin = jnp.clip(hcat[:, D_FF:], -SWIGLU_LIMIT, SWIGLU_LIMIT)
            act = glu * _sigmoid_tanh(SWIGLU_ALPHA * glu) * (lin + 1.0)
            out = _dot(act.astype(BF16), w2_s[...]) + b2_ref[0]
            o_ref[rows, :] = _pack_rows(out.astype(BF16).astype(F32))


def _moe_ffn_call(xs, block_e, nb_used, w1, b1, w2, b2, layer):
    n_rows = xs.shape[0]
    nb = n_rows // MOE_BLOCK

    def blk(i, be, nbu):
        return (jnp.minimum(i, nbu[0] - 1), 0)

    def expert(i, be, nbu):
        return (layer, be[i], 0, 0)

    grid_spec = pltpu.PrefetchScalarGridSpec(
        num_scalar_prefetch=2,
        grid=(nb,),
        in_specs=[pl.BlockSpec((MOE_BLOCK, D_MODEL // 2), blk),
                  pl.BlockSpec((None, 1, D_MODEL, 2 * D_FF), expert),
                  pl.BlockSpec((None, 1, 1, 2 * D_FF), expert),
                  pl.BlockSpec((None, 1, D_FF, D_MODEL), expert),
                  pl.BlockSpec((None, 1, 1, D_MODEL), expert)],
        out_specs=pl.BlockSpec((MOE_BLOCK, D_MODEL // 2), blk),
        scratch_shapes=[pltpu.VMEM((D_MODEL, 2 * D_FF), BF16), pltpu.VMEM((D_FF, D_MODEL), BF16)],
    )
    return pl.pallas_call(
        _moe_ffn_kernel,
        grid_spec=grid_spec,
        out_shape=jax.ShapeDtypeStruct((n_rows, D_MODEL // 2), PACKED),
        compiler_params=_cparams(("arbitrary",)),
        name="moe_ffn",
    )(block_e, nb_used, xs, w1, b1, w2, b2)


def _moe_combine_kernel(off8_ref, len8_ref, dst8_ref, tot8_ref, x_ref, dest_ref, gate_ref,
                        lg_ref, lb_ref, out_hbm, *refs, tiles_a):
    o_refs, (buf, sem) = refs[:-2], refs[-2:]
    s = pl.program_id(0)
    tm = x_ref.shape[0]
    slot = s % 2

    def gather(tile):
        which = tile % 2

        def per_expert(e, _):
            n = len8_ref[tile * N_EXPERTS + e]

            @pl.when(n > 0)
            def _():
                _groups_copy(out_hbm, buf.at[which], sem.at[which], dst8_ref[tile * N_EXPERTS + e],
                             off8_ref[tile * N_EXPERTS + e], n).start()

            return 0

        lax.fori_loop(0, N_EXPERTS, per_expert, 0)

    @pl.when(s == 0)
    def _():
        gather(s)

    @pl.when(s + 1 < pl.num_programs(0))
    def _():
        gather(s + 1)

    def clear(g, _):
        buf[slot, pl.ds(pl.multiple_of(g * MOE_GROUP, MOE_GROUP), MOE_GROUP), :] = jnp.zeros(
            (MOE_GROUP, D_MODEL // 2), PACKED)
        return 0

    lax.fori_loop(tot8_ref[s], MOE_TILE_ROWS // MOE_GROUP, clear, 0)

    dest = dest_ref[...]
    gate = gate_ref[...]
    r_iota = lax.broadcasted_iota(I32, (tm, MOE_TILE_ROWS), 1)
    weights = jnp.zeros((tm, MOE_TILE_ROWS), F32)
    for k in range(TOP_K):
        weights = jnp.where(r_iota == dest[:, k:k + 1], gate[:, k:k + 1], weights)

    _groups_copy(out_hbm, buf.at[slot], sem.at[slot], 0, 0, tot8_ref[s]).wait()
    hi, lo = _unpack_rows(buf[slot])
    wb = weights.astype(BF16)
    ff = jnp.concatenate([_dot(wb, hi), _dot(wb, lo)], axis=1)
    y = _layer_norm(DEEPNORM_ALPHA * x_ref[...] + ff, lg_ref[...], lb_ref[...])
    if tiles_a is None:
        o_refs[0][...] = y
    else:
        @pl.when(s < tiles_a)
        def _():
            o_refs[0][...] = y

        @pl.when(s >= tiles_a)
        def _():
            o_refs[1][...] = y


def _moe_combine_call(x2, dest, gate, out_rows, lg, lb, off8, len8, dst8, tot8, split_rows):
    t = x2.shape[0]
    tm = TOK_TILE
    if split_rows is None:
        tiles_a = None
        out_specs = pl.BlockSpec((tm, D_MODEL), lambda i, *_: (i, 0))
        out_shape = jax.ShapeDtypeStruct((t, D_MODEL), F32)
    else:
        tiles_a = split_rows // tm
        out_specs = [pl.BlockSpec((tm, D_MODEL), lambda i, *_: (jnp.minimum(i, tiles_a - 1), 0)),
                     pl.BlockSpec((tm, D_MODEL), lambda i, *_: (jnp.maximum(i - tiles_a, 0), 0))]
        out_shape = [jax.ShapeDtypeStruct((split_rows, D_MODEL), F32),
                     jax.ShapeDtypeStruct((t - split_rows, D_MODEL), F32)]
    grid_spec = pltpu.PrefetchScalarGridSpec(
        num_scalar_prefetch=4,
        grid=(t // tm,),
        in_specs=[pl.BlockSpec((tm, D_MODEL), lambda i, *_: (i, 0)),
                  pl.BlockSpec((tm, TOP_K), lambda i, *_: (i, 0)),
                  pl.BlockSpec((tm, TOP_K), lambda i, *_: (i, 0)),
                  pl.BlockSpec((1, D_MODEL), lambda i, *_: (0, 0)),
                  pl.BlockSpec((1, D_MODEL), lambda i, *_: (0, 0)),
                  pl.BlockSpec(memory_space=pl.ANY)],
        out_specs=out_specs,
        scratch_shapes=[pltpu.VMEM((2, MOE_TILE_ROWS, D_MODEL // 2), PACKED),
                        pltpu.SemaphoreType.DMA((2,))],
    )
    return pl.pallas_call(
        functools.partial(_moe_combine_kernel, tiles_a=tiles_a),
        grid_spec=grid_spec,
        out_shape=out_shape,
        compiler_params=_cparams(("arbitrary",)),
        name="moe_combine_ln3",
    )(off8, len8, dst8, tot8, x2, dest, gate, lg.reshape(1, -1), lb.reshape(1, -1), out_rows)


def _moe_tables(cnt, n_blocks):
    len8 = (cnt + MOE_GROUP - 1) // MOE_GROUP
    off8 = jnp.cumsum(len8, axis=1) - len8
    tot8 = jnp.sum(len8, axis=1)
    per_block = MOE_BLOCK // MOE_GROUP
    blocks_e = (jnp.sum(len8, axis=0) + per_block - 1) // per_block
    ends_e = jnp.cumsum(blocks_e)
    start8_e = (ends_e - blocks_e) * per_block
    dst8 = start8_e[None, :] + jnp.cumsum(len8, axis=0) - len8
    nb_used = ends_e[-1:]
    block_e = jnp.minimum(jnp.sum(ends_e[None, :] <= jnp.arange(n_blocks, dtype=I32)[:, None], axis=1),
                          N_EXPERTS - 1).astype(I32)
    return (off8.reshape(-1).astype(I32), len8.reshape(-1).astype(I32),
            dst8.reshape(-1).astype(I32), tot8.astype(I32), block_e, nb_used.astype(I32))


def _moe(x2, idx, gate, cnt, w1, b1, w2, b2, lg, lb, layer, split_rows=None):
    t = x2.shape[0]
    tiles = t // TOK_TILE
    max_rows = t * TOP_K + tiles * N_EXPERTS * (MOE_GROUP - 1)
    n_blocks = -(-max_rows // MOE_BLOCK) + N_EXPERTS
    off8, len8, dst8, tot8, block_e, nb_used = _moe_tables(cnt, n_blocks)
    offv = jnp.broadcast_to((off8.reshape(tiles, N_EXPERTS, 1) * MOE_GROUP).astype(F32),
                            (tiles, N_EXPERTS, LANES))
    dest_t, xs = _moe_sort_call(x2, idx.T, offv, off8, len8, dst8, n_blocks * MOE_BLOCK)
    out_rows = _moe_ffn_call(xs, block_e, nb_used, w1, b1, w2, b2, layer)
    return _moe_combine_call(x2, dest_t.T, gate, out_rows, lg, lb, off8, len8, dst8, tot8,
                             split_rows)


def _rope_tables(seq):
    rows = seq // GRID_W
    row = jnp.repeat(jnp.arange(rows, dtype=F32), GRID_W)
    col = jnp.tile(jnp.arange(GRID_W, dtype=F32), rows)
    inv_freq = ROPE_THETA ** (-jnp.arange(ROPE_AXIS_HALF, dtype=F32) / ROPE_AXIS_HALF)
    ar = row[:, None] * inv_freq
    ac = col[:, None] * inv_freq
    cos_t = jnp.concatenate([jnp.cos(ar), jnp.cos(ar), jnp.cos(ac), jnp.cos(ac)], axis=1)
    sin_t = jnp.concatenate([-jnp.sin(ar), jnp.sin(ar), -jnp.sin(ac), jnp.sin(ac)], axis=1)
    return cos_t, sin_t


def _encode(xa, xb, mem, p):
    seq = xa.shape[1]
    b = xa.shape[0] + xb.shape[0]
    t = b * seq
    ta = xa.shape[0] * seq
    cos_t, sin_t = _rope_tables(seq)
    x = _ln_call(xa.reshape(ta, D_MODEL), xb.reshape(t - ta, D_MODEL), p["ln_in_g"], p["ln_in_b"])
    c0 = D_ATTN_Q + 2 * D_ATTN_KV
    c1 = c0 + 3 * D_MODEL
    c2 = c1 + 2 * D_MODEL
    for l in range(DEPTH):
        w_in = p["w_in"][l].astype(BF16)
        qg, kg = p["q_norm_g"][l], p["k_norm_g"][l]
        bound = (ATTN_BOUND_MARGIN * LOG2E * HEAD_DIM ** 0.5
                 * jnp.max(jnp.abs(qg)) * jnp.max(jnp.abs(kg)))
        use_shift = bound <= ATTN_SHIFT_LIMIT
        q, k, vt = _qkv_call(x, w_in[:, :c0], cos_t, sin_t, qg, kg,
                             jnp.where(use_shift, bound, 0.0), seq)
        cb, pc = _convproj_call(x, w_in[:, c0:c1])
        gy, rx = _rnnproj_call(x, w_in[:, c1:c2])
        g = _gateproj_call(x, w_in[:, c2:], p["b_gate"][l].reshape(-1))
        ya = lax.cond(use_shift,
                      functools.partial(_attn_call, shifted=True),
                      functools.partial(_attn_call, shifted=False),
                      q.reshape(b, seq, -1), k.reshape(b, seq, -1), vt)
        h = _lru_call(rx.reshape(b, seq, -1), p["lru_conv_w"][l], p["lru_conv_b"][l],
                      (0.5 * p["lru_wa"][l]).astype(BF16), (0.5 * p["lru_wx"][l]).astype(BF16),
                      0.5 * p["lru_ba"][l], 0.5 * p["lru_bx"][l], p["lru_lam"][l])
        mk, mv = _kv_call(mem, p["xkv_w"][l].astype(BF16))
        rw = jnp.pad(p["router_w"][l], ((0, 0), (0, LANES - N_EXPERTS))).astype(BF16)
        rb = jnp.pad(p["router_b"][l], (0, LANES - N_EXPERTS), constant_values=NEG_BIG)
        x2, idx, gate, cnt = _mix_cross_call(
            x, cb, pc, ya.reshape(t, -1), h.reshape(t, -1), gy, g, p["conv_w"][l],
            p["w_mix_out"][l].astype(BF16), p["ln1_g"][l], p["ln1_b"][l], mk, mv,
            p["xq_w"][l].astype(BF16), p["xo_w"][l].astype(BF16), p["ln2_g"][l], p["ln2_b"][l],
            rw, rb.reshape(1, -1), seq)
        x = _moe(x2, idx[:, :TOP_K], gate[:, :TOP_K], cnt[:, 0, :N_EXPERTS],
                 p["w1"], p["b1"][:, :, None, :], p["w2"], p["b2"][:, :, None, :],
                 p["ln3_g"][l], p["ln3_b"][l], l, split_rows=ta if l == DEPTH - 1 else None)
    ya, yb = x
    return ya.reshape(xa.shape), yb.reshape(xb.shape)


def kernel(x_prompt, x_sample, mem_prompt, mem_sample, ln_in_g, ln_in_b, w_in, b_gate, q_norm_g, k_norm_g, conv_w, lru_conv_w, lru_conv_b, lru_wa, lru_ba, lru_wx, lru_bx, lru_lam, w_mix_out, ln1_g, ln1_b, xq_w, xkv_w, xo_w, ln2_g, ln2_b, router_w, router_b, w1, b1, w2, b2, ln3_g, ln3_b):
    params = dict(ln_in_g=ln_in_g, ln_in_b=ln_in_b, w_in=w_in, b_gate=b_gate, q_norm_g=q_norm_g,
                  k_norm_g=k_norm_g, conv_w=conv_w, lru_conv_w=lru_conv_w, lru_conv_b=lru_conv_b,
                  lru_wa=lru_wa, lru_ba=lru_ba, lru_wx=lru_wx, lru_bx=lru_bx, lru_lam=lru_lam,
                  w_mix_out=w_mix_out, ln1_g=ln1_g, ln1_b=ln1_b, xq_w=xq_w, xkv_w=xkv_w, xo_w=xo_w,
                  ln2_g=ln2_g, ln2_b=ln2_b, router_w=router_w, router_b=router_b, w1=w1, b1=b1,
                  w2=w2, b2=b2, ln3_g=ln3_g, ln3_b=ln3_b)
    assert x_prompt.shape[1:] == x_sample.shape[1:]
    return _encode(x_prompt, x_sample, jnp.concatenate([mem_prompt, mem_sample], axis=0), params)
```

```python
import functools

import jax
import jax.numpy as jnp
from jax import lax
from jax.experimental import pallas as pl
from jax.experimental.pallas import tpu as pltpu

F32 = jnp.float32
BF16 = jnp.bfloat16
I32 = jnp.int32
PACKED = jnp.uint32

D_MODEL = 1024
DEPTH = 2
N_HEADS = 8
N_KV_HEADS = 2
HEAD_DIM = 128
Q_GROUPS = N_HEADS // N_KV_HEADS
D_ATTN_Q = N_HEADS * HEAD_DIM
D_ATTN_KV = N_KV_HEADS * HEAD_DIM
GRID_W = 64
ROPE_AXIS_HALF = HEAD_DIM // 4
ROPE_THETA = 10000.0
QK_EPS = 1e-6
LRU_BLOCKS = 8
LRU_BLOCK_W = D_MODEL // LRU_BLOCKS
LRU_C = 8.0
MEM_HEADS = 4
MEM_HEAD_DIM = D_MODEL // MEM_HEADS
N_EXPERTS = 32
TOP_K = 4
D_FF = D_MODEL // 2
SWIGLU_LIMIT = 7.0
SWIGLU_ALPHA = 1.702
LN_EPS = 1e-5
DEEPNORM_ALPHA = (2 * DEPTH) ** 0.25

LANES = 128
SUBLANES = 8
BF16_SUBLANES = 16
VMEM_LIMIT = 56 * 1024 * 1024

PROJ_TM = 512
ATTN_TQ = 256
ATTN_TK = 512
ATTN_TK_SHIFTED = 4096
ATTN_PAIRS_PER_BODY = 8
TOK_TILE = 256
CROSS_TM = 512
LRU_CHUNK = 512
LRU_SUBSEQ = 16
LRU_UNROLL = 8
MOE_GROUP = SUBLANES
MOE_BLOCK = 512
MOE_HALF = 256
MOE_TILE_ROWS = TOK_TILE * TOP_K + N_EXPERTS * MOE_GROUP
HEAD_AUG = 2 * HEAD_DIM
ATTN_SHIFT_LIMIT = 60.0
ATTN_BOUND_MARGIN = 1.02
NEG_BIG = -1e30
LOG2E = 1.4426950408889634


def _cparams(sem, vmem=VMEM_LIMIT):
    return pltpu.CompilerParams(dimension_semantics=sem, vmem_limit_bytes=vmem)


def _layer_norm(x, g, b):
    mu = jnp.mean(x, axis=-1, keepdims=True)
    xc = x - mu
    var = jnp.mean(xc * xc, axis=-1, keepdims=True)
    return xc * lax.rsqrt(var + LN_EPS) * g + b


def _sigmoid(x):
    return 1.0 / (1.0 + jnp.exp(-x))


def _sigmoid_tanh(x):
    return 0.5 * jnp.tanh(0.5 * x) + 0.5


def _dot(a, b):
    return jnp.dot(a, b, preferred_element_type=F32)


def _dot_nt(a, b):
    return lax.dot_general(a, b, (((1,), (1,)), ((), ())), preferred_element_type=F32)


def _ln_kernel(xa_ref, xb_ref, g_ref, b_ref, o_ref, *, tiles_a):
    @pl.when(pl.program_id(0) < tiles_a)
    def _():
        o_ref[...] = _layer_norm(xa_ref[...], g_ref[...], b_ref[...])

    @pl.when(pl.program_id(0) >= tiles_a)
    def _():
        o_ref[...] = _layer_norm(xb_ref[...], g_ref[...], b_ref[...])


def _ln_call(xa, xb, g, b):
    tm = PROJ_TM
    ta, tb = xa.shape[0] // tm, xb.shape[0] // tm
    return pl.pallas_call(
        functools.partial(_ln_kernel, tiles_a=ta),
        grid=(ta + tb,),
        in_specs=[pl.BlockSpec((tm, D_MODEL), lambda i: (jnp.minimum(i, ta - 1), 0)),
                  pl.BlockSpec((tm, D_MODEL), lambda i: (jnp.maximum(i - ta, 0), 0)),
                  pl.BlockSpec((1, D_MODEL), lambda i: (0, 0)),
                  pl.BlockSpec((1, D_MODEL), lambda i: (0, 0))],
        out_specs=pl.BlockSpec((tm, D_MODEL), lambda i: (i, 0)),
        out_shape=jax.ShapeDtypeStruct(((ta + tb) * tm, D_MODEL), F32),
        compiler_params=_cparams(("arbitrary",)),
        name="ln_in",
    )(xa, xb, g.reshape(1, -1), b.reshape(1, -1))


def _rope_partner_index():
    lane = jnp.arange(HEAD_DIM)
    return jnp.where((lane % (2 * ROPE_AXIS_HALF)) < ROPE_AXIS_HALF,
                     lane + ROPE_AXIS_HALF, lane - ROPE_AXIS_HALF)


def _qkv_kernel(x_ref, w_ref, cos_ref, sin_ref, qg_ref, kg_ref, shift_ref, q_ref, k_ref, vt_ref):
    x = x_ref[...].astype(BF16)
    c = cos_ref[...]
    s = sin_ref[...]
    lane = lax.broadcasted_iota(I32, c.shape, 1)
    n_qk = D_ATTN_Q + D_ATTN_KV
    z = _dot(x, w_ref[...])

    def prep(col, g_ref, scale):
        zh = z[:, col:col + HEAD_DIM]
        zp = z[:, n_qk + D_ATTN_KV + col:n_qk + D_ATTN_KV + col + HEAD_DIM]
        r = lax.rsqrt(jnp.mean(zh * zh, axis=-1, keepdims=True) + QK_EPS)
        return ((zh * (r * g_ref[0:1, :])) * c + (zp * (r * g_ref[1:2, :])) * s) * scale

    q_extra = jnp.broadcast_to(shift_ref[...], c.shape).astype(q_ref.dtype)
    k_extra = jnp.where(lane == 0, 1.0, 0.0).astype(k_ref.dtype)
    for h in range(N_HEADS):
        q_ref[:, h * HEAD_AUG:h * HEAD_AUG + HEAD_DIM] = prep(
            h * HEAD_DIM, qg_ref, LOG2E * HEAD_DIM ** -0.5).astype(q_ref.dtype)
        q_ref[:, h * HEAD_AUG + HEAD_DIM:(h + 1) * HEAD_AUG] = q_extra
    for h in range(N_KV_HEADS):
        k_ref[:, h * HEAD_AUG:h * HEAD_AUG + HEAD_DIM] = prep(
            D_ATTN_Q + h * HEAD_DIM, kg_ref, 1.0).astype(k_ref.dtype)
        k_ref[:, h * HEAD_AUG + HEAD_DIM:(h + 1) * HEAD_AUG] = k_extra
    vt_ref[0] = z[:, n_qk:n_qk + D_ATTN_KV].T.astype(vt_ref.dtype)


def _qkv_call(x, w, cos_t, sin_t, qg, kg, shift, seq):
    t = x.shape[0]
    tm = PROJ_TM
    nseq = seq // tm
    shift_row = jnp.zeros((1, HEAD_DIM), F32).at[0, 0].set(-shift)
    partner = _rope_partner_index()
    n_qk = D_ATTN_Q + D_ATTN_KV
    w_partner = w[:, :n_qk].reshape(D_MODEL, -1, 2, ROPE_AXIS_HALF)[:, :, ::-1, :].reshape(D_MODEL, n_qk)
    w = jnp.concatenate([w, w_partner], axis=1)
    n = w.shape[1]
    qg = jnp.stack([qg, qg[partner]])
    kg = jnp.stack([kg, kg[partner]])
    vec_spec = pl.BlockSpec((1, HEAD_DIM), lambda i: (0, 0))
    gain_spec = pl.BlockSpec((2, HEAD_DIM), lambda i: (0, 0))
    return pl.pallas_call(
        _qkv_kernel,
        grid=(t // tm,),
        in_specs=[pl.BlockSpec((tm, D_MODEL), lambda i: (i, 0)),
                  pl.BlockSpec((D_MODEL, n), lambda i: (0, 0)),
                  pl.BlockSpec((tm, HEAD_DIM), lambda i: (i % nseq, 0)),
                  pl.BlockSpec((tm, HEAD_DIM), lambda i: (i % nseq, 0)),
                  gain_spec, gain_spec, vec_spec],
        out_specs=[pl.BlockSpec((tm, N_HEADS * HEAD_AUG), lambda i: (i, 0)),
                   pl.BlockSpec((tm, N_KV_HEADS * HEAD_AUG), lambda i: (i, 0)),
                   pl.BlockSpec((1, D_ATTN_KV, tm), lambda i: (i // nseq, 0, i % nseq))],
        out_shape=[jax.ShapeDtypeStruct((t, N_HEADS * HEAD_AUG), BF16),
                   jax.ShapeDtypeStruct((t, N_KV_HEADS * HEAD_AUG), BF16),
                   jax.ShapeDtypeStruct((t // seq, D_ATTN_KV, seq), BF16)],
        compiler_params=_cparams(("parallel",)),
        name="qkv_proj",
    )(x, w, cos_t, sin_t, qg, kg, shift_row)


def _convproj_kernel(x_ref, w_ref, cb_ref, p_ref):
    x = x_ref[...].astype(BF16)
    z = _dot(x, w_ref[...])
    cb_ref[...] = z[:, :D_MODEL].astype(cb_ref.dtype)
    p_ref[...] = (z[:, D_MODEL:2 * D_MODEL] * z[:, 2 * D_MODEL:]).astype(p_ref.dtype)


def _convproj_call(x, w):
    t = x.shape[0]
    tm = PROJ_TM
    return pl.pallas_call(
        _convproj_kernel,
        grid=(t // tm,),
        in_specs=[pl.BlockSpec((tm, D_MODEL), lambda i: (i, 0)),
                  pl.BlockSpec((D_MODEL, 3 * D_MODEL), lambda i: (0, 0))],
        out_specs=[pl.BlockSpec((tm, D_MODEL), lambda i: (i, 0)),
                   pl.BlockSpec((tm, D_MODEL), lambda i: (i, 0))],
        out_shape=[jax.ShapeDtypeStruct((t, D_MODEL), BF16),
                   jax.ShapeDtypeStruct((t, D_MODEL), BF16)],
        compiler_params=_cparams(("parallel",)),
        name="conv_proj",
    )(x, w)


def _rnnproj_kernel(x_ref, w_ref, gy_ref, rx_ref):
    x = x_ref[...].astype(BF16)
    z = _dot(x, w_ref[...])
    gy_ref[...] = jax.nn.gelu(z[:, :D_MODEL]).astype(gy_ref.dtype)
    rx_ref[...] = z[:, D_MODEL:]


def _rnnproj_call(x, w):
    t = x.shape[0]
    tm = PROJ_TM
    return pl.pallas_call(
        _rnnproj_kernel,
        grid=(t // tm,),
        in_specs=[pl.BlockSpec((tm, D_MODEL), lambda i: (i, 0)),
                  pl.BlockSpec((D_MODEL, 2 * D_MODEL), lambda i: (0, 0))],
        out_specs=[pl.BlockSpec((tm, D_MODEL), lambda i: (i, 0)),
                   pl.BlockSpec((tm, D_MODEL), lambda i: (i, 0))],
        out_shape=[jax.ShapeDtypeStruct((t, D_MODEL), BF16),
                   jax.ShapeDtypeStruct((t, D_MODEL), F32)],
        compiler_params=_cparams(("parallel",)),
        name="rnn_proj",
    )(x, w)


def _gateproj_kernel(x_ref, w_ref, b_ref, g_ref):
    x = x_ref[...].astype(BF16)
    z = _dot(x, w_ref[...]) + b_ref[...]
    g_ref[...] = _sigmoid(z).astype(g_ref.dtype)


def _gateproj_call(x, w, b):
    t = x.shape[0]
    tm = PROJ_TM
    return pl.pallas_call(
        _gateproj_kernel,
        grid=(t // tm,),
        in_specs=[pl.BlockSpec((tm, D_MODEL), lambda i: (i, 0)),
                  pl.BlockSpec((D_MODEL, 3 * D_MODEL), lambda i: (0, 0)),
                  pl.BlockSpec((1, 3 * D_MODEL), lambda i: (0, 0))],
        out_specs=pl.BlockSpec((tm, 3 * D_MODEL), lambda i: (i, 0)),
        out_shape=jax.ShapeDtypeStruct((t, 3 * D_MODEL), BF16),
        compiler_params=_cparams(("parallel",)),
        name="gate_proj",
    )(x, w, b.reshape(1, -1))


def _attn_kernel(q_ref, k_ref, vt_ref, o_ref, acc_s, s0_s, s1_s, p0_s, p1_s, *, tk):
    tq = q_ref.shape[1]
    n = Q_GROUPS * tq
    nk = k_ref.shape[1] // tk
    qs = _stack_heads(q_ref)
    acc_s[...] = jnp.zeros(acc_s.shape, F32)
    s_bufs = (s0_s, s1_s)
    p_bufs = (p0_s, p1_s)

    def scores(j, slot):
        start = pl.multiple_of(j * tk, tk)
        s_bufs[slot][...] = _dot_nt(k_ref[0, pl.ds(start, tk), :], qs)

    def softmax(slot, m, l):
        st = s_bufs[slot][...]
        m_new = jnp.maximum(m, jnp.max(st, axis=0, keepdims=True))
        alpha = jnp.exp2(m - m_new)
        p = jnp.exp2(st - m_new)
        p_bufs[slot][...] = p.astype(BF16)
        return m_new, alpha * l + jnp.sum(p, axis=0, keepdims=True), alpha

    def accumulate(j, slot, alpha):
        start = pl.multiple_of(j * tk, tk)
        acc_s[...] = alpha * acc_s[...] + _dot(vt_ref[0, :, pl.ds(start, tk)], p_bufs[slot][...])

    def stage(j, m, l, alphas, first, last):
        new_alphas = []
        for slot in range(2):
            if not first:
                accumulate(j - 2 + slot, slot, alphas[slot])
            m, l, a = softmax(slot, m, l)
            new_alphas.append(a)
            if not last:
                scores(j + 2 + slot, slot)
        return m, l, tuple(new_alphas)

    m = jnp.full((1, n), NEG_BIG, F32)
    l = jnp.zeros((1, n), F32)
    scores(0, 0)
    scores(1, 1)
    if nk == 2:
        m, l, alphas = stage(0, m, l, None, True, True)
    else:
        m, l, alphas = stage(0, m, l, None, True, False)
        m, l, alphas = lax.fori_loop(
            1, nk // 2 - 1, lambda i, c: stage(2 * i, c[0], c[1], c[2], False, False), (m, l, alphas))
        m, l, alphas = stage(nk - 2, m, l, alphas, False, True)
    accumulate(nk - 2, 0, alphas[0])
    accumulate(nk - 1, 1, alphas[1])
    _write_heads(o_ref, acc_s[...] / l)


def _stack_heads(q_ref):
    return jnp.concatenate([q_ref[0, :, g * HEAD_AUG:(g + 1) * HEAD_AUG] for g in range(Q_GROUPS)],
                           axis=0)


def _write_heads(o_ref, ot):
    tq = o_ref.shape[1]
    for g in range(Q_GROUPS):
        o_ref[0, :, g * HEAD_DIM:(g + 1) * HEAD_DIM] = ot[:, g * tq:(g + 1) * tq].T.astype(o_ref.dtype)


def _attn_shifted_kernel(q_ref, k_ref, vt_ref, o_ref, acc_s, p0_s, p1_s, *, tk):
    n = Q_GROUPS * q_ref.shape[1]
    nk = k_ref.shape[1] // tk
    qs = _stack_heads(q_ref)
    acc_s[...] = jnp.zeros(acc_s.shape, F32)
    p_bufs = (p0_s, p1_s)

    def probs(j, slot, l):
        start = pl.multiple_of(j * tk, tk)
        p = jnp.exp2(_dot_nt(k_ref[0, pl.ds(start, tk), :], qs))
        p_bufs[slot][...] = p.astype(BF16)
        return l + jnp.sum(p, axis=0, keepdims=True)

    def accumulate(j, slot):
        start = pl.multiple_of(j * tk, tk)
        acc_s[...] += _dot(vt_ref[0, :, pl.ds(start, tk)], p_bufs[slot][...])

    def pair(i, l):
        j = 2 * i
        l = probs(j + 1, 1, l)
        accumulate(j, 0)
        l = probs(j + 2, 0, l)
        accumulate(j + 1, 1)
        return l

    l = probs(0, 0, jnp.zeros((1, n), F32))
    n_pairs = nk // 2 - 1
    trips = n_pairs // ATTN_PAIRS_PER_BODY

    def body(i, l):
        for r in range(ATTN_PAIRS_PER_BODY):
            l = pair(i * ATTN_PAIRS_PER_BODY + r, l)
        return l

    if trips > 1:
        l = lax.fori_loop(0, trips, body, l)
    else:
        trips = 0
    for i in range(trips * ATTN_PAIRS_PER_BODY, n_pairs):
        l = pair(i, l)
    l = probs(nk - 1, 1, l)
    accumulate(nk - 2, 0)
    accumulate(nk - 1, 1)
    _write_heads(o_ref, acc_s[...] / l)


def _attn_call(q, k, vt, shifted):
    b, s, _ = q.shape
    tq = min(ATTN_TQ, s)
    tk = min(ATTN_TK_SHIFTED if shifted else ATTN_TK, s // 2)
    assert (s // tk) % 2 == 0
    n = Q_GROUPS * tq
    p_bufs = [pltpu.VMEM((tk, n), BF16)] * 2
    s_bufs = [] if shifted else [pltpu.VMEM((tk, n), F32)] * 2
    return pl.pallas_call(
        functools.partial(_attn_shifted_kernel if shifted else _attn_kernel, tk=tk),
        grid=(b, N_KV_HEADS, s // tq),
        in_specs=[pl.BlockSpec((1, tq, Q_GROUPS * HEAD_AUG), lambda bi, h, i: (bi, i, h)),
                  pl.BlockSpec((1, s, HEAD_AUG), lambda bi, h, i: (bi, 0, h)),
                  pl.BlockSpec((1, HEAD_DIM, s), lambda bi, h, i: (bi, h, 0))],
        out_specs=pl.BlockSpec((1, tq, Q_GROUPS * HEAD_DIM), lambda bi, h, i: (bi, i, h)),
        out_shape=jax.ShapeDtypeStruct((b, s, D_ATTN_Q), BF16),
        scratch_shapes=[pltpu.VMEM((HEAD_DIM, n), F32)] + s_bufs + p_bufs,
        compiler_params=_cparams(("parallel", "parallel", "parallel")),
        name="flash_attn_shifted" if shifted else "flash_attn",
    )(q, k, vt)


def _lru_kernel(x_ref, cw_ref, cb_ref, wa_ref, wx_ref, ba_ref, bx_ref, lam_ref, o_hbm,
                xp_s, a0_s, u0_s, a1_s, u1_s, h_s, sem, *, seq, chunk):
    sub = seq // LRU_SUBSEQ
    cps = sub // chunk
    nchunks = seq // chunk

    zeros8 = jnp.zeros((SUBLANES, LANES), F32)
    xp_s[0:SUBLANES, :] = zeros8
    xp_s[seq + SUBLANES:seq + 2 * SUBLANES, :] = zeros8
    xp_s[SUBLANES:seq + SUBLANES, :] = x_ref[0]
    cw = cw_ref[...]
    cb = cb_ref[...]
    a_bufs = (a0_s, a1_s)
    u_bufs = (u0_s, u1_s)

    neg = -lam_ref[...]
    softplus = jnp.maximum(neg, 0.0) + jnp.log1p(jnp.exp(-jnp.abs(neg)))
    half_decay = -0.5 * LRU_C * softplus

    def gate_chunk(c, _):
        base = pl.multiple_of(c * chunk, chunk)
        xc = cb
        for k in range(4):
            xc = xc + cw[k:k + 1, :] * xp_s[pl.ds(base + SUBLANES - 1 + k, chunk), :]
        xb = xc.astype(BF16)
        xh = 0.5 * xc
        dst = pl.ds((c % cps) * (chunk * LRU_SUBSEQ) + c // cps, chunk, stride=LRU_SUBSEQ)
        for d in range(2):
            tr = jnp.tanh(_dot(xb, wa_ref[d, 0]) + ba_ref[d:d + 1, :])
            ti = jnp.tanh(_dot(xb, wx_ref[d, 0]) + bx_ref[d:d + 1, :])
            a = jnp.exp(half_decay[d:d + 1, :] * tr + half_decay[d:d + 1, :])
            u_bufs[d][dst, :] = jnp.sqrt(1.0 - a * a) * (xh * (ti + 1.0))
            a_bufs[d][dst, :] = a
        return 0

    lax.fori_loop(0, nchunks, gate_chunk, 0)

    def step(tt, carry):
        out = []
        for d in range(2):
            h, p = carry[d]
            t = tt if d == 0 else sub - 1 - tt
            idx = pl.ds(pl.multiple_of(t * LRU_SUBSEQ, LRU_SUBSEQ), LRU_SUBSEQ)
            av = a_bufs[d][idx, :]
            h = av * h + u_bufs[d][idx, :]
            p = av * p
            u_bufs[d][idx, :] = h
            a_bufs[d][idx, :] = p
            out.append((h, p))
        return tuple(out)

    h0 = jnp.zeros((LRU_SUBSEQ, LANES), F32)
    p0 = jnp.ones((LRU_SUBSEQ, LANES), F32)
    ends = lax.fori_loop(0, sub, step, ((h0, p0), (h0, p0)), unroll=LRU_UNROLL)

    carry_in = []
    for d in range(2):
        h_end, p_end = ends[d]
        order = range(LRU_SUBSEQ) if d == 0 else range(LRU_SUBSEQ - 1, -1, -1)
        states = [None] * LRU_SUBSEQ
        state = jnp.zeros((1, LANES), F32)
        for j in order:
            states[j] = state
            state = p_end[j:j + 1, :] * state + h_end[j:j + 1, :]
        carry_in.append(jnp.concatenate(states, axis=0))

    step_id = pl.program_id(0) * pl.num_programs(1) + pl.program_id(1)
    last_id = pl.num_programs(0) * pl.num_programs(1) - 1

    def out_copies(bi, ci):
        return [pltpu.make_async_copy(
            h_s.at[:, j, :],
            o_hbm.at[bi, pl.ds(j * sub, sub), pl.ds(pl.multiple_of(ci * LANES, LANES), LANES)],
            sem) for j in range(LRU_SUBSEQ)]

    @pl.when(step_id > 0)
    def _():
        for cp in out_copies(0, 0):
            cp.wait()

    steps = min(sub, chunk)
    for c in range(sub // steps):
        rows = slice(c * steps * LRU_SUBSEQ, (c + 1) * steps * LRU_SUBSEQ)
        c0 = jnp.tile(carry_in[0], (steps, 1))
        c1 = jnp.tile(carry_in[1], (steps, 1))
        val = u0_s[rows, :] + a0_s[rows, :] * c0 + u1_s[rows, :] + a1_s[rows, :] * c1
        h_s[c * steps:(c + 1) * steps] = val.reshape(steps, LRU_SUBSEQ, LANES)

    for cp in out_copies(pl.program_id(0), pl.program_id(1)):
        cp.start()

    @pl.when(step_id == last_id)
    def _():
        for cp in out_copies(0, 0):
            cp.wait()


def _lru_call(rx, cw, cb, wa, wx, ba, bx, lam):
    b, s, _ = rx.shape
    chunk = min(LRU_CHUNK, s // LRU_SUBSEQ)
    assert LRU_SUBSEQ % SUBLANES == 0
    w = LRU_BLOCK_W
    return pl.pallas_call(
        functools.partial(_lru_kernel, seq=s, chunk=chunk),
        grid=(b, LRU_BLOCKS),
        in_specs=[pl.BlockSpec((1, s, w), lambda bi, c: (bi, 0, c)),
                  pl.BlockSpec((4, w), lambda bi, c: (0, c)),
                  pl.BlockSpec((1, w), lambda bi, c: (0, c)),
                  pl.BlockSpec((2, 1, w, w), lambda bi, c: (0, c, 0, 0)),
                  pl.BlockSpec((2, 1, w, w), lambda bi, c: (0, c, 0, 0)),
                  pl.BlockSpec((2, w), lambda bi, c: (0, c)),
                  pl.BlockSpec((2, w), lambda bi, c: (0, c)),
                  pl.BlockSpec((2, w), lambda bi, c: (0, c))],
        out_specs=pl.BlockSpec(memory_space=pl.ANY),
        out_shape=jax.ShapeDtypeStruct((b, s, D_MODEL), F32),
        scratch_shapes=[pltpu.VMEM((s + 2 * SUBLANES, w), F32)]
        + [pltpu.VMEM((s, w), F32)] * 4
        + [pltpu.VMEM((s // LRU_SUBSEQ, LRU_SUBSEQ, w), F32), pltpu.SemaphoreType.DMA(())],
        compiler_params=_cparams(("arbitrary", "arbitrary")),
        name="rg_lru",
    )(rx, cw, cb.reshape(1, -1), wa, wx, ba, bx, lam)


def _kv_kernel(m_ref, w_ref, k_ref, v_ref):
    z = _dot(m_ref[0].astype(BF16), w_ref[...])
    k_ref[0] = z[:, :D_MODEL].astype(k_ref.dtype)
    v_ref[0] = z[:, D_MODEL:].astype(v_ref.dtype)


def _kv_call(mem, w):
    b, m, _ = mem.shape
    return pl.pallas_call(
        _kv_kernel,
        grid=(b,),
        in_specs=[pl.BlockSpec((1, m, D_MODEL), lambda i: (i, 0, 0)),
                  pl.BlockSpec((D_MODEL, 2 * D_MODEL), lambda i: (0, 0))],
        out_specs=[pl.BlockSpec((1, m, D_MODEL), lambda i: (i, 0, 0)),
                   pl.BlockSpec((1, m, D_MODEL), lambda i: (i, 0, 0))],
        out_shape=[jax.ShapeDtypeStruct((b, m, D_MODEL), BF16),
                   jax.ShapeDtypeStruct((b, m, D_MODEL), BF16)],
        compiler_params=_cparams(("parallel",)),
        name="mem_kv_proj",
    )(mem, w)


def _mix_cross_kernel(x_ref, cb_ref, p_ref, pprev_ref, pnext_ref, ya_ref, h_ref, gy_ref, g_ref,
                      cw_ref, wm_ref, l1g_ref, l1b_ref, k_ref, v_ref, wq_ref, wo_ref, l2g_ref,
                      l2b_ref, rw_ref, rb_ref, x2_ref, idx_ref, gate_ref, cnt_ref, *, seq):
    tm = x_ref.shape[0]
    t0 = pl.program_id(0) * tm
    has_prev = ((t0 % seq) != 0).astype(F32)
    has_next = (((t0 + tm) % seq) != 0).astype(F32)
    cw = cw_ref[...]
    row = lax.broadcasted_iota(I32, (TOK_TILE, D_MODEL), 0)
    halves = tm // TOK_TILE
    for r in range(halves):
        rows = slice(r * TOK_TILE, (r + 1) * TOK_TILE)
        lo, hi = r * TOK_TILE, (r + 1) * TOK_TILE
        prev_row = (pprev_ref[BF16_SUBLANES - 1:BF16_SUBLANES, :].astype(F32) * has_prev if r == 0
                    else p_ref[lo - 1:lo, :].astype(F32))
        next_row = (pnext_ref[0:1, :].astype(F32) * has_next if r == halves - 1
                    else p_ref[hi:hi + 1, :].astype(F32))
        p = p_ref[rows, :].astype(F32)
        p_dn = jnp.where(row == 0, prev_row, pltpu.roll(p, 1, 0))
        p_up = jnp.where(row == TOK_TILE - 1, next_row, pltpu.roll(p, TOK_TILE - 1, 0))
        conv = cw[0:1, :] * p_dn + cw[1:2, :] * p + cw[2:3, :] * p_up
        y_conv = cb_ref[rows, :].astype(F32) * conv
        y_rnn = h_ref[rows, :] * gy_ref[rows, :].astype(F32)
        merged = (g_ref[rows, :D_MODEL].astype(F32) * y_conv
                  + g_ref[rows, D_MODEL:2 * D_MODEL].astype(F32) * ya_ref[rows, :].astype(F32)
                  + g_ref[rows, 2 * D_MODEL:].astype(F32) * y_rnn)
        mix = _dot(merged.astype(BF16), wm_ref[...])
        x1 = _layer_norm(DEEPNORM_ALPHA * x_ref[rows, :] + mix, l1g_ref[...], l1b_ref[...])
        _cross_rows(x1, k_ref, v_ref, wq_ref, wo_ref, l2g_ref, l2b_ref, rw_ref, rb_ref,
                    x2_ref.at[rows, :], idx_ref.at[rows, :], gate_ref.at[rows, :], cnt_ref.at[r])


def _cross_rows(x, k_ref, v_ref, wq_ref, wo_ref, lg_ref, lb_ref, rw_ref, rb_ref,
                x2_ref, idx_ref, gate_ref, cnt_ref):
    q = _dot(x.astype(BF16), wq_ref[...]) * (MEM_HEAD_DIM ** -0.5)
    outs = []
    for h in range(MEM_HEADS):
        sl = slice(h * MEM_HEAD_DIM, (h + 1) * MEM_HEAD_DIM)
        s = _dot_nt(q[:, sl].astype(BF16), k_ref[0, :, sl])
        m = jnp.max(s, axis=-1, keepdims=True)
        p = jnp.exp(s - m)
        p = p / jnp.sum(p, axis=-1, keepdims=True)
        outs.append(_dot(p.astype(BF16), v_ref[0, :, sl]))
    o = jnp.concatenate(outs, axis=1)
    cross = _dot(o.astype(BF16), wo_ref[...])
    x2 = _layer_norm(DEEPNORM_ALPHA * x + cross, lg_ref[...], lb_ref[...])
    x2_ref[...] = x2

    logits = _dot(x2.astype(BF16), rw_ref[...]) + rb_ref[...]
    lane = lax.broadcasted_iota(I32, logits.shape, 1)
    idx_acc = jnp.zeros(logits.shape, I32)
    val_acc = jnp.zeros(logits.shape, F32)
    member = jnp.zeros(logits.shape, F32)
    top0 = None
    for k in range(TOP_K):
        m = jnp.max(logits, axis=-1, keepdims=True)
        pick = jnp.min(jnp.where(logits == m, lane, LANES), axis=-1, keepdims=True)
        hit = lane == pick
        if k == 0:
            top0 = m
        idx_acc = jnp.where(lane == k, pick, idx_acc)
        val_acc = jnp.where(lane == k, jnp.exp(m - top0), val_acc)
        member = member + hit.astype(F32)
        logits = jnp.where(hit, NEG_BIG, logits)
    idx_ref[...] = idx_acc
    gate_ref[...] = val_acc / jnp.sum(val_acc, axis=-1, keepdims=True)
    cnt_ref[...] = jnp.sum(member, axis=0, keepdims=True).astype(I32)


def _mix_cross_call(x, cb, p, ya, h, gy, g, cw, wm, l1g, l1b, k, v, wq, wo, l2g, l2b, rw, rb, seq):
    t = x.shape[0]
    tm = CROSS_TM
    halves = tm // TOK_TILE
    per_seq = seq // tm
    m = k.shape[1]
    hb = BF16_SUBLANES
    per = tm // hb
    last = t // hb - 1
    row_spec = pl.BlockSpec((tm, D_MODEL), lambda i: (i, 0))
    vec_spec = pl.BlockSpec((1, D_MODEL), lambda i: (0, 0))
    mat_spec = pl.BlockSpec((D_MODEL, D_MODEL), lambda i: (0, 0))
    mem_spec = pl.BlockSpec((1, m, D_MODEL), lambda i: (i // per_seq, 0, 0))
    lane_spec = pl.BlockSpec((tm, LANES), lambda i: (i, 0))
    return pl.pallas_call(
        functools.partial(_mix_cross_kernel, seq=seq),
        grid=(t // tm,),
        in_specs=[row_spec, row_spec, row_spec,
                  pl.BlockSpec((hb, D_MODEL), lambda i: (jnp.maximum(i * per - 1, 0), 0)),
                  pl.BlockSpec((hb, D_MODEL), lambda i: (jnp.minimum((i + 1) * per, last), 0)),
                  row_spec, row_spec, row_spec,
                  pl.BlockSpec((tm, 3 * D_MODEL), lambda i: (i, 0)),
                  pl.BlockSpec((3, D_MODEL), lambda i: (0, 0)),
                  mat_spec, vec_spec, vec_spec,
                  mem_spec, mem_spec, mat_spec, mat_spec, vec_spec, vec_spec,
                  pl.BlockSpec((D_MODEL, LANES), lambda i: (0, 0)),
                  pl.BlockSpec((1, LANES), lambda i: (0, 0))],
        out_specs=[row_spec, lane_spec, lane_spec,
                   pl.BlockSpec((halves, 1, LANES), lambda i: (i, 0, 0))],
        out_shape=[jax.ShapeDtypeStruct((t, D_MODEL), F32),
                   jax.ShapeDtypeStruct((t, LANES), I32),
                   jax.ShapeDtypeStruct((t, LANES), F32),
                   jax.ShapeDtypeStruct((t // TOK_TILE, 1, LANES), I32)],
        compiler_params=_cparams(("parallel",)),
        name="mix_cross_router",
    )(x, cb, p, p, p, ya, h, gy, g, cw, wm, l1g.reshape(1, -1), l1b.reshape(1, -1), k, v, wq, wo,
      l2g.reshape(1, -1), l2b.reshape(1, -1), rw, rb)


def _pack_rows(v):
    bits = pltpu.bitcast(v, PACKED)
    half = v.shape[1] // 2
    return (bits[:, :half] & jnp.uint32(0xFFFF0000)) | (bits[:, half:] >> 16)


def _unpack_rows(u):
    hi = pltpu.bitcast(u & jnp.uint32(0xFFFF0000), F32).astype(BF16)
    lo = pltpu.bitcast(u << 16, F32).astype(BF16)
    return hi, lo


def _groups_copy(src, dst, sem, s_grp, d_grp, n_grp):
    rows = n_grp * MOE_GROUP
    return pltpu.make_async_copy(
        src.at[pl.ds(pl.multiple_of(s_grp * MOE_GROUP, MOE_GROUP), rows), :],
        dst.at[pl.ds(pl.multiple_of(d_grp * MOE_GROUP, MOE_GROUP), rows), :], sem)


def _moe_sort_kernel(off8_ref, len8_ref, dst8_ref, x_ref, idx_ref, offv_ref, dest_ref, xs_hbm,
                     xs_s, sem, pending_s):
    s = pl.program_id(0)
    tm = x_ref.shape[0]
    idx = idx_ref[...]
    e_iota = lax.broadcasted_iota(I32, (N_EXPERTS, tm), 0)
    member = jnp.zeros((N_EXPERTS, tm), F32)
    for k in range(TOP_K):
        member = member + (e_iota == idx[k:k + 1, :]).astype(F32)
    earlier = (lax.broadcasted_iota(I32, (tm, tm), 0)
               < lax.broadcasted_iota(I32, (tm, tm), 1)).astype(BF16)
    rank = _dot(member.astype(BF16), earlier)
    pos = rank + offv_ref[0][:, 0:1]
    r_iota = lax.broadcasted_iota(I32, (MOE_TILE_ROWS, tm), 0)
    onehot = jnp.zeros((MOE_TILE_ROWS, tm), F32)
    dests = []
    for k in range(TOP_K):
        dk = jnp.sum(jnp.where(e_iota == idx[k:k + 1, :], pos, 0.0), axis=0, keepdims=True)
        dk = dk.astype(I32)
        dests.append(dk)
        onehot = jnp.where(r_iota == dk, 1.0, onehot)
    dest_ref[...] = jnp.concatenate(dests, axis=0)
    slot = s % 2
    xs_s[slot] = _pack_rows(_dot(onehot.astype(BF16), x_ref[...].astype(BF16)))

    def per_expert(e, total):
        n = len8_ref[s * N_EXPERTS + e]

        @pl.when(n > 0)
        def _():
            _groups_copy(xs_s.at[slot], xs_hbm, sem.at[slot], off8_ref[s * N_EXPERTS + e],
                         dst8_ref[s * N_EXPERTS + e], n).start()

        return total + n

    total = lax.fori_loop(0, N_EXPERTS, per_expert, 0)

    def drain(which, count):
        @pl.when(count > 0)
        def _():
            _groups_copy(xs_s.at[which], xs_hbm, sem.at[which], 0, 0, count).wait()

    @pl.when(s > 0)
    def _():
        drain(1 - slot, pending_s[0])

    pending_s[0] = total

    @pl.when(s == pl.num_programs(0) - 1)
    def _():
        drain(slot, total)


def _moe_sort_call(x2, idx_t, offv, off8, len8, dst8, n_rows):
    t = x2.shape[0]
    tm = TOK_TILE
    grid_spec = pltpu.PrefetchScalarGridSpec(
        num_scalar_prefetch=3,
        grid=(t // tm,),
        in_specs=[pl.BlockSpec((tm, D_MODEL), lambda i, *_: (i, 0)),
                  pl.BlockSpec((TOP_K, tm), lambda i, *_: (0, i)),
                  pl.BlockSpec((1, N_EXPERTS, LANES), lambda i, *_: (i, 0, 0))],
        out_specs=[pl.BlockSpec((TOP_K, tm), lambda i, *_: (0, i)),
                   pl.BlockSpec(memory_space=pl.ANY)],
        scratch_shapes=[pltpu.VMEM((2, MOE_TILE_ROWS, D_MODEL // 2), PACKED),
                        pltpu.SemaphoreType.DMA((2,)),
                        pltpu.SMEM((1,), I32)],
    )
    return pl.pallas_call(
        _moe_sort_kernel,
        grid_spec=grid_spec,
        out_shape=[jax.ShapeDtypeStruct((TOP_K, t), I32),
                   jax.ShapeDtypeStruct((n_rows, D_MODEL // 2), PACKED)],
        compiler_params=_cparams(("arbitrary",)),
        name="moe_sort",
    )(off8, len8, dst8, x2, idx_t, offv)


def _moe_ffn_kernel(be_ref, nb_ref, xs_ref, w1_ref, b1_ref, w2_ref, b2_ref, o_ref, w1_s, w2_s):
    i = pl.program_id(0)

    @pl.when((i == 0) | (be_ref[i] != be_ref[jnp.maximum(i - 1, 0)]))
    def _():
        w1_s[...] = w1_ref[0].astype(BF16)
        w2_s[...] = w2_ref[0].astype(BF16)

    @pl.when(i < nb_ref[0])
    def _():
        half = D_MODEL // 2
        for r in range(MOE_BLOCK // MOE_HALF):
            rows = slice(r * MOE_HALF, (r + 1) * MOE_HALF)
            hi, lo = _unpack_rows(xs_ref[rows, :])
            hcat = _dot(hi, w1_s[:half, :]) + _dot(lo, w1_s[half:, :]) + b1_ref[0]
            glu = jnp.minimum(hcat[:, :D_FF], SWIGLU_LIMIT)
            lin = jnp.clip(hcat[:, D_FF:], -SWIGLU_LIMIT, SWIGLU_LIMIT)
            act = glu * _sigmoid_tanh(SWIGLU_ALPHA * glu) * (lin + 1.0)
            out = _dot(act.astype(BF16), w2_s[...]) + b2_ref[0]
            o_ref[rows, :] = _pack_rows(out.astype(BF16).astype(F32))


def _moe_ffn_call(xs, block_e, nb_used, w1, b1, w2, b2, layer):
    n_rows = xs.shape[0]
    nb = n_rows // MOE_BLOCK

    def blk(i, be, nbu):
        return (jnp.minimum(i, nbu[0] - 1), 0)

    def expert(i, be, nbu):
        return (layer, be[i], 0, 0)

    grid_spec = pltpu.PrefetchScalarGridSpec(
        num_scalar_prefetch=2,
        grid=(nb,),
        in_specs=[pl.BlockSpec((MOE_BLOCK, D_MODEL // 2), blk),
                  pl.BlockSpec((None, 1, D_MODEL, 2 * D_FF), expert),
                  pl.BlockSpec((None, 1, 1, 2 * D_FF), expert),
                  pl.BlockSpec((None, 1, D_FF, D_MODEL), expert),
                  pl.BlockSpec((None, 1, 1, D_MODEL), expert)],
        out_specs=pl.BlockSpec((MOE_BLOCK, D_MODEL // 2), blk),
        scratch_shapes=[pltpu.VMEM((D_MODEL, 2 * D_FF), BF16), pltpu.VMEM((D_FF, D_MODEL), BF16)],
    )
    return pl.pallas_call(
        _moe_ffn_kernel,
        grid_spec=grid_spec,
        out_shape=jax.ShapeDtypeStruct((n_rows, D_MODEL // 2), PACKED),
        compiler_params=_cparams(("arbitrary",)),
        name="moe_ffn",
    )(block_e, nb_used, xs, w1, b1, w2, b2)


def _moe_combine_kernel(off8_ref, len8_ref, dst8_ref, tot8_ref, x_ref, dest_ref, gate_ref,
                        lg_ref, lb_ref, out_hbm, *refs, tiles_a):
    o_refs, (buf, sem) = refs[:-2], refs[-2:]
    s = pl.program_id(0)
    tm = x_ref.shape[0]
    slot = s % 2

    def gather(tile):
        which = tile % 2

        def per_expert(e, _):
            n = len8_ref[tile * N_EXPERTS + e]

            @pl.when(n > 0)
            def _():
                _groups_copy(out_hbm, buf.at[which], sem.at[which], dst8_ref[tile * N_EXPERTS + e],
                             off8_ref[tile * N_EXPERTS + e], n).start()

            return 0

        lax.fori_loop(0, N_EXPERTS, per_expert, 0)

    @pl.when(s == 0)
    def _():
        gather(s)

    @pl.when(s + 1 < pl.num_programs(0))
    def _():
        gather(s + 1)

    def clear(g, _):
        buf[slot, pl.ds(pl.multiple_of(g * MOE_GROUP, MOE_GROUP), MOE_GROUP), :] = jnp.zeros(
            (MOE_GROUP, D_MODEL // 2), PACKED)
        return 0

    lax.fori_loop(tot8_ref[s], MOE_TILE_ROWS // MOE_GROUP, clear, 0)

    dest = dest_ref[...]
    gate = gate_ref[...]
    r_iota = lax.broadcasted_iota(I32, (tm, MOE_TILE_ROWS), 1)
    weights = jnp.zeros((tm, MOE_TILE_ROWS), F32)
    for k in range(TOP_K):
        weights = jnp.where(r_iota == dest[:, k:k + 1], gate[:, k:k + 1], weights)

    _groups_copy(out_hbm, buf.at[slot], sem.at[slot], 0, 0, tot8_ref[s]).wait()
    hi, lo = _unpack_rows(buf[slot])
    wb = weights.astype(BF16)
    ff = jnp.concatenate([_dot(wb, hi), _dot(wb, lo)], axis=1)
    y = _layer_norm(DEEPNORM_ALPHA * x_ref[...] + ff, lg_ref[...], lb_ref[...])
    if tiles_a is None:
        o_refs[0][...] = y
    else:
        @pl.when(s < tiles_a)
        def _():
            o_refs[0][...] = y

        @pl.when(s >= tiles_a)
        def _():
            o_refs[1][...] = y


def _moe_combine_call(x2, dest, gate, out_rows, lg, lb, off8, len8, dst8, tot8, split_rows):
    t = x2.shape[0]
    tm = TOK_TILE
    if split_rows is None:
        tiles_a = None
        out_specs = pl.BlockSpec((tm, D_MODEL), lambda i, *_: (i, 0))
        out_shape = jax.ShapeDtypeStruct((t, D_MODEL), F32)
    else:
        tiles_a = split_rows // tm
        out_specs = [pl.BlockSpec((tm, D_MODEL), lambda i, *_: (jnp.minimum(i, tiles_a - 1), 0)),
                     pl.BlockSpec((tm, D_MODEL), lambda i, *_: (jnp.maximum(i - tiles_a, 0), 0))]
        out_shape = [jax.ShapeDtypeStruct((split_rows, D_MODEL), F32),
                     jax.ShapeDtypeStruct((t - split_rows, D_MODEL), F32)]
    grid_spec = pltpu.PrefetchScalarGridSpec(
        num_scalar_prefetch=4,
        grid=(t // tm,),
        in_specs=[pl.BlockSpec((tm, D_MODEL), lambda i, *_: (i, 0)),
                  pl.BlockSpec((tm, TOP_K), lambda i, *_: (i, 0)),
                  pl.BlockSpec((tm, TOP_K), lambda i, *_: (i, 0)),
                  pl.BlockSpec((1, D_MODEL), lambda i, *_: (0, 0)),
                  pl.BlockSpec((1, D_MODEL), lambda i, *_: (0, 0)),
                  pl.BlockSpec(memory_space=pl.ANY)],
        out_specs=out_specs,
        scratch_shapes=[pltpu.VMEM((2, MOE_TILE_ROWS, D_MODEL // 2), PACKED),
                        pltpu.SemaphoreType.DMA((2,))],
    )
    return pl.pallas_call(
        functools.partial(_moe_combine_kernel, tiles_a=tiles_a),
        grid_spec=grid_spec,
        out_shape=out_shape,
        compiler_params=_cparams(("arbitrary",)),
        name="moe_combine_ln3",
    )(off8, len8, dst8, tot8, x2, dest, gate, lg.reshape(1, -1), lb.reshape(1, -1), out_rows)


def _moe_tables(cnt, n_blocks):
    len8 = (cnt + MOE_GROUP - 1) // MOE_GROUP
    off8 = jnp.cumsum(len8, axis=1) - len8
    tot8 = jnp.sum(len8, axis=1)
    per_block = MOE_BLOCK // MOE_GROUP
    blocks_e = (jnp.sum(len8, axis=0) + per_block - 1) // per_block
    ends_e = jnp.cumsum(blocks_e)
    start8_e = (ends_e - blocks_e) * per_block
    dst8 = start8_e[None, :] + jnp.cumsum(len8, axis=0) - len8
    nb_used = ends_e[-1:]
    block_e = jnp.minimum(jnp.sum(ends_e[None, :] <= jnp.arange(n_blocks, dtype=I32)[:, None], axis=1),
                          N_EXPERTS - 1).astype(I32)
    return (off8.reshape(-1).astype(I32), len8.reshape(-1).astype(I32),
            dst8.reshape(-1).astype(I32), tot8.astype(I32), block_e, nb_used.astype(I32))


def _moe(x2, idx, gate, cnt, w1, b1, w2, b2, lg, lb, layer, split_rows=None):
    t = x2.shape[0]
    tiles = t // TOK_TILE
    max_rows = t * TOP_K + tiles * N_EXPERTS * (MOE_GROUP - 1)
    n_blocks = -(-max_rows // MOE_BLOCK) + N_EXPERTS
    off8, len8, dst8, tot8, block_e, nb_used = _moe_tables(cnt, n_blocks)
    offv = jnp.broadcast_to((off8.reshape(tiles, N_EXPERTS, 1) * MOE_GROUP).astype(F32),
                            (tiles, N_EXPERTS, LANES))
    dest_t, xs = _moe_sort_call(x2, idx.T, offv, off8, len8, dst8, n_blocks * MOE_BLOCK)
    out_rows = _moe_ffn_call(xs, block_e, nb_used, w1, b1, w2, b2, layer)
    return _moe_combine_call(x2, dest_t.T, gate, out_rows, lg, lb, off8, len8, dst8, tot8,
                             split_rows)


def _rope_tables(seq):
    rows = seq // GRID_W
    row = jnp.repeat(jnp.arange(rows, dtype=F32), GRID_W)
    col = jnp.tile(jnp.arange(GRID_W, dtype=F32), rows)
    inv_freq = ROPE_THETA ** (-jnp.arange(ROPE_AXIS_HALF, dtype=F32) / ROPE_AXIS_HALF)
    ar = row[:, None] * inv_freq
    ac = col[:, None] * inv_freq
    cos_t = jnp.concatenate([jnp.cos(ar), jnp.cos(ar), jnp.cos(ac), jnp.cos(ac)], axis=1)
    sin_t = jnp.concatenate([-jnp.sin(ar), jnp.sin(ar), -jnp.sin(ac), jnp.sin(ac)], axis=1)
    return cos_t, sin_t


def _encode(xa, xb, mem, p):
    seq = xa.shape[1]
    b = xa.shape[0] + xb.shape[0]
    t = b * seq
    ta = xa.shape[0] * seq
    cos_t, sin_t = _rope_tables(seq)
    x = _ln_call(xa.reshape(ta, D_MODEL), xb.reshape(t - ta, D_MODEL), p["ln_in_g"], p["ln_in_b"])
    c0 = D_ATTN_Q + 2 * D_ATTN_KV
    c1 = c0 + 3 * D_MODEL
    c2 = c1 + 2 * D_MODEL
    for l in range(DEPTH):
        w_in = p["w_in"][l].astype(BF16)
        qg, kg = p["q_norm_g"][l], p["k_norm_g"][l]
        bound = (ATTN_BOUND_MARGIN * LOG2E * HEAD_DIM ** 0.5
                 * jnp.max(jnp.abs(qg)) * jnp.max(jnp.abs(kg)))
        use_shift = bound <= ATTN_SHIFT_LIMIT
        q, k, vt = _qkv_call(x, w_in[:, :c0], cos_t, sin_t, qg, kg,
                             jnp.where(use_shift, bound, 0.0), seq)
        cb, pc = _convproj_call(x, w_in[:, c0:c1])
        gy, rx = _rnnproj_call(x, w_in[:, c1:c2])
        g = _gateproj_call(x, w_in[:, c2:], p["b_gate"][l].reshape(-1))
        ya = lax.cond(use_shift,
                      functools.partial(_attn_call, shifted=True),
                      functools.partial(_attn_call, shifted=False),
                      q.reshape(b, seq, -1), k.reshape(b, seq, -1), vt)
        h = _lru_call(rx.reshape(b, seq, -1), p["lru_conv_w"][l], p["lru_conv_b"][l],
                      (0.5 * p["lru_wa"][l]).astype(BF16), (0.5 * p["lru_wx"][l]).astype(BF16),
                      0.5 * p["lru_ba"][l], 0.5 * p["lru_bx"][l], p["lru_lam"][l])
        mk, mv = _kv_call(mem, p["xkv_w"][l].astype(BF16))
        rw = jnp.pad(p["router_w"][l], ((0, 0), (0, LANES - N_EXPERTS))).astype(BF16)
        rb = jnp.pad(p["router_b"][l], (0, LANES - N_EXPERTS), constant_values=NEG_BIG)
        x2, idx, gate, cnt = _mix_cross_call(
            x, cb, pc, ya.reshape(t, -1), h.reshape(t, -1), gy, g, p["conv_w"][l],
            p["w_mix_out"][l].astype(BF16), p["ln1_g"][l], p["ln1_b"][l], mk, mv,
            p["xq_w"][l].astype(BF16), p["xo_w"][l].astype(BF16), p["ln2_g"][l], p["ln2_b"][l],
            rw, rb.reshape(1, -1), seq)
        x = _moe(x2, idx[:, :TOP_K], gate[:, :TOP_K], cnt[:, 0, :N_EXPERTS],
                 p["w1"], p["b1"][:, :, None, :], p["w2"], p["b2"][:, :, None, :],
                 p["ln3_g"][l], p["ln3_b"][l], l, split_rows=ta if l == DEPTH - 1 else None)
    ya, yb = x
    return ya.reshape(xa.shape), yb.reshape(xb.shape)


def kernel(x_prompt, x_sample, mem_prompt, mem_sample, ln_in_g, ln_in_b, w_in, b_gate, q_norm_g, k_norm_g, conv_w, lru_conv_w, lru_conv_b, lru_wa, lru_ba, lru_wx, lru_bx, lru_lam, w_mix_out, ln1_g, ln1_b, xq_w, xkv_w, xo_w, ln2_g, ln2_b, router_w, router_b, w1, b1, w2, b2, ln3_g, ln3_b):
    params = dict(ln_in_g=ln_in_g, ln_in_b=ln_in_b, w_in=w_in, b_gate=b_gate, q_norm_g=q_norm_g,
                  k_norm_g=k_norm_g, conv_w=conv_w, lru_conv_w=lru_conv_w, lru_conv_b=lru_conv_b,
                  lru_wa=lru_wa, lru_ba=lru_ba, lru_wx=lru_wx, lru_bx=lru_bx, lru_lam=lru_lam,
                  w_mix_out=w_mix_out, ln1_g=ln1_g, ln1_b=ln1_b, xq_w=xq_w, xkv_w=xkv_w, xo_w=xo_w,
                  ln2_g=ln2_g, ln2_b=ln2_b, router_w=router_w, router_b=router_b, w1=w1, b1=b1,
                  w2=w2, b2=b2, ln3_g=ln3_g, ln3_b=ln3_b)
    assert x_prompt.shape[1:] == x_sample.shape[1:]
    return _encode(x_prompt, x_sample, jnp.concatenate([mem_prompt, mem_sample], axis=0), params)
```
